```python
import jax, jax.numpy as jnp
from jax import lax
import numpy as np

D_MODEL = 1024
BATCH = 8
SEQ = 2048
DEPTH = 4
DEC_BATCH = 128
DEC_SEQ = 1
PAST_LEN = 16384
PAGE_SIZE = 128

D_CONV = D_MODEL // 2
CONV_W = 3
RET_HEADS = 4
RET_DK = 128
RET_DV = 128
D_RET = RET_HEADS * RET_DK
RET_CHUNK = 128
ROPE_BASE = 10000.0
D_FF = 3584
N_EXPERTS = 8
TOP_K = 2
D_EXPERT = 3584
D_PLE = 256
EPS = 1e-6
N_IN = 3 * D_CONV + 4 * D_RET + 2 * D_MODEL
N_DENSE = (DEPTH + 1) // 2
N_MOE = DEPTH // 2

kernel_name = "hybrid_conv_retention_moe_decoder_step"


def rms_norm(x, g):
    x32 = x.astype(jnp.float32)
    y = x32 * lax.rsqrt(jnp.mean(x32 * x32, axis=-1, keepdims=True) + EPS) * g.astype(jnp.float32)
    return y.astype(x.dtype)


def rope(x, pos):
    half = x.shape[-1] // 2
    inv = ROPE_BASE ** (-jnp.arange(half, dtype=jnp.float32) / half)
    ang = pos[:, None] * inv[None, :]
    cos = jnp.cos(ang)[None, :, None, :]
    sin = jnp.sin(ang)[None, :, None, :]
    x1, x2 = x[..., :half], x[..., half:]
    return jnp.concatenate([x1 * cos - x2 * sin, x2 * cos + x1 * sin], axis=-1)


def short_conv(u, buf, w):
    T = u.shape[1]
    full = jnp.concatenate([buf.astype(u.dtype), u], axis=1)
    y = w[0] * full[:, 0:T]
    for j in range(1, CONV_W):
        y = y + w[j] * full[:, j:j + T]
    return y, full[:, T:]


def retention_chunk(q, k, v, state, log_gamma):
    L = q.shape[1]
    idx = jnp.arange(L, dtype=jnp.float32)
    diff = idx[:, None] - idx[None, :]
    mask = diff >= 0
    decay = jnp.where(mask[None], jnp.exp(jnp.where(mask, diff, 0.0)[None] * log_gamma[:, None, None]), 0.0)
    scores = jnp.einsum('blhd,bmhd->bhlm', q, k) * decay[None]
    intra = jnp.einsum('bhlm,bmhe->blhe', scores, v)
    q_dec = jnp.exp((idx + 1.0)[:, None] * log_gamma[None, :])
    inter = jnp.einsum('blhd,bhde->blhe', q, state) * q_dec[None, :, :, None]
    k_dec = jnp.exp((L - 1.0 - idx)[:, None] * log_gamma[None, :])
    new_state = jnp.exp(L * log_gamma)[None, :, None, None] * state + jnp.einsum(
        'blhd,blhe->bhde', k * k_dec[None, :, :, None], v)
    return intra + inter, new_state


def retention(q, k, v, state, log_gamma):
    b, T, h, dk = q.shape
    dv = v.shape[-1]
    chunk = RET_CHUNK if T % RET_CHUNK == 0 else T
    n = T // chunk
    if n == 1:
        return retention_chunk(q, k, v, state, log_gamma)
    to_chunks = lambda a: a.reshape(b, n, chunk, h, a.shape[-1]).transpose(1, 0, 2, 3, 4)

    def step(s, inp):
        qc, kc, vc = inp
        o, s2 = retention_chunk(qc, kc, vc, s, log_gamma)
        return s2, o

    s_final, o = lax.scan(step, state, (to_chunks(q), to_chunks(k), to_chunks(v)))
    return o.transpose(1, 0, 2, 3, 4).reshape(b, T, h, dv), s_final


def swiglu(h, w_gu, w_down):
    g, u = jnp.split(h @ w_gu, 2, axis=-1)
    return (jax.nn.silu(g) * u) @ w_down


def moe(h, w_router, w_gu, w_down):
    logits = (h @ w_router).astype(jnp.float32)
    top_v, top_i = lax.top_k(logits, TOP_K)
    top_w = jax.nn.softmax(top_v, axis=-1)
    gates = jnp.sum(jax.nn.one_hot(top_i, N_EXPERTS, dtype=jnp.float32) * top_w[..., None], axis=-2)
    gates = gates.astype(h.dtype)
    out = jnp.zeros_like(h)
    for e in range(N_EXPERTS):
        out = out + gates[:, e:e + 1] * swiglu(h, w_gu[e], w_down[e])
    return out


def mixer(xn, conv_buf, ret_state, pos, w_in, conv_w, w_conv_out, w_ret_out, w_o, log_gamma):
    b, T, _ = xn.shape
    z = xn @ w_in
    cuts = [D_CONV, 2 * D_CONV, 3 * D_CONV,
            3 * D_CONV + D_RET, 3 * D_CONV + 2 * D_RET, 3 * D_CONV + 3 * D_RET, 3 * D_CONV + 4 * D_RET,
            3 * D_CONV + 4 * D_RET + D_MODEL]
    cb, cc, ch, q, k, v, g, ga, gr = jnp.split(z, cuts, axis=-1)
    conv_out, new_buf = short_conv(cc * ch, conv_buf, conv_w)
    a = (cb * conv_out) @ w_conv_out
    q = rope(q.reshape(b, T, RET_HEADS, RET_DK).astype(jnp.float32), pos)
    k = rope(k.reshape(b, T, RET_HEADS, RET_DK).astype(jnp.float32), pos) * (RET_DK ** -0.5)
    v = v.reshape(b, T, RET_HEADS, RET_DV).astype(jnp.float32)
    o, new_state = retention(q, k, v, ret_state.astype(jnp.float32), log_gamma)
    mu = jnp.mean(o, axis=-1, keepdims=True)
    var = jnp.mean(jnp.square(o - mu), axis=-1, keepdims=True)
    o = ((o - mu) * lax.rsqrt(var + EPS)).reshape(b, T, RET_HEADS * RET_DV)
    r = (jax.nn.silu(g.astype(jnp.float32)) * o).astype(xn.dtype) @ w_ret_out
    m = jax.nn.sigmoid(ga) * a + jax.nn.sigmoid(gr) * r
    return m @ w_o, new_buf, new_state


def trunk(x, p, conv_bufs, ret_states, pos0, weights):
    (norm_mix_g, w_in, conv_w, w_conv_out, w_ret_out, w_o, norm_ffn_g,
     w_dense_gu, w_dense_down, w_router, w_exp_gu, w_exp_down,
     norm_ple_g, w_ple, w_ple_gate, norm_final_g) = weights
    b, T, D = x.shape
    pos = pos0 + jnp.arange(T, dtype=jnp.float32)
    log_gamma = jnp.log(1.0 - 2.0 ** (-5.0 - jnp.arange(RET_HEADS, dtype=jnp.float32)))
    new_bufs, new_states = [], []
    for i in range(DEPTH):
        mix, nb, ns = mixer(rms_norm(x, norm_mix_g[i]), conv_bufs[i], ret_states[i], pos,
                            w_in[i], conv_w[i], w_conv_out[i], w_ret_out[i], w_o[i], log_gamma)
        new_bufs.append(nb)
        new_states.append(ns)
        x = x + mix
        hn = rms_norm(x, norm_ffn_g[i])
        if i % 2 == 0:
            f = swiglu(hn, w_dense_gu[i // 2], w_dense_down[i // 2])
        else:
            f = moe(hn.reshape(b * T, D), w_router[i // 2], w_exp_gu[i // 2],
                    w_exp_down[i // 2]).reshape(b, T, D)
        x = x + f
        gate = jax.nn.sigmoid(rms_norm(x, norm_ple_g[i]) @ w_ple_gate[i])
        x = x + gate * (p[i] @ w_ple[i])
    return rms_norm(x, norm_final_g), jnp.stack(new_bufs), jnp.stack(new_states)


def setup_inputs(seed: int = 0) -> dict:
    key = jax.random.key(seed)
    ks = jax.random.split(key, 32)
    f32 = jnp.float32
    nrm = lambda k, shape, scale: jax.random.normal(k, shape, f32) * scale
    gain = lambda k, shape: 1.0 + 0.02 * jax.random.normal(k, shape, f32)
    return {
        "x_prompt": nrm(ks[0], (BATCH, SEQ, D_MODEL), 1.0),
        "x_sample": nrm(ks[1], (DEC_BATCH, DEC_SEQ, D_MODEL), 1.0),
        "state_conv": nrm(ks[2], (DEPTH, DEC_BATCH, CONV_W - 1, D_CONV), 1.0),
        "state_ret": nrm(ks[3], (DEPTH, DEC_BATCH, RET_HEADS, RET_DK, RET_DV), 0.1),
        "p_prompt": nrm(ks[4], (DEPTH, BATCH, SEQ, D_PLE), 1.0),
        "p_sample": nrm(ks[5], (DEPTH, DEC_BATCH, DEC_SEQ, D_PLE), 1.0),
        "norm_mix_g": gain(ks[6], (DEPTH, D_MODEL)),
        "w_in": nrm(ks[7], (DEPTH, D_MODEL, N_IN), D_MODEL ** -0.5),
        "conv_w": nrm(ks[8], (DEPTH, CONV_W, D_CONV), CONV_W ** -0.5),
        "w_conv_out": nrm(ks[9], (DEPTH, D_CONV, D_MODEL), D_CONV ** -0.5),
        "w_ret_out": nrm(ks[10], (DEPTH, RET_HEADS * RET_DV, D_MODEL), (RET_HEADS * RET_DV) ** -0.5),
        "w_o": nrm(ks[11], (DEPTH, D_MODEL, D_MODEL), D_MODEL ** -0.5),
        "norm_ffn_g": gain(ks[12], (DEPTH, D_MODEL)),
        "w_dense_gu": nrm(ks[13], (N_DENSE, D_MODEL, 2 * D_FF), D_MODEL ** -0.5),
        "w_dense_down": nrm(ks[14], (N_DENSE, D_FF, D_MODEL), D_FF ** -0.5),
        "w_router": nrm(ks[15], (N_MOE, D_MODEL, N_EXPERTS), D_MODEL ** -0.5),
        "w_exp_gu": nrm(ks[16], (N_MOE, N_EXPERTS, D_MODEL, 2 * D_EXPERT), D_MODEL ** -0.5),
        "w_exp_down": nrm(ks[17], (N_MOE, N_EXPERTS, D_EXPERT, D_MODEL), D_EXPERT ** -0.5),
        "norm_ple_g": gain(ks[18], (DEPTH, D_MODEL)),
        "w_ple": nrm(ks[19], (DEPTH, D_PLE, D_MODEL), D_PLE ** -0.5),
        "w_ple_gate": nrm(ks[20], (DEPTH, D_MODEL, D_MODEL), D_MODEL ** -0.5),
        "norm_final_g": gain(ks[21], (D_MODEL,)),
    }


def reference(x_prompt, x_sample, state_conv, state_ret, p_prompt, p_sample,
              norm_mix_g, w_in, conv_w, w_conv_out, w_ret_out, w_o, norm_ffn_g,
              w_dense_gu, w_dense_down, w_router, w_exp_gu, w_exp_down,
              norm_ple_g, w_ple, w_ple_gate, norm_final_g):
    weights = (norm_mix_g, w_in, conv_w, w_conv_out, w_ret_out, w_o, norm_ffn_g,
               w_dense_gu, w_dense_down, w_router, w_exp_gu, w_exp_down,
               norm_ple_g, w_ple, w_ple_gate, norm_final_g)
    b = x_prompt.shape[0]
    zero_conv = jnp.zeros((DEPTH, b, CONV_W - 1, D_CONV), x_prompt.dtype)
    zero_ret = jnp.zeros((DEPTH, b, RET_HEADS, RET_DK, RET_DV), jnp.float32)
    y_prompt, conv_prompt, ret_prompt = trunk(x_prompt, p_prompt, zero_conv, zero_ret, 0.0, weights)
    y_sample, conv_sample, ret_sample = trunk(x_sample, p_sample, state_conv, state_ret,
                                              float(PAST_LEN), weights)
    return (y_prompt, y_sample, conv_prompt, ret_prompt, conv_sample, ret_sample)
```

```python
import functools
import math

import jax
import jax.numpy as jnp
from jax import lax
from jax.experimental import pallas as pl
from jax.experimental.pallas import tpu as pltpu

F32 = jnp.float32
BF16 = jnp.bfloat16

CONV_W = 3
TOP_K = 2
ROPE_BASE = 10000.0
EPS = 1e-6
PAST_LEN = 16384

V7X_VMEM_BYTES = 64 * 1024 * 1024
VMEM_LIMIT = V7X_VMEM_BYTES - 8 * 1024 * 1024

MIX_CHUNK = 256
FFN_TM = 1024
FFN_FC = 512
MOE_TILE = 2048
MOE_ROWS = 256
MOE_FC = 896
PLE_TM = 512
DEC_BB = 16


def _cparams(sem):
    return pltpu.CompilerParams(dimension_semantics=sem, vmem_limit_bytes=VMEM_LIMIT)


def _dot(a, b):
    return jnp.dot(a, b, preferred_element_type=F32)


def _dot_nt(a, b, precision=None):
    return lax.dot_general(a, b, (((1,), (1,)), ((), ())), precision=precision,
                           preferred_element_type=F32)


def _dot_tn(a, b):
    return lax.dot_general(a, b, (((0,), (0,)), ((), ())), preferred_element_type=F32)


def _rms(x, g):
    return x * lax.rsqrt(jnp.mean(x * x, axis=-1, keepdims=True) + EPS) * g


def _sigmoid(x):
    return 1.0 / (1.0 + jnp.exp(-x))


def _silu(x):
    return x * _sigmoid(x)


def _log_gammas(n_heads):
    return tuple(math.log(1.0 - 2.0 ** (-5.0 - h)) for h in range(n_heads))


def _rope_table_kernel(pos_ref, inv_ref, sgn_ref, cos_ref, sin_ref):
    ang = pos_ref[...] * inv_ref[...]
    cos_ref[...] = jnp.cos(ang)
    sin_ref[...] = jnp.sin(ang) * sgn_ref[...]


def _rope_tables(pos, inv_dup, sgn):
    n = pos.shape[0]
    dk = inv_dup.shape[1]
    return pl.pallas_call(
        _rope_table_kernel,
        out_shape=(jax.ShapeDtypeStruct((n, dk), F32), jax.ShapeDtypeStruct((n, dk), F32)),
        name="rope_tables",
    )(pos, inv_dup, sgn)


def _mixer_prompt_kernel(x_ref, cos_ref, sin_ref, g_ref, win_ref, cw_ref, wco_ref, wro_ref,
                         wo_ref, out_ref, cst_ref, rst_ref, tail_ref, state_ref, decay_ref,
                         *, L, H, DK, DV, DC, D, log_gamma):
    c = pl.program_id(1)
    DR = H * DK

    @pl.when(c == 0)
    def _init():
        tail_ref[...] = jnp.zeros_like(tail_ref)
        state_ref[...] = jnp.zeros_like(state_ref)
        diff = (lax.broadcasted_iota(jnp.int32, (L, L), 0)
                - lax.broadcasted_iota(jnp.int32, (L, L), 1)).astype(F32)
        for h in range(H):
            decay_ref[h] = jnp.where(diff >= 0.0,
                                     jnp.exp(jnp.maximum(diff, 0.0) * log_gamma[h]), 0.0)

    x = x_ref[...]
    xn = _rms(x, g_ref[...]).astype(BF16)

    zc = _dot(xn, win_ref[:, 0:3 * DC])
    cb, u = zc[:, 0:DC], zc[:, DC:2 * DC] * zc[:, 2 * DC:3 * DC]
    tail = tail_ref[...]
    p1, p2 = tail[7:8, :], tail[6:7, :]
    row = lax.broadcasted_iota(jnp.int32, (L, DC), 0)
    u1 = jnp.where(row == 0, p1, pltpu.roll(u, 1, axis=0))
    u2 = jnp.where(row == 0, p2, jnp.where(row == 1, p1, pltpu.roll(u, 2, axis=0)))
    cw = cw_ref[...]
    conv = cw[0:1, :] * u2 + cw[1:2, :] * u1 + cw[2:3, :] * u
    tail_ref[...] = u[L - 8:L, :]
    a = _dot((cb * conv).astype(BF16), wco_ref[...])

    zr = _dot(xn, win_ref[:, 3 * DC:3 * DC + 4 * DR])
    cos, sin = cos_ref[...], sin_ref[...]
    ridx_k = lax.broadcasted_iota(jnp.int32, (L, DK), 0).astype(F32)
    ridx_v = lax.broadcasted_iota(jnp.int32, (L, DV), 0).astype(F32)
    heads = []
    for h in range(H):
        lg = log_gamma[h]
        qh = zr[:, h * DK:(h + 1) * DK]
        kh = zr[:, DR + h * DK:DR + (h + 1) * DK]
        vh = zr[:, 2 * DR + h * DV:2 * DR + (h + 1) * DV]
        gh = zr[:, 3 * DR + h * DV:3 * DR + (h + 1) * DV]
        qh = qh * cos + pltpu.roll(qh, DK // 2, axis=1) * sin
        kh = (kh * cos + pltpu.roll(kh, DK // 2, axis=1) * sin) * (DK ** -0.5)
        qb, vb = qh.astype(BF16), vh.astype(BF16)
        s_prev = state_ref[h]
        scores = _dot_nt(qb, kh.astype(BF16)) * decay_ref[h]
        o = _dot(scores.astype(BF16), vb)
        o = o + _dot(qb, s_prev.astype(BF16)) * jnp.exp((ridx_v + 1.0) * lg)
        k_dec = jnp.exp((L - 1.0 - ridx_k) * lg)
        state_ref[h] = math.exp(L * lg) * s_prev + _dot_tn((kh * k_dec).astype(BF16), vb)
        mu = jnp.mean(o, axis=-1, keepdims=True)
        d = o - mu
        var = jnp.mean(d * d, axis=-1, keepdims=True)
        heads.append(_silu(gh) * (d * lax.rsqrt(var + EPS)))
    r = _dot(jnp.concatenate(heads, axis=-1).astype(BF16), wro_ref[...])

    zg = _dot(xn, win_ref[:, 3 * DC + 4 * DR:3 * DC + 4 * DR + 2 * D])
    m = _sigmoid(zg[:, 0:D]) * a + _sigmoid(zg[:, D:2 * D]) * r
    out_ref[...] = x + _dot(m.astype(BF16), wo_ref[...])

    @pl.when(c == pl.num_programs(1) - 1)
    def _final():
        cst_ref[...] = u[L - 8:L, :]
        rst_ref[...] = state_ref[...]


def _mixer_prompt(x, cos, sin, g, w_in, cw, w_co, w_ro, w_o, *, H, DK, DV):
    B, T, D = x.shape
    DC = cw.shape[1]
    L = MIX_CHUNK
    const = lambda *shape: pl.BlockSpec(shape, lambda b, c: (0,) * len(shape))
    kern = functools.partial(_mixer_prompt_kernel, L=L, H=H, DK=DK, DV=DV, DC=DC, D=D,
                             log_gamma=_log_gammas(H))
    return pl.pallas_call(
        kern,
        grid=(B, T // L),
        in_specs=[
            pl.BlockSpec((None, L, D), lambda b, c: (b, c, 0)),
            pl.BlockSpec((L, DK), lambda b, c: (c, 0)),
            pl.BlockSpec((L, DK), lambda b, c: (c, 0)),
            const(1, D), const(*w_in.shape), const(*cw.shape), const(*w_co.shape),
            const(*w_ro.shape), const(*w_o.shape),
        ],
        out_specs=[
            pl.BlockSpec((None, L, D), lambda b, c: (b, c, 0)),
            pl.BlockSpec((None, 8, DC), lambda b, c: (b, 0, 0)),
            pl.BlockSpec((None, H, DK, DV), lambda b, c: (b, 0, 0, 0)),
        ],
        out_shape=(jax.ShapeDtypeStruct((B, T, D), F32),
                   jax.ShapeDtypeStruct((B, 8, DC), F32),
                   jax.ShapeDtypeStruct((B, H, DK, DV), F32)),
        scratch_shapes=[pltpu.VMEM((8, DC), F32), pltpu.VMEM((H, DK, DV), F32),
                        pltpu.VMEM((H, L, L), F32)],
        compiler_params=_cparams(("arbitrary", "arbitrary")),
        name="mixer_prompt",
    )(x, cos, sin, g, w_in, cw, w_co, w_ro, w_o)


def _dec_inproj_kernel(x_ref, g_ref, wm_ref, wg_ref, wqkt_ref, invt_ref, zm_ref, zg_ref,
                       qkt_ref, *, H, DK, pos0):
    xn = _rms(x_ref[...], g_ref[...]).astype(BF16)
    zm_ref[...] = _dot(xn, wm_ref[...])
    zg_ref[...] = _dot(xn, wg_ref[...])
    qkt = _dot_nt(wqkt_ref[...], xn)
    ang = pos0 * invt_ref[...]
    cos, sin = jnp.cos(ang), jnp.sin(ang)
    half = DK // 2
    for hh in range(2 * H):
        blk = qkt[hh * DK:(hh + 1) * DK, :]
        x1, x2 = blk[0:half, :], blk[half:DK, :]
        scale = 1.0 if hh < H else DK ** -0.5
        qkt_ref[hh * DK:hh * DK + half, :] = (x1 * cos[0:half] - x2 * sin[0:half]) * scale
        qkt_ref[hh * DK + half:(hh + 1) * DK, :] = (x2 * cos[half:DK] + x1 * sin[half:DK]) * scale


def _dec_inproj(x, g, w_main, w_gate, w_qkt, inv_t, *, H, DK):
    n, D = x.shape
    kern = functools.partial(_dec_inproj_kernel, H=H, DK=DK, pos0=float(PAST_LEN))
    return pl.pallas_call(
        kern,
        out_shape=(jax.ShapeDtypeStruct((n, w_main.shape[1]), F32),
                   jax.ShapeDtypeStruct((n, w_gate.shape[1]), F32),
                   jax.ShapeDtypeStruct((w_qkt.shape[0], n), F32)),
        compiler_params=pltpu.CompilerParams(vmem_limit_bytes=VMEM_LIMIT),
        name="dec_inproj",
    )(x, g, w_main, w_gate, w_qkt, inv_t)


def _dec_state_kernel(zm_ref, qkt_ref, conv_ref, cw_ref, s_ref, ain_ref, rin_ref, nconv_ref,
                      ns_ref, o_scr, *, BB, H, DK, DV, DC, log_gamma):
    DR = H * DK
    zm = zm_ref[...]
    cb, u = zm[:, 0:DC], zm[:, DC:2 * DC] * zm[:, 2 * DC:3 * DC]
    buf = conv_ref[...]
    b0, b1 = buf[:, 0:DC], buf[:, DC:2 * DC]
    cw = cw_ref[...]
    conv = cw[0:1, :] * b0 + cw[1:2, :] * b1 + cw[2:3, :] * u
    nconv_ref[:, 0:DC] = b1
    nconv_ref[:, DC:2 * DC] = u
    ain_ref[...] = cb * conv

    qkt = qkt_ref[...]
    v = zm[:, 3 * DC + 2 * DR:3 * DC + 3 * DR]
    g = zm[:, 3 * DC + 3 * DR:3 * DC + 4 * DR]
    for j in range(BB):
        for h in range(H):
            gamma = math.exp(log_gamma[h])
            qc = qkt[h * DK:(h + 1) * DK, j:j + 1]
            kc = qkt[DR + h * DK:DR + (h + 1) * DK, j:j + 1]
            s_prev = s_ref[j, h]
            vrow = v[j:j + 1, h * DV:(h + 1) * DV]
            qk = jnp.sum(qc * kc, axis=0, keepdims=True)
            inter = jnp.sum(qc * s_prev, axis=0, keepdims=True) * gamma
            o_scr[j:j + 1, h * DV:(h + 1) * DV] = qk * vrow + inter
            ns_ref[j, h] = gamma * s_prev + kc * vrow
    o = o_scr[...]
    for h in range(H):
        oh = o[:, h * DV:(h + 1) * DV]
        mu = jnp.mean(oh, axis=-1, keepdims=True)
        d = oh - mu
        var = jnp.mean(d * d, axis=-1, keepdims=True)
        rin_ref[:, h * DV:(h + 1) * DV] = _silu(g[:, h * DV:(h + 1) * DV]) * (d * lax.rsqrt(var + EPS))


def _dec_state(zm, qkt3, conv2d, cw, state_all, layer, *, H, DK, DV):
    n = zm.shape[0]
    DC = cw.shape[1]
    BB = DEC_BB
    kern = functools.partial(_dec_state_kernel, BB=BB, H=H, DK=DK, DV=DV, DC=DC,
                             log_gamma=_log_gammas(H))
    return pl.pallas_call(
        kern,
        grid=(n // BB,),
        in_specs=[
            pl.BlockSpec((BB, zm.shape[1]), lambda b: (b, 0)),
            pl.BlockSpec((None, qkt3.shape[1], BB), lambda b: (b, 0, 0)),
            pl.BlockSpec((BB, 2 * DC), lambda b: (b, 0)),
            pl.BlockSpec(cw.shape, lambda b: (0, 0)),
            pl.BlockSpec((None, BB, H, DK, DV), lambda b: (layer, b, 0, 0, 0)),
        ],
        out_specs=[
            pl.BlockSpec((BB, DC), lambda b: (b, 0)),
            pl.BlockSpec((BB, H * DV), lambda b: (b, 0)),
            pl.BlockSpec((BB, 2 * DC), lambda b: (b, 0)),
            pl.BlockSpec((BB, H, DK, DV), lambda b: (b, 0, 0, 0)),
        ],
        out_shape=(jax.ShapeDtypeStruct((n, DC), F32),
                   jax.ShapeDtypeStruct((n, H * DV), F32),
                   jax.ShapeDtypeStruct((n, 2 * DC), F32),
                   jax.ShapeDtypeStruct((n, H, DK, DV), F32)),
        scratch_shapes=[pltpu.VMEM((BB, H * DV), F32)],
        compiler_params=_cparams(("arbitrary",)),
        name="dec_state",
    )(zm, qkt3, conv2d, cw, state_all)


def _dec_outproj_kernel(x_ref, zg_ref, ain_ref, rin_ref, wco_ref, wro_ref, wo_ref, out_ref, *, D):
    a = _dot(ain_ref[...].astype(BF16), wco_ref[...])
    r = _dot(rin_ref[...].astype(BF16), wro_ref[...])
    zg = zg_ref[...]
    m = _sigmoid(zg[:, 0:D]) * a + _sigmoid(zg[:, D:2 * D]) * r
    out_ref[...] = x_ref[...] + _dot(m.astype(BF16), wo_ref[...])


def _dec_outproj(x, zg, ain, rin, w_co, w_ro, w_o):
    n, D = x.shape
    return pl.pallas_call(
        functools.partial(_dec_outproj_kernel, D=D),
        out_shape=jax.ShapeDtypeStruct((n, D), F32),
        compiler_params=pltpu.CompilerParams(vmem_limit_bytes=VMEM_LIMIT),
        name="dec_outproj",
    )(x, zg, ain, rin, w_co, w_ro, w_o)


def _ffn_dense_kernel(x_ref, g_ref, wg_ref, wu_ref, wd_ref, out_ref, hn_ref):
    @pl.when(pl.program_id(1) == 0)
    def _init():
        x = x_ref[...]
        hn_ref[...] = _rms(x, g_ref[...]).astype(BF16)
        out_ref[...] = x

    hn = hn_ref[...]
    act = (_silu(_dot(hn, wg_ref[...])) * _dot(hn, wu_ref[...])).astype(BF16)
    out_ref[...] += _dot(act, wd_ref[...])


def _ffn_dense(x, g, w_gu, w_down):
    n, D = x.shape
    FF = w_down.shape[0]
    tm = min(FFN_TM, n)
    fc = FFN_FC
    nf = FF // fc
    return pl.pallas_call(
        _ffn_dense_kernel,
        grid=(n // tm, nf),
        in_specs=[
            pl.BlockSpec((tm, D), lambda i, j: (i, 0)),
            pl.BlockSpec((1, D), lambda i, j: (0, 0)),
            pl.BlockSpec((D, fc), lambda i, j: (0, j)),
            pl.BlockSpec((D, fc), lambda i, j: (0, nf + j)),
            pl.BlockSpec((fc, D), lambda i, j: (j, 0)),
        ],
        out_specs=pl.BlockSpec((tm, D), lambda i, j: (i, 0)),
        out_shape=jax.ShapeDtypeStruct((n, D), F32),
        scratch_shapes=[pltpu.VMEM((tm, D), BF16)],
        compiler_params=_cparams(("arbitrary", "arbitrary")),
        name="ffn_dense",
    )(x, g, w_gu, w_gu, w_down)


def _router_kernel(x_ref, g_ref, wrt_ref, hn_ref, slot_ref, gate_ref, cnt_ref, *, T, E, CH):
    hn = _rms(x_ref[...], g_ref[...])
    hn_ref[...] = hn.astype(BF16)
    logits = _dot_nt(wrt_ref[...], hn, precision=lax.Precision.HIGHEST)
    eidx = lax.broadcasted_iota(jnp.int32, (E, T), 0)
    m1 = jnp.max(logits, axis=0, keepdims=True)
    i1 = jnp.min(jnp.where(logits == m1, eidx, E), axis=0, keepdims=True)
    sel1 = eidx == i1
    rest = jnp.where(sel1, -jnp.inf, logits)
    m2 = jnp.max(rest, axis=0, keepdims=True)
    i2 = jnp.min(jnp.where(rest == m2, eidx, E), axis=0, keepdims=True)
    sel2 = eidx == i2
    e2 = jnp.exp(m2 - m1)
    w1 = 1.0 / (1.0 + e2)
    gate_ref[...] = jnp.where(sel1, w1, 0.0) + jnp.where(sel2, e2 * w1, 0.0)
    sel = jnp.where(sel1 | sel2, 1.0, 0.0)
    selb = sel.astype(BF16)
    for cidx in range(T // CH):
        src = lax.broadcasted_iota(jnp.int32, (T, CH), 0)
        dst = lax.broadcasted_iota(jnp.int32, (T, CH), 1) + cidx * CH
        tri = jnp.where(src < dst, 1.0, 0.0).astype(BF16)
        pos = _dot(selb, tri)
        s = sel[:, cidx * CH:(cidx + 1) * CH]
        slot_ref[:, cidx * CH:(cidx + 1) * CH] = jnp.where(s > 0.0, pos, -1.0)
    cnt = jnp.sum(sel, axis=1, keepdims=True)
    cnt_ref[...] = jnp.broadcast_to(cnt, cnt_ref.shape).astype(jnp.int32)


def _router(x, g, w_router_t, T):
    n, D = x.shape
    E = w_router_t.shape[0]
    nt = n // T
    kern = functools.partial(_router_kernel, T=T, E=E, CH=min(256, T))
    return pl.pallas_call(
        kern,
        grid=(nt,),
        in_specs=[
            pl.BlockSpec((T, D), lambda i: (i, 0)),
            pl.BlockSpec((1, D), lambda i: (0, 0)),
            pl.BlockSpec((E, D), lambda i: (0, 0)),
        ],
        out_specs=[
            pl.BlockSpec((T, D), lambda i: (i, 0)),
            pl.BlockSpec((E, T), lambda i: (0, i)),
            pl.BlockSpec((E, T), lambda i: (0, i)),
            pl.BlockSpec((None, E, 128), lambda i: (i, 0, 0)),
        ],
        out_shape=(jax.ShapeDtypeStruct((n, D), BF16),
                   jax.ShapeDtypeStruct((E, n), F32),
                   jax.ShapeDtypeStruct((E, n), F32),
                   jax.ShapeDtypeStruct((nt, E, 128), jnp.int32)),
        compiler_params=_cparams(("arbitrary",)),
        name="moe_router",
    )(x, g, w_router_t)


def _moe_kernel(cnt_ref, hn_ref, slot_ref, gate_ref, wg_ref, wu_ref, wd_ref, out_ref,
                xg_ref, yacc_ref, *, T, R, E):
    i, e, j = pl.program_id(0), pl.program_id(1), pl.program_id(2)
    nch = lax.shift_right_logical(cnt_ref[i * E + e] + (R - 1), R.bit_length() - 1)
    slot = slot_ref[...]

    def hit(c):
        rows = (lax.broadcasted_iota(jnp.int32, (R, T), 0) + c * R).astype(F32)
        return slot == rows

    def rows_of(c):
        return pl.ds(pl.multiple_of(c * R, R), R)

    @pl.when((e == 0) & (j == 0))
    def _zero():
        out_ref[...] = jnp.zeros_like(out_ref)

    @pl.when(j == 0)
    def _gather():
        def body(c, carry):
            onehot = jnp.where(hit(c), 1.0, 0.0).astype(BF16)
            xg_ref[rows_of(c), :] = _dot(onehot, hn_ref[...]).astype(BF16)
            return carry
        lax.fori_loop(0, nch, body, 0)

    def expert_chunk(c):
        xs = xg_ref[rows_of(c), :]
        act = (_silu(_dot(xs, wg_ref[...])) * _dot(xs, wu_ref[...])).astype(BF16)
        return _dot(act, wd_ref[...])

    @pl.when(j == 0)
    def _first():
        def body(c, carry):
            yacc_ref[rows_of(c), :] = expert_chunk(c)
            return carry
        lax.fori_loop(0, nch, body, 0)

    @pl.when(j > 0)
    def _rest():
        def body(c, carry):
            yacc_ref[rows_of(c), :] += expert_chunk(c)
            return carry
        lax.fori_loop(0, nch, body, 0)

    @pl.when(j == pl.num_programs(2) - 1)
    def _scatter():
        gate = gate_ref[...]
        def body(c, carry):
            m = hit(c)
            onehot = jnp.where(m, 1.0, 0.0).astype(BF16)
            grow = jnp.sum(jnp.where(m, gate, 0.0), axis=1, keepdims=True)
            yw = (yacc_ref[rows_of(c), :] * grow).astype(BF16)
            out_ref[...] += _dot_tn(onehot, yw)
            return carry
        lax.fori_loop(0, nch, body, 0)


def _moe(hn, slot4, gate4, counts, w_gu, w_down, T):
    n, D = hn.shape
    E, FF = w_down.shape[0], w_down.shape[1]
    R = min(MOE_ROWS, T)
    fc = MOE_FC
    nf = FF // fc
    nt = n // T
    kern = functools.partial(_moe_kernel, T=T, R=R, E=E)
    grid_spec = pltpu.PrefetchScalarGridSpec(
        num_scalar_prefetch=1,
        grid=(nt, E, nf),
        in_specs=[
            pl.BlockSpec((T, D), lambda i, e, j, cnt: (i, 0)),
            pl.BlockSpec((None, None, 1, T), lambda i, e, j, cnt: (i, e, 0, 0)),
            pl.BlockSpec((None, None, 1, T), lambda i, e, j, cnt: (i, e, 0, 0)),
            pl.BlockSpec((None, D, fc), lambda i, e, j, cnt: (e, 0, j)),
            pl.BlockSpec((None, D, fc), lambda i, e, j, cnt: (e, 0, nf + j)),
            pl.BlockSpec((None, fc, D), lambda i, e, j, cnt: (e, j, 0)),
        ],
        out_specs=pl.BlockSpec((T, D), lambda i, e, j, cnt: (i, 0)),
        scratch_shapes=[pltpu.VMEM((T, D), BF16), pltpu.VMEM((T, D), F32)],
    )
    return pl.pallas_call(
        kern,
        grid_spec=grid_spec,
        out_shape=jax.ShapeDtypeStruct((n, D), F32),
        compiler_params=_cparams(("arbitrary", "arbitrary", "arbitrary")),
        name="moe_experts",
    )(counts, hn, slot4, gate4, w_gu, w_gu, w_down)


def _moe_layer(x, g, w_router_t, w_gu, w_down):
    n = x.shape[0]
    E = w_router_t.shape[0]
    T = min(MOE_TILE, n)
    nt = n // T
    hn, slot, gate, cnt = _router(x, g, w_router_t, T)
    to4 = lambda a: a.reshape(E, nt, 1, T).transpose(1, 0, 2, 3)
    counts = cnt[:, :, 0].reshape(nt * E)
    return _moe(hn, to4(slot), to4(gate), counts, w_gu, w_down, T)


def _ple_kernel(*refs, has_f, final):
    refs = list(refs)
    x_ref = refs.pop(0)
    f_ref = refs.pop(0) if has_f else None
    p_ref, g_ref, wgate_ref, wple_ref = refs[0:4]
    gfin_ref = refs[4] if final else None
    out_ref = refs[-1]
    x = x_ref[...]
    if has_f:
        x = x + f_ref[...]
    gate = _sigmoid(_dot(_rms(x, g_ref[...]).astype(BF16), wgate_ref[...]))
    y = x + gate * _dot(p_ref[...].astype(BF16), wple_ref[...])
    if final:
        y = _rms(y, gfin_ref[...])
    out_ref[...] = y


def _ple(x, f, p_all, layer, g, w_gate, w_ple, g_final):
    n, D = x.shape
    DP = p_all.shape[-1]
    tm = min(PLE_TM, n)
    tok = pl.BlockSpec((tm, D), lambda i: (i, 0))
    const = lambda *shape: pl.BlockSpec(shape, lambda i: (0,) * len(shape))
    args, specs = [x], [tok]
    if f is not None:
        args.append(f)
        specs.append(tok)
    args += [p_all, g, w_gate, w_ple]
    specs += [pl.BlockSpec((None, tm, DP), lambda i: (layer, i, 0)), const(1, D),
              const(D, D), const(DP, D)]
    if g_final is not None:
        args.append(g_final)
        specs.append(const(1, D))
    kern = functools.partial(_ple_kernel, has_f=f is not None, final=g_final is not None)
    return pl.pallas_call(
        kern,
        grid=(n // tm,),
        in_specs=specs,
        out_specs=tok,
        out_shape=jax.ShapeDtypeStruct((n, D), F32),
        compiler_params=_cparams(("arbitrary",)),
        name="ple",
    )(*args)


def kernel(x_prompt, x_sample, state_conv, state_ret, p_prompt, p_sample, norm_mix_g, w_in,
           conv_w, w_conv_out, w_ret_out, w_o, norm_ffn_g, w_dense_gu, w_dense_down, w_router,
           w_exp_gu, w_exp_down, norm_ple_g, w_ple, w_ple_gate, norm_final_g):
    B, T, D = x_prompt.shape
    NS = x_sample.shape[0]
    depth = w_in.shape[0]
    _, _, H, DK, DV = state_ret.shape
    DC = conv_w.shape[-1]
    DR = H * DK
    assert x_sample.shape[1] == 1 and CONV_W - 1 == state_conv.shape[2]
    assert T % MIX_CHUNK == 0 and NS % DEC_BB == 0

    half = DK // 2
    inv = ROPE_BASE ** (-jnp.arange(half, dtype=F32) / half)
    inv_dup = jnp.concatenate([inv, inv]).reshape(1, DK)
    sgn = jnp.concatenate([-jnp.ones((half,), F32), jnp.ones((half,), F32)]).reshape(1, DK)
    pos = jnp.arange(T, dtype=F32).reshape(T, 1)
    cos_p, sin_p = _rope_tables(pos, inv_dup, sgn)
    inv_t = jnp.broadcast_to(inv_dup.reshape(DK, 1), (DK, NS))

    row = lambda a: a.reshape(1, -1)
    xp = x_prompt
    xs = x_sample.reshape(NS, D)
    pp = p_prompt.reshape(depth, B * T, -1)
    ps = p_sample.reshape(depth, NS, -1)
    conv_p, ret_p, conv_s, ret_s = [], [], [], []
    for i in range(depth):
        w_in_b = w_in[i].astype(BF16)
        w_co_b = w_conv_out[i].astype(BF16)
        w_ro_b = w_ret_out[i].astype(BF16)
        w_o_b = w_o[i].astype(BF16)
        g_mix = row(norm_mix_g[i])

        xp, cst, rst = _mixer_prompt(xp, cos_p, sin_p, g_mix, w_in_b, conv_w[i], w_co_b, w_ro_b,
                                     w_o_b, H=H, DK=DK, DV=DV)
        conv_p.append(cst[:, 8 - (CONV_W - 1):, :])
        ret_p.append(rst)

        n_main = 3 * DC + 4 * DR
        w_qkt = w_in_b[:, 3 * DC:3 * DC + 2 * DR].T
        zm, zg, qkt = _dec_inproj(xs, g_mix, w_in_b[:, :n_main], w_in_b[:, n_main:], w_qkt,
                                  inv_t, H=H, DK=DK)
        qkt3 = qkt.reshape(2 * DR, NS // DEC_BB, DEC_BB).transpose(1, 0, 2)
        ain, rin, ncv, nrs = _dec_state(zm, qkt3, state_conv[i].reshape(NS, -1), conv_w[i],
                                        state_ret, i, H=H, DK=DK, DV=DV)
        xs = _dec_outproj(xs, zg, ain, rin, w_co_b, w_ro_b, w_o_b)
        conv_s.append(ncv.reshape(NS, CONV_W - 1, DC))
        ret_s.append(nrs)

        g_ffn = row(norm_ffn_g[i])
        xp2 = xp.reshape(B * T, D)
        if i % 2 == 0:
            w_gu_b = w_dense_gu[i // 2].astype(BF16)
            w_dn_b = w_dense_down[i // 2].astype(BF16)
            xp2 = _ffn_dense(xp2, g_ffn, w_gu_b, w_dn_b)
            xs = _ffn_dense(xs, g_ffn, w_gu_b, w_dn_b)
            fp = fs = None
        else:
            w_rt = w_router[i // 2].T
            w_gu_b = w_exp_gu[i // 2].astype(BF16)
            w_dn_b = w_exp_down[i // 2].astype(BF16)
            fp = _moe_layer(xp2, g_ffn, w_rt, w_gu_b, w_dn_b)
            fs = _moe_layer(xs, g_ffn, w_rt, w_gu_b, w_dn_b)

        g_fin = row(norm_final_g) if i == depth - 1 else None
        w_pg_b = w_ple_gate[i].astype(BF16)
        w_pl_b = w_ple[i].astype(BF16)
        xp = _ple(xp2, fp, pp, i, row(norm_ple_g[i]), w_pg_b, w_pl_b, g_fin).reshape(B, T, D)
        xs = _ple(xs, fs, ps, i, row(norm_ple_g[i]), w_pg_b, w_pl_b, g_fin)

    return (xp, xs.reshape(NS, 1, D), jnp.stack(conv_p), jnp.stack(ret_p),
            jnp.stack(conv_s), jnp.stack(ret_s))
```

```python
import functools
import math

import jax
import jax.numpy as jnp
from jax import lax
from jax.experimental import pallas as pl
from jax.experimental.pallas import tpu as pltpu

F32 = jnp.float32
BF16 = jnp.bfloat16

CONV_W = 3
TOP_K = 2
ROPE_BASE = 10000.0
EPS = 1e-6
PAST_LEN = 16384

V7X_VMEM_BYTES = 64 * 1024 * 1024
VMEM_LIMIT = V7X_VMEM_BYTES - 8 * 1024 * 1024

MIX_CHUNK = 256
FFN_TM = 1024
FFN_FC = 512
ROUTE_TILE = 2048
MOE_TM = 1024
MOE_ROWS = 256
MOE_FC = 896
MOE_TD = 512
PLE_TM = 512
DEC_BB = 16


def _cparams(sem):
    return pltpu.CompilerParams(dimension_semantics=sem, vmem_limit_bytes=VMEM_LIMIT)


def _dot(a, b):
    return jnp.dot(a, b, preferred_element_type=F32)


def _dot_nt(a, b, precision=None):
    return lax.dot_general(a, b, (((1,), (1,)), ((), ())), precision=precision,
                           preferred_element_type=F32)


def _dot_tn(a, b):
    return lax.dot_general(a, b, (((0,), (0,)), ((), ())), preferred_element_type=F32)


def _rms(x, g):
    return x * lax.rsqrt(jnp.mean(x * x, axis=-1, keepdims=True) + EPS) * g


def _sigmoid(x):
    return 1.0 / (1.0 + jnp.exp(-x))


def _silu(x):
    return x * _sigmoid(x)


def _log_gammas(n_heads):
    return tuple(math.log(1.0 - 2.0 ** (-5.0 - h)) for h in range(n_heads))


def _rope_table_kernel(pos_ref, inv_ref, sgn_ref, cos_ref, sin_ref):
    ang = pos_ref[...] * inv_ref[...]
    cos_ref[...] = jnp.cos(ang)
    sin_ref[...] = jnp.sin(ang) * sgn_ref[...]


def _rope_tables(pos, inv_dup, sgn):
    n = pos.shape[0]
    dk = inv_dup.shape[1]
    return pl.pallas_call(
        _rope_table_kernel,
        out_shape=(jax.ShapeDtypeStruct((n, dk), F32), jax.ShapeDtypeStruct((n, dk), F32)),
        name="rope_tables",
    )(pos, inv_dup, sgn)


def _mixer_prompt_kernel(x_ref, cos_ref, sin_ref, g_ref, win_ref, cw_ref, wco_ref, wro_ref,
                         wo_ref, out_ref, cst_ref, rst_ref, tail_ref, state_ref, decay_ref,
                         *, L, H, DK, DV, DC, D, log_gamma):
    c = pl.program_id(1)
    DR = H * DK

    @pl.when(c == 0)
    def _init():
        tail_ref[...] = jnp.zeros_like(tail_ref)
        state_ref[...] = jnp.zeros_like(state_ref)
        diff = (lax.broadcasted_iota(jnp.int32, (L, L), 0)
                - lax.broadcasted_iota(jnp.int32, (L, L), 1)).astype(F32)
        for h in range(H):
            decay_ref[h] = jnp.where(diff >= 0.0,
                                     jnp.exp(jnp.maximum(diff, 0.0) * log_gamma[h]), 0.0)

    x = x_ref[...]
    xn = _rms(x, g_ref[...]).astype(BF16)

    zc = _dot(xn, win_ref[:, 0:3 * DC])
    cb, u = zc[:, 0:DC], zc[:, DC:2 * DC] * zc[:, 2 * DC:3 * DC]
    tail = tail_ref[...]
    p1, p2 = tail[7:8, :], tail[6:7, :]
    row = lax.broadcasted_iota(jnp.int32, (L, DC), 0)
    u1 = jnp.where(row == 0, p1, pltpu.roll(u, 1, axis=0))
    u2 = jnp.where(row == 0, p2, jnp.where(row == 1, p1, pltpu.roll(u, 2, axis=0)))
    cw = cw_ref[...]
    conv = cw[0:1, :] * u2 + cw[1:2, :] * u1 + cw[2:3, :] * u
    tail_ref[...] = u[L - 8:L, :]
    a = _dot((cb * conv).astype(BF16), wco_ref[...])

    zr = _dot(xn, win_ref[:, 3 * DC:3 * DC + 4 * DR])
    cos, sin = cos_ref[...], sin_ref[...]
    ridx_k = lax.broadcasted_iota(jnp.int32, (L, DK), 0).astype(F32)
    ridx_v = lax.broadcasted_iota(jnp.int32, (L, DV), 0).astype(F32)
    heads = []
    for h in range(H):
        lg = log_gamma[h]
        qh = zr[:, h * DK:(h + 1) * DK]
        kh = zr[:, DR + h * DK:DR + (h + 1) * DK]
        vh = zr[:, 2 * DR + h * DV:2 * DR + (h + 1) * DV]
        gh = zr[:, 3 * DR + h * DV:3 * DR + (h + 1) * DV]
        qh = qh * cos + pltpu.roll(qh, DK // 2, axis=1) * sin
        kh = (kh * cos + pltpu.roll(kh, DK // 2, axis=1) * sin) * (DK ** -0.5)
        qb, vb = qh.astype(BF16), vh.astype(BF16)
        s_prev = state_ref[h]
        scores = _dot_nt(qb, kh.astype(BF16)) * decay_ref[h]
        o = _dot(scores.astype(BF16), vb)
        o = o + _dot(qb, s_prev.astype(BF16)) * jnp.exp((ridx_v + 1.0) * lg)
        k_dec = jnp.exp((L - 1.0 - ridx_k) * lg)
        state_ref[h] = math.exp(L * lg) * s_prev + _dot_tn((kh * k_dec).astype(BF16), vb)
        mu = jnp.mean(o, axis=-1, keepdims=True)
        d = o - mu
        var = jnp.mean(d * d, axis=-1, keepdims=True)
        heads.append(_silu(gh) * (d * lax.rsqrt(var + EPS)))
    r = _dot(jnp.concatenate(heads, axis=-1).astype(BF16), wro_ref[...])

    zg = _dot(xn, win_ref[:, 3 * DC + 4 * DR:3 * DC + 4 * DR + 2 * D])
    m = _sigmoid(zg[:, 0:D]) * a + _sigmoid(zg[:, D:2 * D]) * r
    out_ref[...] = x + _dot(m.astype(BF16), wo_ref[...])

    @pl.when(c == pl.num_programs(1) - 1)
    def _final():
        cst_ref[...] = u[L - 8:L, :]
        rst_ref[...] = state_ref[...]


def _mixer_prompt(x, cos, sin, g, w_in, cw, w_co, w_ro, w_o, *, H, DK, DV):
    B, T, D = x.shape
    DC = cw.shape[1]
    L = MIX_CHUNK
    const = lambda *shape: pl.BlockSpec(shape, lambda b, c: (0,) * len(shape))
    kern = functools.partial(_mixer_prompt_kernel, L=L, H=H, DK=DK, DV=DV, DC=DC, D=D,
                             log_gamma=_log_gammas(H))
    return pl.pallas_call(
        kern,
        grid=(B, T // L),
        in_specs=[
            pl.BlockSpec((None, L, D), lambda b, c: (b, c, 0)),
            pl.BlockSpec((L, DK), lambda b, c: (c, 0)),
            pl.BlockSpec((L, DK), lambda b, c: (c, 0)),
            const(1, D), const(*w_in.shape), const(*cw.shape), const(*w_co.shape),
            const(*w_ro.shape), const(*w_o.shape),
        ],
        out_specs=[
            pl.BlockSpec((None, L, D), lambda b, c: (b, c, 0)),
            pl.BlockSpec((None, 8, DC), lambda b, c: (b, 0, 0)),
            pl.BlockSpec((None, H, DK, DV), lambda b, c: (b, 0, 0, 0)),
        ],
        out_shape=(jax.ShapeDtypeStruct((B, T, D), F32),
                   jax.ShapeDtypeStruct((B, 8, DC), F32),
                   jax.ShapeDtypeStruct((B, H, DK, DV), F32)),
        scratch_shapes=[pltpu.VMEM((8, DC), F32), pltpu.VMEM((H, DK, DV), F32),
                        pltpu.VMEM((H, L, L), F32)],
        compiler_params=_cparams(("arbitrary", "arbitrary")),
        name="mixer_prompt",
    )(x, cos, sin, g, w_in, cw, w_co, w_ro, w_o)


def _dec_inproj_kernel(x_ref, g_ref, win_ref, wqkt_ref, invt_ref, zm_ref, zg_ref, qkt_ref,
                       *, H, DK, n_main, pos0):
    xn = _rms(x_ref[...], g_ref[...]).astype(BF16)
    zm_ref[...] = _dot(xn, win_ref[:, 0:n_main])
    zg_ref[...] = _dot(xn, win_ref[:, n_main:win_ref.shape[1]])
    qkt = _dot_nt(wqkt_ref[...], xn)
    ang = pos0 * invt_ref[...]
    cos, sin = jnp.cos(ang), jnp.sin(ang)
    half = DK // 2
    for hh in range(2 * H):
        blk = qkt[hh * DK:(hh + 1) * DK, :]
        x1, x2 = blk[0:half, :], blk[half:DK, :]
        scale = 1.0 if hh < H else DK ** -0.5
        qkt_ref[hh * DK:hh * DK + half, :] = (x1 * cos[0:half] - x2 * sin[0:half]) * scale
        qkt_ref[hh * DK + half:(hh + 1) * DK, :] = (x2 * cos[half:DK] + x1 * sin[half:DK]) * scale


def _dec_inproj(x, g, w_in, w_qkt, inv_t, n_main, *, H, DK):
    n, D = x.shape
    kern = functools.partial(_dec_inproj_kernel, H=H, DK=DK, n_main=n_main,
                             pos0=float(PAST_LEN))
    return pl.pallas_call(
        kern,
        out_shape=(jax.ShapeDtypeStruct((n, n_main), F32),
                   jax.ShapeDtypeStruct((n, w_in.shape[1] - n_main), F32),
                   jax.ShapeDtypeStruct((w_qkt.shape[0], n), F32)),
        compiler_params=pltpu.CompilerParams(vmem_limit_bytes=VMEM_LIMIT),
        name="dec_inproj",
    )(x, g, w_in, w_qkt, inv_t)


def _dec_state_kernel(zm_ref, qkt_ref, conv_ref, cw_ref, s_ref, *rest, BB, H, DK, DV, DC,
                      log_gamma):
    ain_ref, rin_ref, nconv_ref, ns_ref, o_scr = rest[-5:]
    DR = H * DK
    zm = zm_ref[...]
    cb, u = zm[:, 0:DC], zm[:, DC:2 * DC] * zm[:, 2 * DC:3 * DC]
    buf = conv_ref[...]
    b0, b1 = buf[:, 0:DC], buf[:, DC:2 * DC]
    cw = cw_ref[...]
    conv = cw[0:1, :] * b0 + cw[1:2, :] * b1 + cw[2:3, :] * u
    nconv_ref[:, 0:DC] = b1
    nconv_ref[:, DC:2 * DC] = u
    ain_ref[...] = cb * conv

    qkt = qkt_ref[...]
    v = zm[:, 3 * DC + 2 * DR:3 * DC + 3 * DR]
    g = zm[:, 3 * DC + 3 * DR:3 * DC + 4 * DR]
    for j in range(BB):
        for h in range(H):
            gamma = math.exp(log_gamma[h])
            qc = qkt[h * DK:(h + 1) * DK, j:j + 1]
            kc = qkt[DR + h * DK:DR + (h + 1) * DK, j:j + 1]
            s_prev = s_ref[j, h]
            vrow = v[j:j + 1, h * DV:(h + 1) * DV]
            qk = jnp.sum(qc * kc, axis=0, keepdims=True)
            inter = jnp.sum(qc * s_prev, axis=0, keepdims=True) * gamma
            o_scr[j:j + 1, h * DV:(h + 1) * DV] = qk * vrow + inter
            ns_ref[j, h] = gamma * s_prev + kc * vrow
    o = o_scr[...]
    for h in range(H):
        oh = o[:, h * DV:(h + 1) * DV]
        mu = jnp.mean(oh, axis=-1, keepdims=True)
        d = oh - mu
        var = jnp.mean(d * d, axis=-1, keepdims=True)
        rin_ref[:, h * DV:(h + 1) * DV] = _silu(g[:, h * DV:(h + 1) * DV]) * (d * lax.rsqrt(var + EPS))


def _dec_state(zm, qkt3, conv2d, cw, state_all, new_states, layer, *, H, DK, DV):
    n = zm.shape[0]
    depth = state_all.shape[0]
    DC = cw.shape[1]
    BB = DEC_BB
    kern = functools.partial(_dec_state_kernel, BB=BB, H=H, DK=DK, DV=DV, DC=DC,
                             log_gamma=_log_gammas(H))
    args = [zm, qkt3, conv2d, cw, state_all, new_states]
    in_specs = [
        pl.BlockSpec((BB, zm.shape[1]), lambda b: (b, 0)),
        pl.BlockSpec((None, qkt3.shape[1], BB), lambda b: (b, 0, 0)),
        pl.BlockSpec((BB, 2 * DC), lambda b: (b, 0)),
        pl.BlockSpec(cw.shape, lambda b: (0, 0)),
        pl.BlockSpec((None, BB, H, DK, DV), lambda b: (layer, b, 0, 0, 0)),
        pl.BlockSpec(memory_space=pl.ANY),
    ]
    aliases = {5: 3}
    return pl.pallas_call(
        kern,
        grid=(n // BB,),
        in_specs=in_specs,
        out_specs=[
            pl.BlockSpec((BB, DC), lambda b: (b, 0)),
            pl.BlockSpec((BB, H * DV), lambda b: (b, 0)),
            pl.BlockSpec((BB, 2 * DC), lambda b: (b, 0)),
            pl.BlockSpec((None, BB, H, DK, DV), lambda b: (layer, b, 0, 0, 0)),
        ],
        out_shape=(jax.ShapeDtypeStruct((n, DC), F32),
                   jax.ShapeDtypeStruct((n, H * DV), F32),
                   jax.ShapeDtypeStruct((n, 2 * DC), F32),
                   jax.ShapeDtypeStruct((depth, n, H, DK, DV), F32)),
        scratch_shapes=[pltpu.VMEM((BB, H * DV), F32)],
        input_output_aliases=aliases,
        compiler_params=_cparams(("arbitrary",)),
        name="dec_state",
    )(*args)


def _dec_outproj_kernel(x_ref, zg_ref, ain_ref, rin_ref, wco_ref, wro_ref, wo_ref, out_ref, *, D):
    a = _dot(ain_ref[...].astype(BF16), wco_ref[...])
    r = _dot(rin_ref[...].astype(BF16), wro_ref[...])
    zg = zg_ref[...]
    m = _sigmoid(zg[:, 0:D]) * a + _sigmoid(zg[:, D:2 * D]) * r
    out_ref[...] = x_ref[...] + _dot(m.astype(BF16), wo_ref[...])


def _dec_outproj(x, zg, ain, rin, w_co, w_ro, w_o):
    n, D = x.shape
    return pl.pallas_call(
        functools.partial(_dec_outproj_kernel, D=D),
        out_shape=jax.ShapeDtypeStruct((n, D), F32),
        compiler_params=pltpu.CompilerParams(vmem_limit_bytes=VMEM_LIMIT),
        name="dec_outproj",
    )(x, zg, ain, rin, w_co, w_ro, w_o)


def _ffn_dense_kernel(x_ref, g_ref, wg_ref, wu_ref, wd_ref, out_ref, hn_ref):
    @pl.when(pl.program_id(1) == 0)
    def _init():
        x = x_ref[...]
        hn_ref[...] = _rms(x, g_ref[...]).astype(BF16)
        out_ref[...] = x

    hn = hn_ref[...]
    act = (_silu(_dot(hn, wg_ref[...])) * _dot(hn, wu_ref[...])).astype(BF16)
    out_ref[...] += _dot(act, wd_ref[...])


def _ffn_dense(x, g, w_gu, w_down):
    n, D = x.shape
    FF = w_down.shape[0]
    tm = min(FFN_TM, n)
    fc = FFN_FC
    nf = FF // fc
    return pl.pallas_call(
        _ffn_dense_kernel,
        grid=(n // tm, nf),
        in_specs=[
            pl.BlockSpec((tm, D), lambda i, j: (i, 0)),
            pl.BlockSpec((1, D), lambda i, j: (0, 0)),
            pl.BlockSpec((D, fc), lambda i, j: (0, j)),
            pl.BlockSpec((D, fc), lambda i, j: (0, nf + j)),
            pl.BlockSpec((fc, D), lambda i, j: (j, 0)),
        ],
        out_specs=pl.BlockSpec((tm, D), lambda i, j: (i, 0)),
        out_shape=jax.ShapeDtypeStruct((n, D), F32),
        scratch_shapes=[pltpu.VMEM((tm, D), BF16)],
        compiler_params=_cparams(("arbitrary", "arbitrary")),
        name="ffn_dense",
    )(x, g, w_gu, w_gu, w_down)


def _router_kernel(x_ref, g_ref, wrt_ref, hn_ref, route_ref, cnt_ref, *, T, E, CH):
    hn = _rms(x_ref[...], g_ref[...])
    hn_ref[...] = hn
    logits = _dot_nt(wrt_ref[...], hn, precision=lax.Precision.HIGHEST)
    eidx = lax.broadcasted_iota(jnp.int32, (E, T), 0)
    m1 = jnp.max(logits, axis=0, keepdims=True)
    i1 = jnp.min(jnp.where(logits == m1, eidx, E), axis=0, keepdims=True)
    sel1 = eidx == i1
    rest = jnp.where(sel1, -jnp.inf, logits)
    m2 = jnp.max(rest, axis=0, keepdims=True)
    i2 = jnp.min(jnp.where(rest == m2, eidx, E), axis=0, keepdims=True)
    sel2 = eidx == i2
    e2 = jnp.exp(m2 - m1)
    w1 = 1.0 / (1.0 + e2)
    f1 = jnp.where(sel1, 1.0, 0.0)
    f2 = jnp.where(sel2, 1.0, 0.0)
    sel = f1 + f2
    selb = sel.astype(BF16)
    route_ref[0:1, :] = i1.astype(F32)
    route_ref[1:2, :] = i2.astype(F32)
    route_ref[4:5, :] = w1
    route_ref[5:6, :] = e2 * w1
    route_ref[6:8, :] = jnp.zeros((2, T), F32)
    for cidx in range(T // CH):
        cols = slice(cidx * CH, (cidx + 1) * CH)
        src = lax.broadcasted_iota(jnp.int32, (T, CH), 0)
        dst = lax.broadcasted_iota(jnp.int32, (T, CH), 1) + cidx * CH
        tri = jnp.where(src < dst, 1.0, 0.0).astype(BF16)
        pos = _dot(selb, tri)
        route_ref[2:3, cols] = jnp.sum(pos * f1[:, cols], axis=0, keepdims=True)
        route_ref[3:4, cols] = jnp.sum(pos * f2[:, cols], axis=0, keepdims=True)
    cnt = jnp.sum(sel, axis=1, keepdims=True)
    cnt_ref[...] = jnp.broadcast_to(cnt, cnt_ref.shape).astype(jnp.int32)


def _router(x, g, w_router_t, T):
    n, D = x.shape
    E = w_router_t.shape[0]
    nt = n // T
    kern = functools.partial(_router_kernel, T=T, E=E, CH=min(256, T))
    return pl.pallas_call(
        kern,
        grid=(nt,),
        in_specs=[
            pl.BlockSpec((T, D), lambda i: (i, 0)),
            pl.BlockSpec((1, D), lambda i: (0, 0)),
            pl.BlockSpec((E, D), lambda i: (0, 0)),
        ],
        out_specs=[
            pl.BlockSpec((T, D), lambda i: (i, 0)),
            pl.BlockSpec((8, T), lambda i: (0, i)),
            pl.BlockSpec((None, E, 128), lambda i: (i, 0, 0)),
        ],
        out_shape=(jax.ShapeDtypeStruct((n, D), F32),
                   jax.ShapeDtypeStruct((8, n), F32),
                   jax.ShapeDtypeStruct((nt, E, 128), jnp.int32)),
        compiler_params=_cparams(("arbitrary",)),
        name="moe_router",
    )(x, g, w_router_t)


def _route_plan(counts, tm, n_tiles_max):
    E = counts.shape[1]
    tot = jnp.sum(counts, axis=0)
    tiles_e = (tot + (tm - 1)) // tm
    tile_end = jnp.cumsum(tiles_e)
    tile_start = tile_end - tiles_e
    base = (tile_start * tm)[None, :] + jnp.cumsum(counts, axis=0) - counts
    r = jnp.arange(n_tiles_max, dtype=jnp.int32)
    tile_e = jnp.minimum(jnp.sum(r[:, None] >= tile_end[None, :], axis=1), E - 1).astype(jnp.int32)
    n_used = tile_end[E - 1:E].astype(jnp.int32)
    valid = jnp.clip(tot[tile_e] - (r - tile_start[tile_e]) * tm, 0, tm)
    valid = jnp.where(r < n_used[0], valid, 0).astype(jnp.int32)
    return base.astype(jnp.int32), tile_e, valid, n_used


def _global_slots(route, base, T):
    nt, E = base.shape
    idx = route[0:2].astype(jnp.int32).reshape(2, nt, T)
    slot = route[2:4].astype(jnp.int32).reshape(2, nt, T)
    onehot = idx[..., None] == jnp.arange(E, dtype=jnp.int32)
    start = jnp.sum(jnp.where(onehot, base[None, :, None, :], 0), axis=-1)
    return (start + slot).reshape(2, nt * T)


def _row_copy(src_ref, src_row, dst_ref, dst_row, sem):
    return pltpu.make_async_copy(src_ref.at[pl.ds(src_row, 1)], dst_ref.at[pl.ds(dst_row, 1)], sem)


def _dispatch_kernel(g_ref, hn_ref, xs_in_ref, xs_ref, sem, *, TD):
    del xs_in_ref

    def body(t, carry):
        for k in range(TOP_K):
            _row_copy(hn_ref, t, xs_ref, g_ref[k, t], sem).start()
        return carry
    lax.fori_loop(0, TD, body, 0, unroll=8)
    for k in range(TOP_K):
        pltpu.make_async_copy(hn_ref, xs_ref.at[pl.ds(0, TD)], sem).wait()


def _dispatch(g, hn, xs):
    n, D = hn.shape
    TD = min(MOE_TD, n)
    return pl.pallas_call(
        functools.partial(_dispatch_kernel, TD=TD),
        grid=(n // TD,),
        in_specs=[
            pl.BlockSpec((TOP_K, TD), lambda i: (0, i), memory_space=pltpu.SMEM),
            pl.BlockSpec((TD, D), lambda i: (i, 0)),
            pl.BlockSpec(memory_space=pl.ANY),
        ],
        out_specs=pl.BlockSpec(memory_space=pl.ANY),
        out_shape=jax.ShapeDtypeStruct(xs.shape, xs.dtype),
        scratch_shapes=[pltpu.SemaphoreType.DMA(())],
        input_output_aliases={2: 0},
        compiler_params=_cparams(("arbitrary",)),
        name="moe_dispatch",
    )(g, hn, xs)


def _experts_kernel(te_ref, valid_ref, nused_ref, x_ref, wg_ref, wu_ref, wd_ref, y_ref,
                    wgb_ref, wub_ref, wdb_ref, *, TM, R):
    r, j = pl.program_id(0), pl.program_id(1)

    @pl.when((r >= nused_ref[0]) & (j == 0))
    def _idle():
        y_ref[...] = jnp.zeros_like(y_ref)

    @pl.when(r < nused_ref[0])
    def _tile():
        wgb_ref[...] = wg_ref[...].astype(BF16)
        wub_ref[...] = wu_ref[...].astype(BF16)
        wdb_ref[...] = wd_ref[...].astype(BF16)
        nch = lax.shift_right_logical(valid_ref[r] + (R - 1), R.bit_length() - 1)

        def rows_of(c):
            return pl.ds(pl.multiple_of(c * R, R), R)

        def chunk(c):
            xs = x_ref[rows_of(c), :].astype(BF16)
            act = (_silu(_dot(xs, wgb_ref[...])) * _dot(xs, wub_ref[...])).astype(BF16)
            return _dot(act, wdb_ref[...])

        @pl.when(j == 0)
        def _first():
            def body(c, carry):
                y_ref[rows_of(c), :] = chunk(c)
                return carry
            lax.fori_loop(0, nch, body, 0)
            for c in range(TM // R):
                @pl.when(c >= nch)
                def _pad():
                    y_ref[c * R:(c + 1) * R, :] = jnp.zeros((R, y_ref.shape[1]), F32)

        @pl.when(j > 0)
        def _rest():
            def body(c, carry):
                y_ref[rows_of(c), :] += chunk(c)
                return carry
            lax.fori_loop(0, nch, body, 0)


def _experts(tile_e, valid, n_used, xs, w_gu_all, w_down_all, layer):
    rows, D = xs.shape
    FF = w_down_all.shape[2]
    TM, R, fc = MOE_TM, MOE_ROWS, MOE_FC
    nf = FF // fc
    nt = rows // TM

    def tile_of(r, nu):
        return jnp.minimum(r, nu[0] - 1)

    def chunk_of(r, j, nu):
        return jnp.where(r < nu[0], j, nf - 1)

    grid_spec = pltpu.PrefetchScalarGridSpec(
        num_scalar_prefetch=3,
        grid=(nt, nf),
        in_specs=[
            pl.BlockSpec((TM, D), lambda r, j, te, vl, nu: (tile_of(r, nu), 0)),
            pl.BlockSpec((None, None, D, fc),
                         lambda r, j, te, vl, nu: (layer, te[tile_of(r, nu)], 0, chunk_of(r, j, nu))),
            pl.BlockSpec((None, None, D, fc),
                         lambda r, j, te, vl, nu: (layer, te[tile_of(r, nu)], 0,
                                                   nf + chunk_of(r, j, nu))),
            pl.BlockSpec((None, None, fc, D),
                         lambda r, j, te, vl, nu: (layer, te[tile_of(r, nu)], chunk_of(r, j, nu), 0)),
        ],
        out_specs=pl.BlockSpec((TM, D), lambda r, j, te, vl, nu: (r, 0)),
        scratch_shapes=[pltpu.VMEM((D, fc), BF16), pltpu.VMEM((D, fc), BF16),
                        pltpu.VMEM((fc, D), BF16)],
    )
    return pl.pallas_call(
        functools.partial(_experts_kernel, TM=TM, R=R),
        grid_spec=grid_spec,
        out_shape=jax.ShapeDtypeStruct((rows, D), F32),
        compiler_params=_cparams(("arbitrary", "arbitrary")),
        name="moe_experts",
    )(tile_e, valid, n_used, xs, w_gu_all, w_gu_all, w_down_all)


def _ple_kernel(*refs, moe, final, tm):
    refs = list(refs)
    if moe:
        g_ref, x_ref, w_ref, ys_ref = refs[0:4]
        refs = refs[4:]
        buf_ref, sem = refs[-2:]
        refs = refs[:-2]
    else:
        x_ref = refs.pop(0)
    p_ref, gn_ref, wgate_ref, wple_ref = refs[0:4]
    gfin_ref = refs[4] if final else None
    out_ref = refs[-1]

    x = x_ref[...]
    if moe:
        def body(t, carry):
            for k in range(TOP_K):
                _row_copy(ys_ref, g_ref[k, t], buf_ref.at[k], t, sem).start()
            return carry
        lax.fori_loop(0, tm, body, 0, unroll=8)
        for k in range(TOP_K):
            pltpu.make_async_copy(ys_ref.at[pl.ds(0, tm)], buf_ref.at[k], sem).wait()
        w = w_ref[...]
        for k in range(TOP_K):
            x = x + w[:, k:k + 1] * buf_ref[k]
    gate = _sigmoid(_dot(_rms(x, gn_ref[...]).astype(BF16), wgate_ref[...]))
    y = x + gate * _dot(p_ref[...].astype(BF16), wple_ref[...])
    if final:
        y = _rms(y, gfin_ref[...])
    out_ref[...] = y


def _ple(x, moe, p_all, layer, g, w_gate, w_ple, g_final):
    n, D = x.shape
    DP = p_all.shape[-1]
    tm = min(PLE_TM, n)
    tok = pl.BlockSpec((tm, D), lambda i: (i, 0))
    const = lambda *shape: pl.BlockSpec(shape, lambda i: (0,) * len(shape))
    args, specs, scratch = [], [], []
    if moe is not None:
        slots, weights, ys = moe
        args += [slots, x, weights, ys]
        specs += [pl.BlockSpec((TOP_K, tm), lambda i: (0, i), memory_space=pltpu.SMEM), tok,
                  pl.BlockSpec((tm, TOP_K), lambda i: (i, 0)), pl.BlockSpec(memory_space=pl.ANY)]
        scratch = [pltpu.VMEM((TOP_K, tm, D), F32), pltpu.SemaphoreType.DMA(())]
    else:
        args.append(x)
        specs.append(tok)
    args += [p_all, g, w_gate, w_ple]
    specs += [pl.BlockSpec((None, tm, DP), lambda i: (layer, i, 0)), const(1, D),
              const(D, D), const(DP, D)]
    if g_final is not None:
        args.append(g_final)
        specs.append(const(1, D))
    kern = functools.partial(_ple_kernel, moe=moe is not None, final=g_final is not None, tm=tm)
    return pl.pallas_call(
        kern,
        grid=(n // tm,),
        in_specs=specs,
        out_specs=tok,
        out_shape=jax.ShapeDtypeStruct((n, D), F32),
        scratch_shapes=scratch,
        compiler_params=_cparams(("arbitrary",)),
        name="ple_moe" if moe is not None else "ple",
    )(*args)


def kernel(x_prompt, x_sample, state_conv, state_ret, p_prompt, p_sample, norm_mix_g, w_in,
           conv_w, w_conv_out, w_ret_out, w_o, norm_ffn_g, w_dense_gu, w_dense_down, w_router,
           w_exp_gu, w_exp_down, norm_ple_g, w_ple, w_ple_gate, norm_final_g):
    B, T, D = x_prompt.shape
    NP = B * T
    NS = x_sample.shape[0]
    depth = w_in.shape[0]
    _, _, H, DK, DV = state_ret.shape
    DC = conv_w.shape[-1]
    DR = H * DK
    E = w_router.shape[-1]
    assert x_sample.shape[1] == 1 and CONV_W - 1 == state_conv.shape[2]
    assert T % MIX_CHUNK == 0 and NS % DEC_BB == 0 and NP % ROUTE_TILE == 0

    half = DK // 2
    inv = ROPE_BASE ** (-jnp.arange(half, dtype=F32) / half)
    inv_dup = jnp.concatenate([inv, inv]).reshape(1, DK)
    sgn = jnp.concatenate([-jnp.ones((half,), F32), jnp.ones((half,), F32)]).reshape(1, DK)
    pos = jnp.arange(T, dtype=F32).reshape(T, 1)
    cos_p, sin_p = _rope_tables(pos, inv_dup, sgn)
    inv_t = jnp.broadcast_to(inv_dup.reshape(DK, 1), (DK, NS))

    n_tiles_max = (TOP_K * (NP + NS)) // MOE_TM + E

    row = lambda a: a.reshape(1, -1)
    xp = x_prompt
    xs = x_sample.reshape(NS, D)
    pp = p_prompt.reshape(depth, NP, -1)
    ps = p_sample.reshape(depth, NS, -1)
    conv_p, ret_p, conv_s = [], [], []
    ret_s = jnp.zeros(state_ret.shape, F32)
    for i in range(depth):
        w_in_b = w_in[i].astype(BF16)
        w_co_b = w_conv_out[i].astype(BF16)
        w_ro_b = w_ret_out[i].astype(BF16)
        w_o_b = w_o[i].astype(BF16)
        g_mix = row(norm_mix_g[i])

        xp, cst, rst = _mixer_prompt(xp, cos_p, sin_p, g_mix, w_in_b, conv_w[i], w_co_b, w_ro_b,
                                     w_o_b, H=H, DK=DK, DV=DV)
        conv_p.append(cst[:, 8 - (CONV_W - 1):, :])
        ret_p.append(rst)

        n_main = 3 * DC + 4 * DR
        w_qkt = w_in_b[:, 3 * DC:3 * DC + 2 * DR].T
        zm, zg, qkt = _dec_inproj(xs, g_mix, w_in_b, w_qkt, inv_t, n_main, H=H, DK=DK)
        qkt3 = qkt.reshape(2 * DR, NS // DEC_BB, DEC_BB).transpose(1, 0, 2)
        ain, rin, ncv, ret_s = _dec_state(zm, qkt3, state_conv[i].reshape(NS, -1), conv_w[i],
                                          state_ret, ret_s, i, H=H, DK=DK, DV=DV)
        xs = _dec_outproj(xs, zg, ain, rin, w_co_b, w_ro_b, w_o_b)
        conv_s.append(ncv.reshape(NS, CONV_W - 1, DC))

        g_ffn = row(norm_ffn_g[i])
        xp2 = xp.reshape(NP, D)
        if i % 2 == 0:
            w_gu_b = w_dense_gu[i // 2].astype(BF16)
            w_dn_b = w_dense_down[i // 2].astype(BF16)
            xp2 = _ffn_dense(xp2, g_ffn, w_gu_b, w_dn_b)
            xs = _ffn_dense(xs, g_ffn, w_gu_b, w_dn_b)
            moe_p = moe_s = None
        else:
            w_rt = w_router[i // 2].T
            hn_p, route_p, cnt_p = _router(xp2, g_ffn, w_rt, ROUTE_TILE)
            hn_s, route_s, cnt_s = _router(xs, g_ffn, w_rt, NS)
            counts = jnp.concatenate([cnt_p[:, :, 0], cnt_s[:, :, 0]], axis=0)
            base, tile_e, valid, n_used = _route_plan(counts, MOE_TM, n_tiles_max)
            ntp = NP // ROUTE_TILE
            slots_p = _global_slots(route_p, base[:ntp], ROUTE_TILE)
            slots_s = _global_slots(route_s, base[ntp:], NS)
            sorted_x = jnp.zeros((n_tiles_max * MOE_TM, D), F32)
            sorted_x = _dispatch(slots_p, hn_p, sorted_x)
            sorted_x = _dispatch(slots_s, hn_s, sorted_x)
            sorted_y = _experts(tile_e, valid, n_used, sorted_x, w_exp_gu, w_exp_down, i // 2)
            moe_p = (slots_p, route_p[4:6].T, sorted_y)
            moe_s = (slots_s, route_s[4:6].T, sorted_y)

        g_fin = row(norm_final_g) if i == depth - 1 else None
        w_pg_b = w_ple_gate[i].astype(BF16)
        w_pl_b = w_ple[i].astype(BF16)
        xp = _ple(xp2, moe_p, pp, i, row(norm_ple_g[i]), w_pg_b, w_pl_b, g_fin).reshape(B, T, D)
        xs = _ple(xs, moe_s, ps, i, row(norm_ple_g[i]), w_pg_b, w_pl_b, g_fin)

    return (xp, xs.reshape(NS, 1, D), jnp.stack(conv_p), jnp.stack(ret_p),
            jnp.stack(conv_s), ret_s)
```

```python
import functools
import math

import jax
import jax.numpy as jnp
from jax import lax
from jax.experimental import pallas as pl
from jax.experimental.pallas import tpu as pltpu

F32 = jnp.float32
BF16 = jnp.bfloat16

CONV_W = 3
TOP_K = 2
ROPE_BASE = 10000.0
EPS = 1e-6
PAST_LEN = 16384

V7X_VMEM_BYTES = 64 * 1024 * 1024
VMEM_LIMIT = V7X_VMEM_BYTES - 8 * 1024 * 1024

MIX_CHUNK = 256
FFN_TM = 1024
FFN_FC = 512
ROUTE_TILE = 2048
MOE_TM = 1024
MOE_ROWS = 512
MOE_FC = 512
MOE_TD = 512
PLE_TM = 512
DEC_BB = 16


def _cparams(sem):
    return pltpu.CompilerParams(dimension_semantics=sem, vmem_limit_bytes=VMEM_LIMIT)


def _dot(a, b):
    return jnp.dot(a, b, preferred_element_type=F32)


def _dot_nt(a, b, precision=None):
    return lax.dot_general(a, b, (((1,), (1,)), ((), ())), precision=precision,
                           preferred_element_type=F32)


def _dot_tn(a, b):
    return lax.dot_general(a, b, (((0,), (0,)), ((), ())), preferred_element_type=F32)


def _rms(x, g):
    return x * lax.rsqrt(jnp.mean(x * x, axis=-1, keepdims=True) + EPS) * g


def _sigmoid(x):
    return 1.0 / (1.0 + jnp.exp(-x))


def _silu(x):
    return x * _sigmoid(x)


def _log_gammas(n_heads):
    return tuple(math.log(1.0 - 2.0 ** (-5.0 - h)) for h in range(n_heads))


def _rope_table_kernel(pos_ref, inv_ref, sgn_ref, cos_ref, sin_ref):
    ang = pos_ref[...] * inv_ref[...]
    cos_ref[...] = jnp.cos(ang)
    sin_ref[...] = jnp.sin(ang) * sgn_ref[...]


def _rope_tables(pos, inv_dup, sgn):
    n = pos.shape[0]
    dk = inv_dup.shape[1]
    return pl.pallas_call(
        _rope_table_kernel,
        out_shape=(jax.ShapeDtypeStruct((n, dk), F32), jax.ShapeDtypeStruct((n, dk), F32)),
        name="rope_tables",
    )(pos, inv_dup, sgn)


def _mixer_prompt_kernel(x_ref, cos_ref, sin_ref, g_ref, win_ref, cw_ref, wco_ref, wro_ref,
                         wo_ref, out_ref, cst_ref, rst_ref, tail_ref, state_ref, decay_ref,
                         *, L, H, DK, DV, DC, D, log_gamma):
    c = pl.program_id(1)
    DR = H * DK

    @pl.when(c == 0)
    def _init():
        tail_ref[...] = jnp.zeros_like(tail_ref)
        state_ref[...] = jnp.zeros_like(state_ref)
        diff = (lax.broadcasted_iota(jnp.int32, (L, L), 0)
                - lax.broadcasted_iota(jnp.int32, (L, L), 1)).astype(F32)
        for h in range(H):
            decay_ref[h] = jnp.where(diff >= 0.0,
                                     jnp.exp(jnp.maximum(diff, 0.0) * log_gamma[h]), 0.0)

    x = x_ref[...]
    xn = _rms(x, g_ref[...]).astype(BF16)

    zc = _dot(xn, win_ref[:, 0:3 * DC])
    cb, u = zc[:, 0:DC], zc[:, DC:2 * DC] * zc[:, 2 * DC:3 * DC]
    tail = tail_ref[...]
    p1, p2 = tail[7:8, :], tail[6:7, :]
    row = lax.broadcasted_iota(jnp.int32, (L, DC), 0)
    u1 = jnp.where(row == 0, p1, pltpu.roll(u, 1, axis=0))
    u2 = jnp.where(row == 0, p2, jnp.where(row == 1, p1, pltpu.roll(u, 2, axis=0)))
    cw = cw_ref[...]
    conv = cw[0:1, :] * u2 + cw[1:2, :] * u1 + cw[2:3, :] * u
    tail_ref[...] = u[L - 8:L, :]
    a = _dot((cb * conv).astype(BF16), wco_ref[...])

    zr = _dot(xn, win_ref[:, 3 * DC:3 * DC + 4 * DR])
    cos, sin = cos_ref[...], sin_ref[...]
    ridx_k = lax.broadcasted_iota(jnp.int32, (L, DK), 0).astype(F32)
    ridx_v = lax.broadcasted_iota(jnp.int32, (L, DV), 0).astype(F32)
    heads = []
    for h in range(H):
        lg = log_gamma[h]
        qh = zr[:, h * DK:(h + 1) * DK]
        kh = zr[:, DR + h * DK:DR + (h + 1) * DK]
        vh = zr[:, 2 * DR + h * DV:2 * DR + (h + 1) * DV]
        gh = zr[:, 3 * DR + h * DV:3 * DR + (h + 1) * DV]
        qh = qh * cos + pltpu.roll(qh, DK // 2, axis=1) * sin
        kh = (kh * cos + pltpu.roll(kh, DK // 2, axis=1) * sin) * (DK ** -0.5)
        qb, vb = qh.astype(BF16), vh.astype(BF16)
        s_prev = state_ref[h]
        scores = _dot_nt(qb, kh.astype(BF16)) * decay_ref[h]
        o = _dot(scores.astype(BF16), vb)
        o = o + _dot(qb, s_prev.astype(BF16)) * jnp.exp((ridx_v + 1.0) * lg)
        k_dec = jnp.exp((L - 1.0 - ridx_k) * lg)
        state_ref[h] = math.exp(L * lg) * s_prev + _dot_tn((kh * k_dec).astype(BF16), vb)
        mu = jnp.mean(o, axis=-1, keepdims=True)
        d = o - mu
        var = jnp.mean(d * d, axis=-1, keepdims=True)
        heads.append(_silu(gh) * (d * lax.rsqrt(var + EPS)))
    r = _dot(jnp.concatenate(heads, axis=-1).astype(BF16), wro_ref[...])

    zg = _dot(xn, win_ref[:, 3 * DC + 4 * DR:3 * DC + 4 * DR + 2 * D])
    m = _sigmoid(zg[:, 0:D]) * a + _sigmoid(zg[:, D:2 * D]) * r
    out_ref[...] = x + _dot(m.astype(BF16), wo_ref[...])

    @pl.when(c == pl.num_programs(1) - 1)
    def _final():
        cst_ref[...] = u[L - 8:L, :]
        rst_ref[...] = state_ref[...]


def _mixer_prompt(x, cos, sin, g, w_in, cw, w_co, w_ro, w_o, *, H, DK, DV):
    B, T, D = x.shape
    DC = cw.shape[1]
    L = MIX_CHUNK
    const = lambda *shape: pl.BlockSpec(shape, lambda b, c: (0,) * len(shape))
    kern = functools.partial(_mixer_prompt_kernel, L=L, H=H, DK=DK, DV=DV, DC=DC, D=D,
                             log_gamma=_log_gammas(H))
    return pl.pallas_call(
        kern,
        grid=(B, T // L),
        in_specs=[
            pl.BlockSpec((None, L, D), lambda b, c: (b, c, 0)),
            pl.BlockSpec((L, DK), lambda b, c: (c, 0)),
            pl.BlockSpec((L, DK), lambda b, c: (c, 0)),
            const(1, D), const(*w_in.shape), const(*cw.shape), const(*w_co.shape),
            const(*w_ro.shape), const(*w_o.shape),
        ],
        out_specs=[
            pl.BlockSpec((None, L, D), lambda b, c: (b, c, 0)),
            pl.BlockSpec((None, 8, DC), lambda b, c: (b, 0, 0)),
            pl.BlockSpec((None, H, DK, DV), lambda b, c: (b, 0, 0, 0)),
        ],
        out_shape=(jax.ShapeDtypeStruct((B, T, D), F32),
                   jax.ShapeDtypeStruct((B, 8, DC), F32),
                   jax.ShapeDtypeStruct((B, H, DK, DV), F32)),
        scratch_shapes=[pltpu.VMEM((8, DC), F32), pltpu.VMEM((H, DK, DV), F32),
                        pltpu.VMEM((H, L, L), F32)],
        compiler_params=_cparams(("arbitrary", "arbitrary")),
        name="mixer_prompt",
    )(x, cos, sin, g, w_in, cw, w_co, w_ro, w_o)


def _dec_inproj_kernel(x_ref, g_ref, win_ref, wqkt_ref, invt_ref, zm_ref, zg_ref, qkt_ref,
                       *, H, DK, n_main, pos0):
    xn = _rms(x_ref[...], g_ref[...]).astype(BF16)
    zm_ref[...] = _dot(xn, win_ref[:, 0:n_main])
    zg_ref[...] = _dot(xn, win_ref[:, n_main:win_ref.shape[1]])
    qkt = _dot_nt(wqkt_ref[...], xn)
    ang = pos0 * invt_ref[...]
    cos, sin = jnp.cos(ang), jnp.sin(ang)
    half = DK // 2
    for hh in range(2 * H):
        blk = qkt[hh * DK:(hh + 1) * DK, :]
        x1, x2 = blk[0:half, :], blk[half:DK, :]
        scale = 1.0 if hh < H else DK ** -0.5
        qkt_ref[hh * DK:hh * DK + half, :] = (x1 * cos[0:half] - x2 * sin[0:half]) * scale
        qkt_ref[hh * DK + half:(hh + 1) * DK, :] = (x2 * cos[half:DK] + x1 * sin[half:DK]) * scale


def _dec_inproj(x, g, w_in, w_qkt, inv_t, n_main, *, H, DK):
    n, D = x.shape
    kern = functools.partial(_dec_inproj_kernel, H=H, DK=DK, n_main=n_main,
                             pos0=float(PAST_LEN))
    return pl.pallas_call(
        kern,
        out_shape=(jax.ShapeDtypeStruct((n, n_main), F32),
                   jax.ShapeDtypeStruct((n, w_in.shape[1] - n_main), F32),
                   jax.ShapeDtypeStruct((w_qkt.shape[0], n), F32)),
        compiler_params=pltpu.CompilerParams(vmem_limit_bytes=VMEM_LIMIT),
        name="dec_inproj",
    )(x, g, w_in, w_qkt, inv_t)


def _dec_state_kernel(zm_ref, qkt_ref, conv_ref, cw_ref, s_ref, *rest, BB, H, DK, DV, DC,
                      log_gamma):
    ain_ref, rin_ref, nconv_ref, ns_ref, o_scr = rest[-5:]
    DR = H * DK
    zm = zm_ref[...]
    cb, u = zm[:, 0:DC], zm[:, DC:2 * DC] * zm[:, 2 * DC:3 * DC]
    buf = conv_ref[...]
    b0, b1 = buf[:, 0:DC], buf[:, DC:2 * DC]
    cw = cw_ref[...]
    conv = cw[0:1, :] * b0 + cw[1:2, :] * b1 + cw[2:3, :] * u
    nconv_ref[:, 0:DC] = b1
    nconv_ref[:, DC:2 * DC] = u
    ain_ref[...] = cb * conv

    qkt = qkt_ref[...]
    v = zm[:, 3 * DC + 2 * DR:3 * DC + 3 * DR]
    g = zm[:, 3 * DC + 3 * DR:3 * DC + 4 * DR]
    for j in range(BB):
        for h in range(H):
            gamma = math.exp(log_gamma[h])
            qc = qkt[h * DK:(h + 1) * DK, j:j + 1]
            kc = qkt[DR + h * DK:DR + (h + 1) * DK, j:j + 1]
            s_prev = s_ref[j, h]
            vrow = v[j:j + 1, h * DV:(h + 1) * DV]
            qk = jnp.sum(qc * kc, axis=0, keepdims=True)
            inter = jnp.sum(qc * s_prev, axis=0, keepdims=True) * gamma
            o_scr[j:j + 1, h * DV:(h + 1) * DV] = qk * vrow + inter
            ns_ref[j, h] = gamma * s_prev + kc * vrow
    o = o_scr[...]
    for h in range(H):
        oh = o[:, h * DV:(h + 1) * DV]
        mu = jnp.mean(oh, axis=-1, keepdims=True)
        d = oh - mu
        var = jnp.mean(d * d, axis=-1, keepdims=True)
        rin_ref[:, h * DV:(h + 1) * DV] = _silu(g[:, h * DV:(h + 1) * DV]) * (d * lax.rsqrt(var + EPS))


def _dec_state(zm, qkt3, conv2d, cw, state_all, new_states, layer, *, H, DK, DV):
    n = zm.shape[0]
    depth = state_all.shape[0]
    DC = cw.shape[1]
    BB = DEC_BB
    kern = functools.partial(_dec_state_kernel, BB=BB, H=H, DK=DK, DV=DV, DC=DC,
                             log_gamma=_log_gammas(H))
    args = [zm, qkt3, conv2d, cw, state_all, new_states]
    in_specs = [
        pl.BlockSpec((BB, zm.shape[1]), lambda b: (b, 0)),
        pl.BlockSpec((None, qkt3.shape[1], BB), lambda b: (b, 0, 0)),
        pl.BlockSpec((BB, 2 * DC), lambda b: (b, 0)),
        pl.BlockSpec(cw.shape, lambda b: (0, 0)),
        pl.BlockSpec((None, BB, H, DK, DV), lambda b: (layer, b, 0, 0, 0)),
        pl.BlockSpec(memory_space=pl.ANY),
    ]
    aliases = {5: 3}
    return pl.pallas_call(
        kern,
        grid=(n // BB,),
        in_specs=in_specs,
        out_specs=[
            pl.BlockSpec((BB, DC), lambda b: (b, 0)),
            pl.BlockSpec((BB, H * DV), lambda b: (b, 0)),
            pl.BlockSpec((BB, 2 * DC), lambda b: (b, 0)),
            pl.BlockSpec((None, BB, H, DK, DV), lambda b: (layer, b, 0, 0, 0)),
        ],
        out_shape=(jax.ShapeDtypeStruct((n, DC), F32),
                   jax.ShapeDtypeStruct((n, H * DV), F32),
                   jax.ShapeDtypeStruct((n, 2 * DC), F32),
                   jax.ShapeDtypeStruct((depth, n, H, DK, DV), F32)),
        scratch_shapes=[pltpu.VMEM((BB, H * DV), F32)],
        input_output_aliases=aliases,
        compiler_params=_cparams(("arbitrary",)),
        name="dec_state",
    )(*args)


def _dec_outproj_kernel(x_ref, zg_ref, ain_ref, rin_ref, wco_ref, wro_ref, wo_ref, out_ref, *, D):
    a = _dot(ain_ref[...].astype(BF16), wco_ref[...])
    r = _dot(rin_ref[...].astype(BF16), wro_ref[...])
    zg = zg_ref[...]
    m = _sigmoid(zg[:, 0:D]) * a + _sigmoid(zg[:, D:2 * D]) * r
    out_ref[...] = x_ref[...] + _dot(m.astype(BF16), wo_ref[...])


def _dec_outproj(x, zg, ain, rin, w_co, w_ro, w_o):
    n, D = x.shape
    return pl.pallas_call(
        functools.partial(_dec_outproj_kernel, D=D),
        out_shape=jax.ShapeDtypeStruct((n, D), F32),
        compiler_params=pltpu.CompilerParams(vmem_limit_bytes=VMEM_LIMIT),
        name="dec_outproj",
    )(x, zg, ain, rin, w_co, w_ro, w_o)


def _ffn_dense_kernel(x_ref, g_ref, wg_ref, wu_ref, wd_ref, out_ref, hn_ref):
    @pl.when(pl.program_id(1) == 0)
    def _init():
        x = x_ref[...]
        hn_ref[...] = _rms(x, g_ref[...]).astype(BF16)
        out_ref[...] = x

    hn = hn_ref[...]
    act = (_silu(_dot(hn, wg_ref[...])) * _dot(hn, wu_ref[...])).astype(BF16)
    out_ref[...] += _dot(act, wd_ref[...])


def _ffn_dense(x, g, w_gu, w_down):
    n, D = x.shape
    FF = w_down.shape[0]
    tm = min(FFN_TM, n)
    fc = FFN_FC
    nf = FF // fc
    return pl.pallas_call(
        _ffn_dense_kernel,
        grid=(n // tm, nf),
        in_specs=[
            pl.BlockSpec((tm, D), lambda i, j: (i, 0)),
            pl.BlockSpec((1, D), lambda i, j: (0, 0)),
            pl.BlockSpec((D, fc), lambda i, j: (0, j)),
            pl.BlockSpec((D, fc), lambda i, j: (0, nf + j)),
            pl.BlockSpec((fc, D), lambda i, j: (j, 0)),
        ],
        out_specs=pl.BlockSpec((tm, D), lambda i, j: (i, 0)),
        out_shape=jax.ShapeDtypeStruct((n, D), F32),
        scratch_shapes=[pltpu.VMEM((tm, D), BF16)],
        compiler_params=_cparams(("arbitrary", "arbitrary")),
        name="ffn_dense",
    )(x, g, w_gu, w_gu, w_down)


def _router_kernel(x_ref, g_ref, wrt_ref, hn_ref, route_ref, cnt_ref, *, T, E, CH):
    hn = _rms(x_ref[...], g_ref[...])
    hn_ref[...] = hn
    logits = _dot_nt(wrt_ref[...], hn, precision=lax.Precision.HIGHEST)
    eidx = lax.broadcasted_iota(jnp.int32, (E, T), 0)
    m1 = jnp.max(logits, axis=0, keepdims=True)
    i1 = jnp.min(jnp.where(logits == m1, eidx, E), axis=0, keepdims=True)
    sel1 = eidx == i1
    rest = jnp.where(sel1, -jnp.inf, logits)
    m2 = jnp.max(rest, axis=0, keepdims=True)
    i2 = jnp.min(jnp.where(rest == m2, eidx, E), axis=0, keepdims=True)
    sel2 = eidx == i2
    e2 = jnp.exp(m2 - m1)
    w1 = 1.0 / (1.0 + e2)
    f1 = jnp.where(sel1, 1.0, 0.0)
    f2 = jnp.where(sel2, 1.0, 0.0)
    sel = f1 + f2
    selb = sel.astype(BF16)
    route_ref[0:1, :] = i1.astype(F32)
    route_ref[1:2, :] = i2.astype(F32)
    route_ref[4:5, :] = w1
    route_ref[5:6, :] = e2 * w1
    route_ref[6:8, :] = jnp.zeros((2, T), F32)
    for cidx in range(T // CH):
        cols = slice(cidx * CH, (cidx + 1) * CH)
        src = lax.broadcasted_iota(jnp.int32, (T, CH), 0)
        dst = lax.broadcasted_iota(jnp.int32, (T, CH), 1) + cidx * CH
        tri = jnp.where(src < dst, 1.0, 0.0).astype(BF16)
        pos = _dot(selb, tri)
        route_ref[2:3, cols] = jnp.sum(pos * f1[:, cols], axis=0, keepdims=True)
        route_ref[3:4, cols] = jnp.sum(pos * f2[:, cols], axis=0, keepdims=True)
    cnt = jnp.sum(sel, axis=1, keepdims=True)
    cnt_ref[...] = jnp.broadcast_to(cnt, cnt_ref.shape).astype(jnp.int32)


def _router(x, g, w_router_t, T):
    n, D = x.shape
    E = w_router_t.shape[0]
    nt = n // T
    kern = functools.partial(_router_kernel, T=T, E=E, CH=min(256, T))
    return pl.pallas_call(
        kern,
        grid=(nt,),
        in_specs=[
            pl.BlockSpec((T, D), lambda i: (i, 0)),
            pl.BlockSpec((1, D), lambda i: (0, 0)),
            pl.BlockSpec((E, D), lambda i: (0, 0)),
        ],
        out_specs=[
            pl.BlockSpec((T, D), lambda i: (i, 0)),
            pl.BlockSpec((8, T), lambda i: (0, i)),
            pl.BlockSpec((None, E, 128), lambda i: (i, 0, 0)),
        ],
        out_shape=(jax.ShapeDtypeStruct((n, D), F32),
                   jax.ShapeDtypeStruct((8, n), F32),
                   jax.ShapeDtypeStruct((nt, E, 128), jnp.int32)),
        compiler_params=_cparams(("arbitrary",)),
        name="moe_router",
    )(x, g, w_router_t)


def _route_plan(counts, tm, n_tiles_max):
    E = counts.shape[1]
    tot = jnp.sum(counts, axis=0)
    tiles_e = (tot + (tm - 1)) // tm
    tile_end = jnp.cumsum(tiles_e)
    tile_start = tile_end - tiles_e
    base = (tile_start * tm)[None, :] + jnp.cumsum(counts, axis=0) - counts
    r = jnp.arange(n_tiles_max, dtype=jnp.int32)
    tile_e = jnp.minimum(jnp.sum(r[:, None] >= tile_end[None, :], axis=1), E - 1).astype(jnp.int32)
    n_used = tile_end[E - 1:E].astype(jnp.int32)
    valid = jnp.clip(tot[tile_e] - (r - tile_start[tile_e]) * tm, 0, tm)
    valid = jnp.where(r < n_used[0], valid, 0).astype(jnp.int32)
    return base.astype(jnp.int32), tile_e, valid, n_used


def _global_slots(route, base, T):
    nt, E = base.shape
    idx = route[0:2].astype(jnp.int32).reshape(2, nt, T)
    slot = route[2:4].astype(jnp.int32).reshape(2, nt, T)
    onehot = idx[..., None] == jnp.arange(E, dtype=jnp.int32)
    start = jnp.sum(jnp.where(onehot, base[None, :, None, :], 0), axis=-1)
    return (start + slot).reshape(2, nt * T)


SUBLANES = 8


def _for_each_row(n_rows, fn):
    def body(q, carry):
        for u in range(SUBLANES):
            fn(q, u, q * SUBLANES + u)
        return carry
    lax.fori_loop(0, n_rows // SUBLANES, body, 0)


def _dispatch_kernel(g_ref, hn_ref, xs_in_ref, xs_ref, sem, *, TD):
    del xs_in_ref

    def send(q, u, t):
        for k in range(TOP_K):
            pltpu.make_async_copy(hn_ref.at[q, pl.ds(u, 1)], xs_ref.at[pl.ds(g_ref[k, t], 1)],
                                  sem).start()
    _for_each_row(TD, send)
    for k in range(TOP_K):
        pltpu.make_async_copy(hn_ref, hn_ref, sem).wait()


def _dispatch(g, hn, xs):
    n, D = hn.shape
    TD = min(MOE_TD, n)
    return pl.pallas_call(
        functools.partial(_dispatch_kernel, TD=TD),
        grid=(n // TD,),
        in_specs=[
            pl.BlockSpec((TOP_K, TD), lambda i: (0, i), memory_space=pltpu.SMEM),
            pl.BlockSpec((TD // SUBLANES, SUBLANES, D), lambda i: (i, 0, 0)),
            pl.BlockSpec(memory_space=pl.ANY),
        ],
        out_specs=pl.BlockSpec(memory_space=pl.ANY),
        out_shape=jax.ShapeDtypeStruct(xs.shape, xs.dtype),
        scratch_shapes=[pltpu.SemaphoreType.DMA(())],
        input_output_aliases={2: 0},
        compiler_params=_cparams(("arbitrary",)),
        name="moe_dispatch",
    )(g, hn.reshape(n // SUBLANES, SUBLANES, D), xs)


def _experts_kernel(te_ref, valid_ref, nused_ref, x_ref, wg_ref, wu_ref, wd_ref, y_ref,
                    wgb_ref, wub_ref, wdb_ref, *, TM, R):
    r, j = pl.program_id(0), pl.program_id(1)
    valid = jnp.where(r < nused_ref[0], valid_ref[r], 0)

    @pl.when((valid == 0) & (j == 0))
    def _idle():
        y_ref[...] = jnp.zeros_like(y_ref)

    def swiglu(xs, wg, wu, wd):
        act = (_silu(_dot(xs, wg)) * _dot(xs, wu)).astype(BF16)
        return _dot(act, wd)

    def full_tile():
        return swiglu(x_ref[...].astype(BF16), wg_ref[...].astype(BF16),
                      wu_ref[...].astype(BF16), wd_ref[...].astype(BF16))

    @pl.when((valid == TM) & (j == 0))
    def _full_first():
        y_ref[...] = full_tile()

    @pl.when((valid == TM) & (j > 0))
    def _full_rest():
        y_ref[...] += full_tile()

    @pl.when((valid > 0) & (valid < TM))
    def _partial():
        wgb_ref[...] = wg_ref[...].astype(BF16)
        wub_ref[...] = wu_ref[...].astype(BF16)
        wdb_ref[...] = wd_ref[...].astype(BF16)
        nch = lax.shift_right_logical(valid + (R - 1), R.bit_length() - 1)

        def rows_of(c):
            return pl.ds(pl.multiple_of(c * R, R), R)

        def chunk(c):
            return swiglu(x_ref[rows_of(c), :].astype(BF16), wgb_ref[...], wub_ref[...],
                          wdb_ref[...])

        @pl.when(j == 0)
        def _first():
            def body(c, carry):
                y_ref[rows_of(c), :] = chunk(c)
                return carry
            lax.fori_loop(0, nch, body, 0)
            for c in range(TM // R):
                @pl.when(c >= nch)
                def _pad():
                    y_ref[c * R:(c + 1) * R, :] = jnp.zeros((R, y_ref.shape[1]), F32)

        @pl.when(j > 0)
        def _rest():
            def body(c, carry):
                y_ref[rows_of(c), :] += chunk(c)
                return carry
            lax.fori_loop(0, nch, body, 0)


def _experts(tile_e, valid, n_used, xs, w_gu_all, w_down_all, layer):
    rows, D = xs.shape
    FF = w_down_all.shape[2]
    TM, R, fc = MOE_TM, MOE_ROWS, MOE_FC
    nf = FF // fc
    nt = rows // TM

    def tile_of(r, nu):
        return jnp.maximum(jnp.minimum(r, nu[0] - 1), 0)

    def chunk_of(r, j, nu):
        return jnp.where(r < nu[0], j, nf - 1)

    grid_spec = pltpu.PrefetchScalarGridSpec(
        num_scalar_prefetch=3,
        grid=(nt, nf),
        in_specs=[
            pl.BlockSpec((TM, D), lambda r, j, te, vl, nu: (tile_of(r, nu), 0)),
            pl.BlockSpec((None, None, D, fc),
                         lambda r, j, te, vl, nu: (layer, te[tile_of(r, nu)], 0, chunk_of(r, j, nu))),
            pl.BlockSpec((None, None, D, fc),
                         lambda r, j, te, vl, nu: (layer, te[tile_of(r, nu)], 0,
                                                   nf + chunk_of(r, j, nu))),
            pl.BlockSpec((None, None, fc, D),
                         lambda r, j, te, vl, nu: (layer, te[tile_of(r, nu)], chunk_of(r, j, nu), 0)),
        ],
        out_specs=pl.BlockSpec((TM, D), lambda r, j, te, vl, nu: (r, 0)),
        scratch_shapes=[pltpu.VMEM((D, fc), BF16), pltpu.VMEM((D, fc), BF16),
                        pltpu.VMEM((fc, D), BF16)],
    )
    return pl.pallas_call(
        functools.partial(_experts_kernel, TM=TM, R=R),
        grid_spec=grid_spec,
        out_shape=jax.ShapeDtypeStruct((rows, D), F32),
        compiler_params=_cparams(("arbitrary", "arbitrary")),
        name="moe_experts",
    )(tile_e, valid, n_used, xs, w_gu_all, w_gu_all, w_down_all)


def _ple_kernel(*refs, moe, final, tm):
    refs = list(refs)
    if moe:
        g_ref, gnext_ref, x_ref, w_ref, ys_ref = refs[0:5]
        refs = refs[5:]
        buf_ref, sem = refs[-2:]
        refs = refs[:-2]
    else:
        x_ref = refs.pop(0)
    p_ref, gn_ref, wgate_ref, wple_ref = refs[0:4]
    gfin_ref = refs[4] if final else None
    out_ref = refs[-1]

    x = x_ref[...]
    if moe:
        i = pl.program_id(0)
        slot = lax.rem(i, 2)

        def fetch(slots_ref, s):
            def get(q, u, t):
                for k in range(TOP_K):
                    pltpu.make_async_copy(ys_ref.at[pl.ds(slots_ref[k, t], 1)],
                                          buf_ref.at[s, k, q, pl.ds(u, 1)], sem.at[s]).start()
            _for_each_row(tm, get)

        @pl.when(i == 0)
        def _first():
            fetch(g_ref, 0)

        @pl.when(i + 1 < pl.num_programs(0))
        def _ahead():
            fetch(gnext_ref, 1 - slot)

        for k in range(TOP_K):
            pltpu.make_async_copy(buf_ref.at[slot, k], buf_ref.at[slot, k], sem.at[slot]).wait()
        w = w_ref[...]
        for k in range(TOP_K):
            x = x + w[:, k:k + 1] * buf_ref[slot, k].reshape(x.shape)
    gate = _sigmoid(_dot(_rms(x, gn_ref[...]).astype(BF16), wgate_ref[...]))
    y = x + gate * _dot(p_ref[...].astype(BF16), wple_ref[...])
    if final:
        y = _rms(y, gfin_ref[...])
    out_ref[...] = y


def _ple(x, moe, p_all, layer, g, w_gate, w_ple, g_final):
    n, D = x.shape
    DP = p_all.shape[-1]
    tm = min(PLE_TM, n)
    tok = pl.BlockSpec((tm, D), lambda i: (i, 0))
    const = lambda *shape: pl.BlockSpec(shape, lambda i: (0,) * len(shape))
    args, specs, scratch = [], [], []
    if moe is not None:
        slots, weights, ys = moe
        last = n // tm - 1
        args += [slots, slots, x, weights, ys]
        specs += [pl.BlockSpec((TOP_K, tm), lambda i: (0, i), memory_space=pltpu.SMEM),
                  pl.BlockSpec((TOP_K, tm), lambda i: (0, jnp.minimum(i + 1, last)),
                               memory_space=pltpu.SMEM),
                  tok, pl.BlockSpec((tm, TOP_K), lambda i: (i, 0)),
                  pl.BlockSpec(memory_space=pl.ANY)]
        scratch = [pltpu.VMEM((2, TOP_K, tm // SUBLANES, SUBLANES, D), F32),
                   pltpu.SemaphoreType.DMA((2,))]
    else:
        args.append(x)
        specs.append(tok)
    args += [p_all, g, w_gate, w_ple]
    specs += [pl.BlockSpec((None, tm, DP), lambda i: (layer, i, 0)), const(1, D),
              const(D, D), const(DP, D)]
    if g_final is not None:
        args.append(g_final)
        specs.append(const(1, D))
    kern = functools.partial(_ple_kernel, moe=moe is not None, final=g_final is not None, tm=tm)
    return pl.pallas_call(
        kern,
        grid=(n // tm,),
        in_specs=specs,
        out_specs=tok,
        out_shape=jax.ShapeDtypeStruct((n, D), F32),
        scratch_shapes=scratch,
        compiler_params=_cparams(("arbitrary",)),
        name="ple_moe" if moe is not None else "ple",
    )(*args)


def kernel(x_prompt, x_sample, state_conv, state_ret, p_prompt, p_sample, norm_mix_g, w_in,
           conv_w, w_conv_out, w_ret_out, w_o, norm_ffn_g, w_dense_gu, w_dense_down, w_router,
           w_exp_gu, w_exp_down, norm_ple_g, w_ple, w_ple_gate, norm_final_g):
    B, T, D = x_prompt.shape
    NP = B * T
    NS = x_sample.shape[0]
    depth = w_in.shape[0]
    _, _, H, DK, DV = state_ret.shape
    DC = conv_w.shape[-1]
    DR = H * DK
    E = w_router.shape[-1]
    assert x_sample.shape[1] == 1 and CONV_W - 1 == state_conv.shape[2]
    assert T % MIX_CHUNK == 0 and NS % DEC_BB == 0 and NP % ROUTE_TILE == 0

    half = DK // 2
    inv = ROPE_BASE ** (-jnp.arange(half, dtype=F32) / half)
    inv_dup = jnp.concatenate([inv, inv]).reshape(1, DK)
    sgn = jnp.concatenate([-jnp.ones((half,), F32), jnp.ones((half,), F32)]).reshape(1, DK)
    pos = jnp.arange(T, dtype=F32).reshape(T, 1)
    cos_p, sin_p = _rope_tables(pos, inv_dup, sgn)
    inv_t = jnp.broadcast_to(inv_dup.reshape(DK, 1), (DK, NS))

    n_tiles_max = (TOP_K * (NP + NS)) // MOE_TM + E

    row = lambda a: a.reshape(1, -1)
    xp = x_prompt
    xs = x_sample.reshape(NS, D)
    pp = p_prompt.reshape(depth, NP, -1)
    ps = p_sample.reshape(depth, NS, -1)
    conv_p, ret_p, conv_s = [], [], []
    ret_s = jnp.zeros(state_ret.shape, F32)
    for i in range(depth):
        w_in_b = w_in[i].astype(BF16)
        w_co_b = w_conv_out[i].astype(BF16)
        w_ro_b = w_ret_out[i].astype(BF16)
        w_o_b = w_o[i].astype(BF16)
        g_mix = row(norm_mix_g[i])

        xp, cst, rst = _mixer_prompt(xp, cos_p, sin_p, g_mix, w_in_b, conv_w[i], w_co_b, w_ro_b,
                                     w_o_b, H=H, DK=DK, DV=DV)
        conv_p.append(cst[:, 8 - (CONV_W - 1):, :])
        ret_p.append(rst)

        n_main = 3 * DC + 4 * DR
        w_qkt = w_in_b[:, 3 * DC:3 * DC + 2 * DR].T
        zm, zg, qkt = _dec_inproj(xs, g_mix, w_in_b, w_qkt, inv_t, n_main, H=H, DK=DK)
        qkt3 = qkt.reshape(2 * DR, NS // DEC_BB, DEC_BB).transpose(1, 0, 2)
        ain, rin, ncv, ret_s = _dec_state(zm, qkt3, state_conv[i].reshape(NS, -1), conv_w[i],
                                          state_ret, ret_s, i, H=H, DK=DK, DV=DV)
        xs = _dec_outproj(xs, zg, ain, rin, w_co_b, w_ro_b, w_o_b)
        conv_s.append(ncv.reshape(NS, CONV_W - 1, DC))

        g_ffn = row(norm_ffn_g[i])
        xp2 = xp.reshape(NP, D)
        if i % 2 == 0:
            w_gu_b = w_dense_gu[i // 2].astype(BF16)
            w_dn_b = w_dense_down[i // 2].astype(BF16)
            xp2 = _ffn_dense(xp2, g_ffn, w_gu_b, w_dn_b)
            xs = _ffn_dense(xs, g_ffn, w_gu_b, w_dn_b)
            moe_p = moe_s = None
        else:
            w_rt = w_router[i // 2].T
            hn_p, route_p, cnt_p = _router(xp2, g_ffn, w_rt, ROUTE_TILE)
            hn_s, route_s, cnt_s = _router(xs, g_ffn, w_rt, NS)
            counts = jnp.concatenate([cnt_p[:, :, 0], cnt_s[:, :, 0]], axis=0)
            base, tile_e, valid, n_used = _route_plan(counts, MOE_TM, n_tiles_max)
            ntp = NP // ROUTE_TILE
            slots_p = _global_slots(route_p, base[:ntp], ROUTE_TILE)
            slots_s = _global_slots(route_s, base[ntp:], NS)
            sorted_x = jnp.zeros((n_tiles_max * MOE_TM, D), F32)
            sorted_x = _dispatch(slots_p, hn_p, sorted_x)
            sorted_x = _dispatch(slots_s, hn_s, sorted_x)
            sorted_y = _experts(tile_e, valid, n_used, sorted_x, w_exp_gu, w_exp_down, i // 2)
            moe_p = (slots_p, route_p[4:6].T, sorted_y)
            moe_s = (slots_s, route_s[4:6].T, sorted_y)

        g_fin = row(norm_final_g) if i == depth - 1 else None
        w_pg_b = w_ple_gate[i].astype(BF16)
        w_pl_b = w_ple[i].astype(BF16)
        xp = _ple(xp2, moe_p, pp, i, row(norm_ple_g[i]), w_pg_b, w_pl_b, g_fin).reshape(B, T, D)
        xs = _ple(xs, moe_s, ps, i, row(norm_ple_g[i]), w_pg_b, w_pl_b, g_fin)

    return (xp, xs.reshape(NS, 1, D), jnp.stack(conv_p), jnp.stack(ret_p),
            jnp.stack(conv_s), ret_s)
```

```python
import functools
import math

import jax
import jax.numpy as jnp
from jax import lax
from jax.experimental import pallas as pl
from jax.experimental.pallas import tpu as pltpu

F32 = jnp.float32
BF16 = jnp.bfloat16

CONV_W = 3
TOP_K = 2
ROPE_BASE = 10000.0
EPS = 1e-6
PAST_LEN = 16384

V7X_VMEM_BYTES = 64 * 1024 * 1024
VMEM_LIMIT = V7X_VMEM_BYTES - 8 * 1024 * 1024

MIX_CHUNK = 256
FFN_TM = 1024
FFN_FC = 512
ROUTE_TILE = 2048
MOE_TM = 1024
MOE_ROWS = 256
MOE_FC = 512
MOE_TD = 512
PLE_TM = 512
DEC_BB = 16


def _cparams(sem):
    return pltpu.CompilerParams(dimension_semantics=sem, vmem_limit_bytes=VMEM_LIMIT)


def _dot(a, b):
    return jnp.dot(a, b, preferred_element_type=F32)


def _dot_nt(a, b, precision=None):
    return lax.dot_general(a, b, (((1,), (1,)), ((), ())), precision=precision,
                           preferred_element_type=F32)


def _dot_tn(a, b):
    return lax.dot_general(a, b, (((0,), (0,)), ((), ())), preferred_element_type=F32)


def _rms(x, g):
    return x * lax.rsqrt(jnp.mean(x * x, axis=-1, keepdims=True) + EPS) * g


def _sigmoid(x):
    return 1.0 / (1.0 + jnp.exp(-x))


def _silu(x):
    return x * _sigmoid(x)


def _log_gammas(n_heads):
    return tuple(math.log(1.0 - 2.0 ** (-5.0 - h)) for h in range(n_heads))


def _rope_table_kernel(pos_ref, inv_ref, sgn_ref, cos_ref, sin_ref):
    ang = pos_ref[...] * inv_ref[...]
    cos_ref[...] = jnp.cos(ang)
    sin_ref[...] = jnp.sin(ang) * sgn_ref[...]


def _rope_tables(pos, inv_dup, sgn):
    n = pos.shape[0]
    dk = inv_dup.shape[1]
    return pl.pallas_call(
        _rope_table_kernel,
        out_shape=(jax.ShapeDtypeStruct((n, dk), F32), jax.ShapeDtypeStruct((n, dk), F32)),
        name="rope_tables",
    )(pos, inv_dup, sgn)


def _mixer_prompt_kernel(x_ref, cos_ref, sin_ref, g_ref, win_ref, cw_ref, wco_ref, wro_ref,
                         wo_ref, out_ref, cst_ref, rst_ref, tail_ref, state_ref, decay_ref,
                         *, L, H, DK, DV, DC, D, log_gamma):
    c = pl.program_id(1)
    DR = H * DK

    @pl.when(c == 0)
    def _init():
        tail_ref[...] = jnp.zeros_like(tail_ref)
        state_ref[...] = jnp.zeros_like(state_ref)
        diff = (lax.broadcasted_iota(jnp.int32, (L, L), 0)
                - lax.broadcasted_iota(jnp.int32, (L, L), 1)).astype(F32)
        for h in range(H):
            decay_ref[h] = jnp.where(diff >= 0.0,
                                     jnp.exp(jnp.maximum(diff, 0.0) * log_gamma[h]), 0.0)

    x = x_ref[...]
    xn = _rms(x, g_ref[...]).astype(BF16)

    zc = _dot(xn, win_ref[:, 0:3 * DC])
    cb, u = zc[:, 0:DC], zc[:, DC:2 * DC] * zc[:, 2 * DC:3 * DC]
    tail = tail_ref[...]
    p1, p2 = tail[7:8, :], tail[6:7, :]
    row = lax.broadcasted_iota(jnp.int32, (L, DC), 0)
    u1 = jnp.where(row == 0, p1, pltpu.roll(u, 1, axis=0))
    u2 = jnp.where(row == 0, p2, jnp.where(row == 1, p1, pltpu.roll(u, 2, axis=0)))
    cw = cw_ref[...]
    conv = cw[0:1, :] * u2 + cw[1:2, :] * u1 + cw[2:3, :] * u
    tail_ref[...] = u[L - 8:L, :]
    a = _dot((cb * conv).astype(BF16), wco_ref[...])

    zr = _dot(xn, win_ref[:, 3 * DC:3 * DC + 4 * DR])
    cos, sin = cos_ref[...], sin_ref[...]
    ridx_k = lax.broadcasted_iota(jnp.int32, (L, DK), 0).astype(F32)
    ridx_v = lax.broadcasted_iota(jnp.int32, (L, DV), 0).astype(F32)
    heads = []
    for h in range(H):
        lg = log_gamma[h]
        qh = zr[:, h * DK:(h + 1) * DK]
        kh = zr[:, DR + h * DK:DR + (h + 1) * DK]
        vh = zr[:, 2 * DR + h * DV:2 * DR + (h + 1) * DV]
        gh = zr[:, 3 * DR + h * DV:3 * DR + (h + 1) * DV]
        qh = qh * cos + pltpu.roll(qh, DK // 2, axis=1) * sin
        kh = (kh * cos + pltpu.roll(kh, DK // 2, axis=1) * sin) * (DK ** -0.5)
        qb, vb = qh.astype(BF16), vh.astype(BF16)
        s_prev = state_ref[h]
        scores = _dot_nt(qb, kh.astype(BF16)) * decay_ref[h]
        o = _dot(scores.astype(BF16), vb)
        o = o + _dot(qb, s_prev.astype(BF16)) * jnp.exp((ridx_v + 1.0) * lg)
        k_dec = jnp.exp((L - 1.0 - ridx_k) * lg)
        state_ref[h] = math.exp(L * lg) * s_prev + _dot_tn((kh * k_dec).astype(BF16), vb)
        mu = jnp.mean(o, axis=-1, keepdims=True)
        d = o - mu
        var = jnp.mean(d * d, axis=-1, keepdims=True)
        heads.append(_silu(gh) * (d * lax.rsqrt(var + EPS)))
    r = _dot(jnp.concatenate(heads, axis=-1).astype(BF16), wro_ref[...])

    zg = _dot(xn, win_ref[:, 3 * DC + 4 * DR:3 * DC + 4 * DR + 2 * D])
    m = _sigmoid(zg[:, 0:D]) * a + _sigmoid(zg[:, D:2 * D]) * r
    out_ref[...] = x + _dot(m.astype(BF16), wo_ref[...])

    @pl.when(c == pl.num_programs(1) - 1)
    def _final():
        cst_ref[...] = u[L - 8:L, :]
        rst_ref[...] = state_ref[...]


def _mixer_prompt(x, cos, sin, g, w_in_all, layer, cw, w_co, w_ro, w_o, *, H, DK, DV):
    B, T, D = x.shape
    DC = cw.shape[1]
    L = MIX_CHUNK
    const = lambda *shape: pl.BlockSpec(shape, lambda b, c: (0,) * len(shape))
    kern = functools.partial(_mixer_prompt_kernel, L=L, H=H, DK=DK, DV=DV, DC=DC, D=D,
                             log_gamma=_log_gammas(H))
    return pl.pallas_call(
        kern,
        grid=(B, T // L),
        in_specs=[
            pl.BlockSpec((None, L, D), lambda b, c: (b, c, 0)),
            pl.BlockSpec((L, DK), lambda b, c: (c, 0)),
            pl.BlockSpec((L, DK), lambda b, c: (c, 0)),
            const(1, D),
            pl.BlockSpec((None,) + w_in_all.shape[1:], lambda b, c: (layer, 0, 0)),
            const(*cw.shape), const(*w_co.shape),
            const(*w_ro.shape), const(*w_o.shape),
        ],
        out_specs=[
            pl.BlockSpec((None, L, D), lambda b, c: (b, c, 0)),
            pl.BlockSpec((None, 8, DC), lambda b, c: (b, 0, 0)),
            pl.BlockSpec((None, H, DK, DV), lambda b, c: (b, 0, 0, 0)),
        ],
        out_shape=(jax.ShapeDtypeStruct((B, T, D), F32),
                   jax.ShapeDtypeStruct((B, 8, DC), F32),
                   jax.ShapeDtypeStruct((B, H, DK, DV), F32)),
        scratch_shapes=[pltpu.VMEM((8, DC), F32), pltpu.VMEM((H, DK, DV), F32),
                        pltpu.VMEM((H, L, L), F32)],
        compiler_params=_cparams(("arbitrary", "arbitrary")),
        name="mixer_prompt",
    )(x, cos, sin, g, w_in_all, cw, w_co, w_ro, w_o)


def _dec_inproj_kernel(x_ref, g_ref, win_ref, wqkt_ref, invt_ref, zm_ref, zg_ref, qkt_ref,
                       *, H, DK, n_main, pos0):
    xn = _rms(x_ref[...], g_ref[...]).astype(BF16)
    zm_ref[...] = _dot(xn, win_ref[:, 0:n_main])
    zg_ref[...] = _dot(xn, win_ref[:, n_main:win_ref.shape[1]])
    qkt = _dot_nt(wqkt_ref[...], xn)
    ang = pos0 * invt_ref[...]
    cos, sin = jnp.cos(ang), jnp.sin(ang)
    half = DK // 2
    for hh in range(2 * H):
        blk = qkt[hh * DK:(hh + 1) * DK, :]
        x1, x2 = blk[0:half, :], blk[half:DK, :]
        scale = 1.0 if hh < H else DK ** -0.5
        qkt_ref[hh * DK:hh * DK + half, :] = (x1 * cos[0:half] - x2 * sin[0:half]) * scale
        qkt_ref[hh * DK + half:(hh + 1) * DK, :] = (x2 * cos[half:DK] + x1 * sin[half:DK]) * scale


def _dec_inproj(x, g, w_in_all, layer, w_qkt, inv_t, n_main, *, H, DK):
    n, D = x.shape
    n_in = w_in_all.shape[2]
    kern = functools.partial(_dec_inproj_kernel, H=H, DK=DK, n_main=n_main,
                             pos0=float(PAST_LEN))
    whole = lambda a: pl.BlockSpec(a.shape, lambda i: (0,) * a.ndim)
    out_shape = (jax.ShapeDtypeStruct((n, n_main), F32),
                 jax.ShapeDtypeStruct((n, n_in - n_main), F32),
                 jax.ShapeDtypeStruct((w_qkt.shape[0], n), F32))
    return pl.pallas_call(
        kern,
        grid=(1,),
        in_specs=[whole(x), whole(g), pl.BlockSpec((None, D, n_in), lambda i: (layer, 0, 0)),
                  whole(w_qkt), whole(inv_t)],
        out_specs=[whole(s) for s in out_shape],
        out_shape=out_shape,
        compiler_params=_cparams(("arbitrary",)),
        name="dec_inproj",
    )(x, g, w_in_all, w_qkt, inv_t)


def _dec_state_kernel(zm_ref, qkt_ref, conv_ref, cw_ref, s_ref, *rest, BB, H, DK, DV, DC,
                      log_gamma):
    ain_ref, rin_ref, nconv_ref, ns_ref, o_scr = rest[-5:]
    DR = H * DK
    zm = zm_ref[...]
    cb, u = zm[:, 0:DC], zm[:, DC:2 * DC] * zm[:, 2 * DC:3 * DC]
    buf = conv_ref[...]
    b0, b1 = buf[:, 0:DC], buf[:, DC:2 * DC]
    cw = cw_ref[...]
    conv = cw[0:1, :] * b0 + cw[1:2, :] * b1 + cw[2:3, :] * u
    nconv_ref[:, 0:DC] = b1
    nconv_ref[:, DC:2 * DC] = u
    ain_ref[...] = cb * conv

    qkt = qkt_ref[...]
    v = zm[:, 3 * DC + 2 * DR:3 * DC + 3 * DR]
    g = zm[:, 3 * DC + 3 * DR:3 * DC + 4 * DR]
    for j in range(BB):
        for h in range(H):
            gamma = math.exp(log_gamma[h])
            qc = qkt[h * DK:(h + 1) * DK, j:j + 1]
            kc = qkt[DR + h * DK:DR + (h + 1) * DK, j:j + 1]
            s_prev = s_ref[j, h]
            vrow = v[j:j + 1, h * DV:(h + 1) * DV]
            qk = jnp.sum(qc * kc, axis=0, keepdims=True)
            inter = jnp.sum(qc * s_prev, axis=0, keepdims=True) * gamma
            o_scr[j:j + 1, h * DV:(h + 1) * DV] = qk * vrow + inter
            ns_ref[j, h] = gamma * s_prev + kc * vrow
    o = o_scr[...]
    for h in range(H):
        oh = o[:, h * DV:(h + 1) * DV]
        mu = jnp.mean(oh, axis=-1, keepdims=True)
        d = oh - mu
        var = jnp.mean(d * d, axis=-1, keepdims=True)
        rin_ref[:, h * DV:(h + 1) * DV] = _silu(g[:, h * DV:(h + 1) * DV]) * (d * lax.rsqrt(var + EPS))


def _dec_state(zm, qkt3, conv2d, cw, state_all, new_states, layer, *, H, DK, DV):
    n = zm.shape[0]
    depth = state_all.shape[0]
    DC = cw.shape[1]
    BB = DEC_BB
    kern = functools.partial(_dec_state_kernel, BB=BB, H=H, DK=DK, DV=DV, DC=DC,
                             log_gamma=_log_gammas(H))
    args = [zm, qkt3, conv2d, cw, state_all, new_states]
    in_specs = [
        pl.BlockSpec((BB, zm.shape[1]), lambda b: (b, 0)),
        pl.BlockSpec((None, qkt3.shape[1], BB), lambda b: (b, 0, 0)),
        pl.BlockSpec((BB, 2 * DC), lambda b: (b, 0)),
        pl.BlockSpec(cw.shape, lambda b: (0, 0)),
        pl.BlockSpec((None, BB, H, DK, DV), lambda b: (layer, b, 0, 0, 0)),
        pl.BlockSpec(memory_space=pl.ANY),
    ]
    aliases = {5: 3}
    return pl.pallas_call(
        kern,
        grid=(n // BB,),
        in_specs=in_specs,
        out_specs=[
            pl.BlockSpec((BB, DC), lambda b: (b, 0)),
            pl.BlockSpec((BB, H * DV), lambda b: (b, 0)),
            pl.BlockSpec((BB, 2 * DC), lambda b: (b, 0)),
            pl.BlockSpec((None, BB, H, DK, DV), lambda b: (layer, b, 0, 0, 0)),
        ],
        out_shape=(jax.ShapeDtypeStruct((n, DC), F32),
                   jax.ShapeDtypeStruct((n, H * DV), F32),
                   jax.ShapeDtypeStruct((n, 2 * DC), F32),
                   jax.ShapeDtypeStruct((depth, n, H, DK, DV), F32)),
        scratch_shapes=[pltpu.VMEM((BB, H * DV), F32)],
        input_output_aliases=aliases,
        compiler_params=_cparams(("arbitrary",)),
        name="dec_state",
    )(*args)


def _dec_outproj_kernel(x_ref, zg_ref, ain_ref, rin_ref, wco_ref, wro_ref, wo_ref, out_ref, *, D):
    a = _dot(ain_ref[...].astype(BF16), wco_ref[...])
    r = _dot(rin_ref[...].astype(BF16), wro_ref[...])
    zg = zg_ref[...]
    m = _sigmoid(zg[:, 0:D]) * a + _sigmoid(zg[:, D:2 * D]) * r
    out_ref[...] = x_ref[...] + _dot(m.astype(BF16), wo_ref[...])


def _dec_outproj(x, zg, ain, rin, w_co, w_ro, w_o):
    n, D = x.shape
    return pl.pallas_call(
        functools.partial(_dec_outproj_kernel, D=D),
        out_shape=jax.ShapeDtypeStruct((n, D), F32),
        compiler_params=pltpu.CompilerParams(vmem_limit_bytes=VMEM_LIMIT),
        name="dec_outproj",
    )(x, zg, ain, rin, w_co, w_ro, w_o)


def _ffn_dense_kernel(x_ref, g_ref, wg_ref, wu_ref, wd_ref, out_ref, hn_ref):
    @pl.when(pl.program_id(1) == 0)
    def _init():
        x = x_ref[...]
        hn_ref[...] = _rms(x, g_ref[...]).astype(BF16)
        out_ref[...] = x

    hn = hn_ref[...]
    act = (_silu(_dot(hn, wg_ref[...])) * _dot(hn, wu_ref[...])).astype(BF16)
    out_ref[...] += _dot(act, wd_ref[...])


def _ffn_dense(x, g, w_gu, w_down):
    n, D = x.shape
    FF = w_down.shape[0]
    tm = min(FFN_TM, n)
    fc = FFN_FC
    nf = FF // fc
    return pl.pallas_call(
        _ffn_dense_kernel,
        grid=(n // tm, nf),
        in_specs=[
            pl.BlockSpec((tm, D), lambda i, j: (i, 0)),
            pl.BlockSpec((1, D), lambda i, j: (0, 0)),
            pl.BlockSpec((D, fc), lambda i, j: (0, j)),
            pl.BlockSpec((D, fc), lambda i, j: (0, nf + j)),
            pl.BlockSpec((fc, D), lambda i, j: (j, 0)),
        ],
        out_specs=pl.BlockSpec((tm, D), lambda i, j: (i, 0)),
        out_shape=jax.ShapeDtypeStruct((n, D), F32),
        scratch_shapes=[pltpu.VMEM((tm, D), BF16)],
        compiler_params=_cparams(("arbitrary", "arbitrary")),
        name="ffn_dense",
    )(x, g, w_gu, w_gu, w_down)


def _router_kernel(x_ref, g_ref, wrt_ref, hn_ref, route_ref, cnt_ref, *, T, E, CH):
    hn = _rms(x_ref[...], g_ref[...])
    hn_ref[...] = hn
    logits = _dot_nt(wrt_ref[...], hn, precision=lax.Precision.HIGHEST)
    eidx = lax.broadcasted_iota(jnp.int32, (E, T), 0)
    m1 = jnp.max(logits, axis=0, keepdims=True)
    i1 = jnp.min(jnp.where(logits == m1, eidx, E), axis=0, keepdims=True)
    sel1 = eidx == i1
    rest = jnp.where(sel1, -jnp.inf, logits)
    m2 = jnp.max(rest, axis=0, keepdims=True)
    i2 = jnp.min(jnp.where(rest == m2, eidx, E), axis=0, keepdims=True)
    sel2 = eidx == i2
    e2 = jnp.exp(m2 - m1)
    w1 = 1.0 / (1.0 + e2)
    f1 = jnp.where(sel1, 1.0, 0.0)
    f2 = jnp.where(sel2, 1.0, 0.0)
    sel = f1 + f2
    selb = sel.astype(BF16)
    route_ref[0:1, :] = i1.astype(F32)
    route_ref[1:2, :] = i2.astype(F32)
    route_ref[4:5, :] = w1
    route_ref[5:6, :] = e2 * w1
    route_ref[6:8, :] = jnp.zeros((2, T), F32)
    for cidx in range(T // CH):
        cols = slice(cidx * CH, (cidx + 1) * CH)
        src = lax.broadcasted_iota(jnp.int32, (T, CH), 0)
        dst = lax.broadcasted_iota(jnp.int32, (T, CH), 1) + cidx * CH
        tri = jnp.where(src < dst, 1.0, 0.0).astype(BF16)
        pos = _dot(selb, tri)
        route_ref[2:3, cols] = jnp.sum(pos * f1[:, cols], axis=0, keepdims=True)
        route_ref[3:4, cols] = jnp.sum(pos * f2[:, cols], axis=0, keepdims=True)
    cnt = jnp.sum(sel, axis=1, keepdims=True)
    cnt_ref[...] = jnp.broadcast_to(cnt, cnt_ref.shape).astype(jnp.int32)


def _router(x, g, w_router_t, T):
    n, D = x.shape
    E = w_router_t.shape[0]
    nt = n // T
    kern = functools.partial(_router_kernel, T=T, E=E, CH=min(256, T))
    return pl.pallas_call(
        kern,
        grid=(nt,),
        in_specs=[
            pl.BlockSpec((T, D), lambda i: (i, 0)),
            pl.BlockSpec((1, D), lambda i: (0, 0)),
            pl.BlockSpec((E, D), lambda i: (0, 0)),
        ],
        out_specs=[
            pl.BlockSpec((T, D), lambda i: (i, 0)),
            pl.BlockSpec((8, T), lambda i: (0, i)),
            pl.BlockSpec((None, E, 128), lambda i: (i, 0, 0)),
        ],
        out_shape=(jax.ShapeDtypeStruct((n, D), F32),
                   jax.ShapeDtypeStruct((8, n), F32),
                   jax.ShapeDtypeStruct((nt, E, 128), jnp.int32)),
        compiler_params=_cparams(("arbitrary",)),
        name="moe_router",
    )(x, g, w_router_t)


def _route_plan(counts, tm, n_tiles_max):
    E = counts.shape[1]
    tot = jnp.sum(counts, axis=0)
    tiles_e = (tot + (tm - 1)) // tm
    tile_end = jnp.cumsum(tiles_e)
    tile_start = tile_end - tiles_e
    base = (tile_start * tm)[None, :] + jnp.cumsum(counts, axis=0) - counts
    r = jnp.arange(n_tiles_max, dtype=jnp.int32)
    tile_e = jnp.minimum(jnp.sum(r[:, None] >= tile_end[None, :], axis=1), E - 1).astype(jnp.int32)
    n_used = tile_end[E - 1:E].astype(jnp.int32)
    valid = jnp.clip(tot[tile_e] - (r - tile_start[tile_e]) * tm, 0, tm)
    valid = jnp.where(r < n_used[0], valid, 0).astype(jnp.int32)
    return base.astype(jnp.int32), tile_e, valid, n_used


def _global_slots(route, base, T):
    nt, E = base.shape
    idx = route[0:TOP_K].astype(jnp.int32).reshape(TOP_K, nt, T)
    slot = route[TOP_K:2 * TOP_K].astype(jnp.int32).reshape(TOP_K, nt, T)
    onehot = idx[..., None] == jnp.arange(E, dtype=jnp.int32)
    start = jnp.sum(jnp.where(onehot, base[None, :, None, :], 0), axis=-1)
    return (start + slot).reshape(TOP_K, nt * T).T.reshape(-1)


SUBLANES = 8


def _for_each_row(n_rows, slots_ref, fn):
    def body(q, carry):
        first = q * (SUBLANES * TOP_K)
        for u in range(SUBLANES):
            fn(q, u, [slots_ref[first + (u * TOP_K + k)] for k in range(TOP_K)])
        return carry
    lax.fori_loop(0, n_rows // SUBLANES, body, 0)


def _dispatch_kernel(g_ref, hn_ref, xs_in_ref, xs_ref, sem, *, TD):
    del xs_in_ref

    def send(q, u, slots):
        for dst in slots:
            pltpu.make_async_copy(hn_ref.at[q, pl.ds(u, 1)], xs_ref.at[pl.ds(dst, 1)], sem).start()
    _for_each_row(TD, g_ref, send)
    for k in range(TOP_K):
        pltpu.make_async_copy(hn_ref, hn_ref, sem).wait()


def _dispatch(g, hn, xs):
    n, D = hn.shape
    TD = min(MOE_TD, n)
    return pl.pallas_call(
        functools.partial(_dispatch_kernel, TD=TD),
        grid=(n // TD,),
        in_specs=[
            pl.BlockSpec((TOP_K * TD,), lambda i: (i,), memory_space=pltpu.SMEM),
            pl.BlockSpec((TD // SUBLANES, SUBLANES, D), lambda i: (i, 0, 0)),
            pl.BlockSpec(memory_space=pl.ANY),
        ],
        out_specs=pl.BlockSpec(memory_space=pl.ANY),
        out_shape=jax.ShapeDtypeStruct(xs.shape, xs.dtype),
        scratch_shapes=[pltpu.SemaphoreType.DMA(())],
        input_output_aliases={2: 0},
        compiler_params=_cparams(("arbitrary",)),
        name="moe_dispatch",
    )(g, hn.reshape(n // SUBLANES, SUBLANES, D), xs)


def _experts_kernel(te_ref, valid_ref, nused_ref, x_ref, wg_ref, wu_ref, wd_ref, y_ref, *, TM, R):
    r, j = pl.program_id(0), pl.program_id(1)
    valid = jnp.where(r < nused_ref[0], valid_ref[r], 0)
    nch = lax.shift_right_logical(valid + (R - 1), R.bit_length() - 1)

    @pl.when((nch == 0) & (j == 0))
    def _idle():
        y_ref[...] = jnp.zeros_like(y_ref)

    def variant(rows):
        def part():
            xs = x_ref[0:rows, :].astype(BF16)
            act = (_silu(_dot(xs, wg_ref[...].astype(BF16)))
                   * _dot(xs, wu_ref[...].astype(BF16))).astype(BF16)
            return _dot(act, wd_ref[...].astype(BF16))

        @pl.when((nch == rows // R) & (j == 0))
        def _first():
            y_ref[0:rows, :] = part()
            if rows < TM:
                y_ref[rows:TM, :] = jnp.zeros((TM - rows, y_ref.shape[1]), F32)

        @pl.when((nch == rows // R) & (j > 0))
        def _rest():
            y_ref[0:rows, :] += part()

    for n in range(1, TM // R + 1):
        variant(n * R)


def _experts(tile_e, valid, n_used, xs, w_gu_all, w_down_all, layer):
    rows, D = xs.shape
    FF = w_down_all.shape[2]
    TM, R, fc = MOE_TM, MOE_ROWS, MOE_FC
    nf = FF // fc
    nt = rows // TM

    def tile_of(r, nu):
        return jnp.maximum(jnp.minimum(r, nu[0] - 1), 0)

    def chunk_of(r, j, nu):
        return jnp.where(r < nu[0], j, nf - 1)

    grid_spec = pltpu.PrefetchScalarGridSpec(
        num_scalar_prefetch=3,
        grid=(nt, nf),
        in_specs=[
            pl.BlockSpec((TM, D), lambda r, j, te, vl, nu: (tile_of(r, nu), 0)),
            pl.BlockSpec((None, None, D, fc),
                         lambda r, j, te, vl, nu: (layer, te[tile_of(r, nu)], 0, chunk_of(r, j, nu))),
            pl.BlockSpec((None, None, D, fc),
                         lambda r, j, te, vl, nu: (layer, te[tile_of(r, nu)], 0,
                                                   nf + chunk_of(r, j, nu))),
            pl.BlockSpec((None, None, fc, D),
                         lambda r, j, te, vl, nu: (layer, te[tile_of(r, nu)], chunk_of(r, j, nu), 0)),
        ],
        out_specs=pl.BlockSpec((TM, D), lambda r, j, te, vl, nu: (r, 0)),
    )
    return pl.pallas_call(
        functools.partial(_experts_kernel, TM=TM, R=R),
        grid_spec=grid_spec,
        out_shape=jax.ShapeDtypeStruct((rows, D), F32),
        compiler_params=_cparams(("arbitrary", "arbitrary")),
        name="moe_experts",
    )(tile_e, valid, n_used, xs, w_gu_all, w_gu_all, w_down_all)


def _ple_kernel(*refs, moe, final, tm):
    refs = list(refs)
    if moe:
        g_ref, gnext_ref, x_ref, w_ref, ys_ref = refs[0:5]
        refs = refs[5:]
        buf_ref, sem = refs[-2:]
        refs = refs[:-2]
    else:
        x_ref = refs.pop(0)
    p_ref, gn_ref, wgate_ref, wple_ref = refs[0:4]
    gfin_ref = refs[4] if final else None
    out_ref = refs[-1]

    x = x_ref[...]
    if moe:
        i = pl.program_id(0)
        slot = lax.rem(i, 2)

        def fetch(slots_ref, s):
            def get(q, u, slots):
                for k, src in enumerate(slots):
                    pltpu.make_async_copy(ys_ref.at[pl.ds(src, 1)],
                                          buf_ref.at[s, k, q, pl.ds(u, 1)], sem.at[s]).start()
            _for_each_row(tm, slots_ref, get)

        @pl.when(i == 0)
        def _first():
            fetch(g_ref, 0)

        @pl.when(i + 1 < pl.num_programs(0))
        def _ahead():
            fetch(gnext_ref, 1 - slot)

        for k in range(TOP_K):
            pltpu.make_async_copy(buf_ref.at[slot, k], buf_ref.at[slot, k], sem.at[slot]).wait()
        w = w_ref[...]
        for k in range(TOP_K):
            x = x + w[:, k:k + 1] * buf_ref[slot, k].reshape(x.shape)
    gate = _sigmoid(_dot(_rms(x, gn_ref[...]).astype(BF16), wgate_ref[...]))
    y = x + gate * _dot(p_ref[...].astype(BF16), wple_ref[...])
    if final:
        y = _rms(y, gfin_ref[...])
    out_ref[...] = y


def _ple(x, moe, p_all, layer, g, w_gate, w_ple, g_final):
    n, D = x.shape
    DP = p_all.shape[-1]
    tm = min(PLE_TM, n)
    tok = pl.BlockSpec((tm, D), lambda i: (i, 0))
    const = lambda *shape: pl.BlockSpec(shape, lambda i: (0,) * len(shape))
    args, specs, scratch = [], [], []
    if moe is not None:
        slots, weights, ys = moe
        last = n // tm - 1
        args += [slots, slots, x, weights, ys]
        specs += [pl.BlockSpec((TOP_K * tm,), lambda i: (i,), memory_space=pltpu.SMEM),
                  pl.BlockSpec((TOP_K * tm,), lambda i: (jnp.minimum(i + 1, last),),
                               memory_space=pltpu.SMEM),
                  tok, pl.BlockSpec((tm, TOP_K), lambda i: (i, 0)),
                  pl.BlockSpec(memory_space=pl.ANY)]
        scratch = [pltpu.VMEM((2, TOP_K, tm // SUBLANES, SUBLANES, D), F32),
                   pltpu.SemaphoreType.DMA((2,))]
    else:
        args.append(x)
        specs.append(tok)
    args += [p_all, g, w_gate, w_ple]
    specs += [pl.BlockSpec((None, tm, DP), lambda i: (layer, i, 0)), const(1, D),
              const(D, D), const(DP, D)]
    if g_final is not None:
        args.append(g_final)
        specs.append(const(1, D))
    kern = functools.partial(_ple_kernel, moe=moe is not None, final=g_final is not None, tm=tm)
    return pl.pallas_call(
        kern,
        grid=(n // tm,),
        in_specs=specs,
        out_specs=tok,
        out_shape=jax.ShapeDtypeStruct((n, D), F32),
        scratch_shapes=scratch,
        compiler_params=_cparams(("arbitrary",)),
        name="ple_moe" if moe is not None else "ple",
    )(*args)


def kernel(x_prompt, x_sample, state_conv, state_ret, p_prompt, p_sample, norm_mix_g, w_in,
           conv_w, w_conv_out, w_ret_out, w_o, norm_ffn_g, w_dense_gu, w_dense_down, w_router,
           w_exp_gu, w_exp_down, norm_ple_g, w_ple, w_ple_gate, norm_final_g):
    B, T, D = x_prompt.shape
    NP = B * T
    NS = x_sample.shape[0]
    depth = w_in.shape[0]
    _, _, H, DK, DV = state_ret.shape
    DC = conv_w.shape[-1]
    DR = H * DK
    E = w_router.shape[-1]
    assert x_sample.shape[1] == 1 and CONV_W - 1 == state_conv.shape[2]
    assert T % MIX_CHUNK == 0 and NS % DEC_BB == 0 and NP % ROUTE_TILE == 0

    half = DK // 2
    inv = ROPE_BASE ** (-jnp.arange(half, dtype=F32) / half)
    inv_dup = jnp.concatenate([inv, inv]).reshape(1, DK)
    sgn = jnp.concatenate([-jnp.ones((half,), F32), jnp.ones((half,), F32)]).reshape(1, DK)
    pos = jnp.arange(T, dtype=F32).reshape(T, 1)
    cos_p, sin_p = _rope_tables(pos, inv_dup, sgn)
    inv_t = jnp.broadcast_to(inv_dup.reshape(DK, 1), (DK, NS))

    n_tiles_max = (TOP_K * (NP + NS)) // MOE_TM + E

    row = lambda a: a.reshape(1, -1)
    xp = x_prompt
    xs = x_sample.reshape(NS, D)
    pp = p_prompt.reshape(depth, NP, -1)
    ps = p_sample.reshape(depth, NS, -1)
    conv_p, ret_p, conv_s = [], [], []
    ret_s = jnp.zeros(state_ret.shape, F32)
    w_in_b = w_in.astype(BF16)
    for i in range(depth):
        w_co_b = w_conv_out[i].astype(BF16)
        w_ro_b = w_ret_out[i].astype(BF16)
        w_o_b = w_o[i].astype(BF16)
        g_mix = row(norm_mix_g[i])

        xp, cst, rst = _mixer_prompt(xp, cos_p, sin_p, g_mix, w_in_b, i, conv_w[i], w_co_b,
                                     w_ro_b, w_o_b, H=H, DK=DK, DV=DV)
        conv_p.append(cst[:, 8 - (CONV_W - 1):, :])
        ret_p.append(rst)

        n_main = 3 * DC + 4 * DR
        w_qkt = w_in_b[i, :, 3 * DC:3 * DC + 2 * DR].T
        zm, zg, qkt = _dec_inproj(xs, g_mix, w_in_b, i, w_qkt, inv_t, n_main, H=H, DK=DK)
        qkt3 = qkt.reshape(2 * DR, NS // DEC_BB, DEC_BB).transpose(1, 0, 2)
        ain, rin, ncv, ret_s = _dec_state(zm, qkt3, state_conv[i].reshape(NS, -1), conv_w[i],
                                          state_ret, ret_s, i, H=H, DK=DK, DV=DV)
        xs = _dec_outproj(xs, zg, ain, rin, w_co_b, w_ro_b, w_o_b)
        conv_s.append(ncv.reshape(NS, CONV_W - 1, DC))

        g_ffn = row(norm_ffn_g[i])
        xp2 = xp.reshape(NP, D)
        if i % 2 == 0:
            w_gu_b = w_dense_gu[i // 2].astype(BF16)
            w_dn_b = w_dense_down[i // 2].astype(BF16)
            xp2 = _ffn_dense(xp2, g_ffn, w_gu_b, w_dn_b)
            xs = _ffn_dense(xs, g_ffn, w_gu_b, w_dn_b)
            moe_p = moe_s = None
        else:
            w_rt = w_router[i // 2].T
            hn_p, route_p, cnt_p = _router(xp2, g_ffn, w_rt, ROUTE_TILE)
            hn_s, route_s, cnt_s = _router(xs, g_ffn, w_rt, NS)
            counts = jnp.concatenate([cnt_p[:, :, 0], cnt_s[:, :, 0]], axis=0)
            base, tile_e, valid, n_used = _route_plan(counts, MOE_TM, n_tiles_max)
            ntp = NP // ROUTE_TILE
            slots_p = _global_slots(route_p, base[:ntp], ROUTE_TILE)
            slots_s = _global_slots(route_s, base[ntp:], NS)
            sorted_x = jnp.zeros((n_tiles_max * MOE_TM, D), F32)
            sorted_x = _dispatch(slots_p, hn_p, sorted_x)
            sorted_x = _dispatch(slots_s, hn_s, sorted_x)
            sorted_y = _experts(tile_e, valid, n_used, sorted_x, w_exp_gu, w_exp_down, i // 2)
            moe_p = (slots_p, route_p[4:6].T, sorted_y)
            moe_s = (slots_s, route_s[4:6].T, sorted_y)

        g_fin = row(norm_final_g) if i == depth - 1 else None
        w_pg_b = w_ple_gate[i].astype(BF16)
        w_pl_b = w_ple[i].astype(BF16)
        xp = _ple(xp2, moe_p, pp, i, row(norm_ple_g[i]), w_pg_b, w_pl_b, g_fin).reshape(B, T, D)
        xs = _ple(xs, moe_s, ps, i, row(norm_ple_g[i]), w_pg_b, w_pl_b, g_fin)

    return (xp, xs.reshape(NS, 1, D), jnp.stack(conv_p), jnp.stack(ret_p),
            jnp.stack(conv_s), ret_s)
```

```python
import functools
import math

import jax
import jax.numpy as jnp
from jax import lax
from jax.experimental import pallas as pl
from jax.experimental.pallas import tpu as pltpu

F32 = jnp.float32
BF16 = jnp.bfloat16

CONV_W = 3
TOP_K = 2
ROPE_BASE = 10000.0
EPS = 1e-6
PAST_LEN = 16384

V7X_VMEM_BYTES = 64 * 1024 * 1024
VMEM_LIMIT = V7X_VMEM_BYTES - 8 * 1024 * 1024

MIX_CHUNK = 512
RET_SUB = 256
FFN_TM = 1024
FFN_FC = 512
ROUTE_TILE = 2048
MOE_TM = 1024
MOE_ROWS = 256
MOE_FC = 512
MOE_TD = 1024
PLE_TM = 512
DEC_BB = 16


def _cparams(sem):
    return pltpu.CompilerParams(dimension_semantics=sem, vmem_limit_bytes=VMEM_LIMIT)


def _dot(a, b):
    return jnp.dot(a, b, preferred_element_type=F32)


def _dot_nt(a, b, precision=None):
    return lax.dot_general(a, b, (((1,), (1,)), ((), ())), precision=precision,
                           preferred_element_type=F32)


def _dot_tn(a, b):
    return lax.dot_general(a, b, (((0,), (0,)), ((), ())), preferred_element_type=F32)


def _rms(x, g):
    return x * lax.rsqrt(jnp.mean(x * x, axis=-1, keepdims=True) + EPS) * g


def _sigmoid(x):
    return 1.0 / (1.0 + jnp.exp(-x))


def _silu(x):
    return x * _sigmoid(x)


def _log_gammas(n_heads):
    return tuple(math.log(1.0 - 2.0 ** (-5.0 - h)) for h in range(n_heads))


def _rope_table_kernel(pos_ref, inv_ref, sgn_ref, cos_ref, sin_ref):
    ang = pos_ref[...] * inv_ref[...]
    cos_ref[...] = jnp.cos(ang)
    sin_ref[...] = jnp.sin(ang) * sgn_ref[...]


def _rope_tables(pos, inv_dup, sgn):
    n = pos.shape[0]
    dk = inv_dup.shape[1]
    return pl.pallas_call(
        _rope_table_kernel,
        out_shape=(jax.ShapeDtypeStruct((n, dk), F32), jax.ShapeDtypeStruct((n, dk), F32)),
        name="rope_tables",
    )(pos, inv_dup, sgn)


def _mixer_prompt_kernel(x_ref, cos_ref, sin_ref, g_ref, win_ref, cw_ref, wco_ref, wro_ref,
                         wo_ref, out_ref, cst_ref, rst_ref, tail_ref, state_ref, decay_ref,
                         *, LT, L, H, DK, DV, DC, D, log_gamma):
    c = pl.program_id(1)
    DR = H * DK

    @pl.when(c == 0)
    def _init():
        tail_ref[...] = jnp.zeros_like(tail_ref)
        state_ref[...] = jnp.zeros_like(state_ref)
        diff = (lax.broadcasted_iota(jnp.int32, (L, L), 0)
                - lax.broadcasted_iota(jnp.int32, (L, L), 1)).astype(F32)
        for h in range(H):
            decay_ref[h] = jnp.where(diff >= 0.0,
                                     jnp.exp(jnp.maximum(diff, 0.0) * log_gamma[h]), 0.0)

    x = x_ref[...]
    xn = _rms(x, g_ref[...]).astype(BF16)

    zc = _dot(xn, win_ref[:, 0:3 * DC])
    cb, u = zc[:, 0:DC], zc[:, DC:2 * DC] * zc[:, 2 * DC:3 * DC]
    tail = tail_ref[...]
    p1, p2 = tail[7:8, :], tail[6:7, :]
    row = lax.broadcasted_iota(jnp.int32, (LT, DC), 0)
    u1 = jnp.where(row == 0, p1, pltpu.roll(u, 1, axis=0))
    u2 = jnp.where(row == 0, p2, jnp.where(row == 1, p1, pltpu.roll(u, 2, axis=0)))
    cw = cw_ref[...]
    conv = cw[0:1, :] * u2 + cw[1:2, :] * u1 + cw[2:3, :] * u
    tail_ref[...] = u[LT - 8:LT, :]
    a = _dot((cb * conv).astype(BF16), wco_ref[...])

    zr = _dot(xn, win_ref[:, 3 * DC:3 * DC + 4 * DR])
    ridx_k = lax.broadcasted_iota(jnp.int32, (L, DK), 0).astype(F32)
    ridx_v = lax.broadcasted_iota(jnp.int32, (L, DV), 0).astype(F32)
    subs = []
    for s in range(LT // L):
        rows = slice(s * L, (s + 1) * L)
        cos, sin = cos_ref[rows, :], sin_ref[rows, :]
        heads = []
        for h in range(H):
            lg = log_gamma[h]
            qh = zr[rows, h * DK:(h + 1) * DK]
            kh = zr[rows, DR + h * DK:DR + (h + 1) * DK]
            vh = zr[rows, 2 * DR + h * DV:2 * DR + (h + 1) * DV]
            gh = zr[rows, 3 * DR + h * DV:3 * DR + (h + 1) * DV]
            qh = qh * cos + pltpu.roll(qh, DK // 2, axis=1) * sin
            kh = (kh * cos + pltpu.roll(kh, DK // 2, axis=1) * sin) * (DK ** -0.5)
            qb, vb = qh.astype(BF16), vh.astype(BF16)
            s_prev = state_ref[h]
            scores = _dot_nt(qb, kh.astype(BF16)) * decay_ref[h]
            o = _dot(scores.astype(BF16), vb)
            o = o + _dot(qb, s_prev.astype(BF16)) * jnp.exp((ridx_v + 1.0) * lg)
            k_dec = jnp.exp((L - 1.0 - ridx_k) * lg)
            state_ref[h] = math.exp(L * lg) * s_prev + _dot_tn((kh * k_dec).astype(BF16), vb)
            mu = jnp.mean(o, axis=-1, keepdims=True)
            d = o - mu
            var = jnp.mean(d * d, axis=-1, keepdims=True)
            heads.append(_silu(gh) * (d * lax.rsqrt(var + EPS)))
        subs.append(jnp.concatenate(heads, axis=-1).astype(BF16))
    r = _dot(jnp.concatenate(subs, axis=0), wro_ref[...])

    zg = _dot(xn, win_ref[:, 3 * DC + 4 * DR:3 * DC + 4 * DR + 2 * D])
    m = _sigmoid(zg[:, 0:D]) * a + _sigmoid(zg[:, D:2 * D]) * r
    out_ref[...] = x + _dot(m.astype(BF16), wo_ref[...])

    @pl.when(c == pl.num_programs(1) - 1)
    def _final():
        cst_ref[...] = u[LT - 8:LT, :]
        rst_ref[...] = state_ref[...]


def _mixer_prompt(x, cos, sin, g, w_in_all, layer, cw, w_co, w_ro, w_o, *, H, DK, DV):
    B, T, D = x.shape
    DC = cw.shape[1]
    LT, L = MIX_CHUNK, RET_SUB
    const = lambda *shape: pl.BlockSpec(shape, lambda b, c: (0,) * len(shape))
    kern = functools.partial(_mixer_prompt_kernel, LT=LT, L=L, H=H, DK=DK, DV=DV, DC=DC, D=D,
                             log_gamma=_log_gammas(H))
    return pl.pallas_call(
        kern,
        grid=(B, T // LT),
        in_specs=[
            pl.BlockSpec((None, LT, D), lambda b, c: (b, c, 0)),
            pl.BlockSpec((LT, DK), lambda b, c: (c, 0)),
            pl.BlockSpec((LT, DK), lambda b, c: (c, 0)),
            const(1, D),
            pl.BlockSpec((None,) + w_in_all.shape[1:], lambda b, c: (layer, 0, 0)),
            const(*cw.shape), const(*w_co.shape),
            const(*w_ro.shape), const(*w_o.shape),
        ],
        out_specs=[
            pl.BlockSpec((None, LT, D), lambda b, c: (b, c, 0)),
            pl.BlockSpec((None, 8, DC), lambda b, c: (b, 0, 0)),
            pl.BlockSpec((None, H, DK, DV), lambda b, c: (b, 0, 0, 0)),
        ],
        out_shape=(jax.ShapeDtypeStruct((B, T, D), F32),
                   jax.ShapeDtypeStruct((B, 8, DC), F32),
                   jax.ShapeDtypeStruct((B, H, DK, DV), F32)),
        scratch_shapes=[pltpu.VMEM((8, DC), F32), pltpu.VMEM((H, DK, DV), F32),
                        pltpu.VMEM((H, L, L), F32)],
        compiler_params=_cparams(("arbitrary", "arbitrary")),
        name="mixer_prompt",
    )(x, cos, sin, g, w_in_all, cw, w_co, w_ro, w_o)


def _dec_inproj_kernel(x_ref, g_ref, win_ref, wqkt_ref, invt_ref, zm_ref, zg_ref, qkt_ref,
                       *, H, DK, n_main, pos0):
    xn = _rms(x_ref[...], g_ref[...]).astype(BF16)
    zm_ref[...] = _dot(xn, win_ref[:, 0:n_main])
    zg_ref[...] = _dot(xn, win_ref[:, n_main:win_ref.shape[1]])
    qkt = _dot_nt(wqkt_ref[...], xn)
    ang = pos0 * invt_ref[...]
    cos, sin = jnp.cos(ang), jnp.sin(ang)
    half = DK // 2
    for hh in range(2 * H):
        blk = qkt[hh * DK:(hh + 1) * DK, :]
        x1, x2 = blk[0:half, :], blk[half:DK, :]
        scale = 1.0 if hh < H else DK ** -0.5
        qkt_ref[hh * DK:hh * DK + half, :] = (x1 * cos[0:half] - x2 * sin[0:half]) * scale
        qkt_ref[hh * DK + half:(hh + 1) * DK, :] = (x2 * cos[half:DK] + x1 * sin[half:DK]) * scale


def _dec_inproj(x, g, w_in_all, layer, w_qkt, inv_t, n_main, *, H, DK):
    n, D = x.shape
    n_in = w_in_all.shape[2]
    kern = functools.partial(_dec_inproj_kernel, H=H, DK=DK, n_main=n_main,
                             pos0=float(PAST_LEN))
    whole = lambda a: pl.BlockSpec(a.shape, lambda i: (0,) * a.ndim)
    out_shape = (jax.ShapeDtypeStruct((n, n_main), F32),
                 jax.ShapeDtypeStruct((n, n_in - n_main), F32),
                 jax.ShapeDtypeStruct((w_qkt.shape[0], n), F32))
    return pl.pallas_call(
        kern,
        grid=(1,),
        in_specs=[whole(x), whole(g), pl.BlockSpec((None, D, n_in), lambda i: (layer, 0, 0)),
                  whole(w_qkt), whole(inv_t)],
        out_specs=[whole(s) for s in out_shape],
        out_shape=out_shape,
        compiler_params=_cparams(("arbitrary",)),
        name="dec_inproj",
    )(x, g, w_in_all, w_qkt, inv_t)


def _dec_state_kernel(zm_ref, qkt_ref, conv_ref, cw_ref, s_ref, *rest, BB, H, DK, DV, DC,
                      log_gamma):
    ain_ref, rin_ref, nconv_ref, ns_ref, o_scr = rest[-5:]
    DR = H * DK
    zm = zm_ref[...]
    cb, u = zm[:, 0:DC], zm[:, DC:2 * DC] * zm[:, 2 * DC:3 * DC]
    buf = conv_ref[...]
    b0, b1 = buf[:, 0:DC], buf[:, DC:2 * DC]
    cw = cw_ref[...]
    conv = cw[0:1, :] * b0 + cw[1:2, :] * b1 + cw[2:3, :] * u
    nconv_ref[:, 0:DC] = b1
    nconv_ref[:, DC:2 * DC] = u
    ain_ref[...] = cb * conv

    qkt = qkt_ref[...]
    v = zm[:, 3 * DC + 2 * DR:3 * DC + 3 * DR]
    g = zm[:, 3 * DC + 3 * DR:3 * DC + 4 * DR]
    for j in range(BB):
        for h in range(H):
            gamma = math.exp(log_gamma[h])
            qc = qkt[h * DK:(h + 1) * DK, j:j + 1]
            kc = qkt[DR + h * DK:DR + (h + 1) * DK, j:j + 1]
            s_prev = s_ref[j, h]
            vrow = v[j:j + 1, h * DV:(h + 1) * DV]
            qk = jnp.sum(qc * kc, axis=0, keepdims=True)
            inter = jnp.sum(qc * s_prev, axis=0, keepdims=True) * gamma
            o_scr[j:j + 1, h * DV:(h + 1) * DV] = qk * vrow + inter
            ns_ref[j, h] = gamma * s_prev + kc * vrow
    o = o_scr[...]
    for h in range(H):
        oh = o[:, h * DV:(h + 1) * DV]
        mu = jnp.mean(oh, axis=-1, keepdims=True)
        d = oh - mu
        var = jnp.mean(d * d, axis=-1, keepdims=True)
        rin_ref[:, h * DV:(h + 1) * DV] = _silu(g[:, h * DV:(h + 1) * DV]) * (d * lax.rsqrt(var + EPS))


def _dec_state(zm, qkt3, conv2d, cw, state_all, new_states, layer, *, H, DK, DV):
    n = zm.shape[0]
    depth = state_all.shape[0]
    DC = cw.shape[1]
    BB = DEC_BB
    kern = functools.partial(_dec_state_kernel, BB=BB, H=H, DK=DK, DV=DV, DC=DC,
                             log_gamma=_log_gammas(H))
    args = [zm, qkt3, conv2d, cw, state_all, new_states]
    in_specs = [
        pl.BlockSpec((BB, zm.shape[1]), lambda b: (b, 0)),
        pl.BlockSpec((None, qkt3.shape[1], BB), lambda b: (b, 0, 0)),
        pl.BlockSpec((BB, 2 * DC), lambda b: (b, 0)),
        pl.BlockSpec(cw.shape, lambda b: (0, 0)),
        pl.BlockSpec((None, BB, H, DK, DV), lambda b: (layer, b, 0, 0, 0)),
        pl.BlockSpec(memory_space=pl.ANY),
    ]
    aliases = {5: 3}
    return pl.pallas_call(
        kern,
        grid=(n // BB,),
        in_specs=in_specs,
        out_specs=[
            pl.BlockSpec((BB, DC), lambda b: (b, 0)),
            pl.BlockSpec((BB, H * DV), lambda b: (b, 0)),
            pl.BlockSpec((BB, 2 * DC), lambda b: (b, 0)),
            pl.BlockSpec((None, BB, H, DK, DV), lambda b: (layer, b, 0, 0, 0)),
        ],
        out_shape=(jax.ShapeDtypeStruct((n, DC), F32),
                   jax.ShapeDtypeStruct((n, H * DV), F32),
                   jax.ShapeDtypeStruct((n, 2 * DC), F32),
                   jax.ShapeDtypeStruct((depth, n, H, DK, DV), F32)),
        scratch_shapes=[pltpu.VMEM((BB, H * DV), F32)],
        input_output_aliases=aliases,
        compiler_params=_cparams(("arbitrary",)),
        name="dec_state",
    )(*args)


def _dec_outproj_kernel(x_ref, zg_ref, ain_ref, rin_ref, wco_ref, wro_ref, wo_ref, out_ref, *, D):
    a = _dot(ain_ref[...].astype(BF16), wco_ref[...])
    r = _dot(rin_ref[...].astype(BF16), wro_ref[...])
    zg = zg_ref[...]
    m = _sigmoid(zg[:, 0:D]) * a + _sigmoid(zg[:, D:2 * D]) * r
    out_ref[...] = x_ref[...] + _dot(m.astype(BF16), wo_ref[...])


def _dec_outproj(x, zg, ain, rin, w_co, w_ro, w_o):
    n, D = x.shape
    return pl.pallas_call(
        functools.partial(_dec_outproj_kernel, D=D),
        out_shape=jax.ShapeDtypeStruct((n, D), F32),
        compiler_params=pltpu.CompilerParams(vmem_limit_bytes=VMEM_LIMIT),
        name="dec_outproj",
    )(x, zg, ain, rin, w_co, w_ro, w_o)


def _ffn_dense_kernel(x_ref, g_ref, wg_ref, wu_ref, wd_ref, out_ref, hn_ref):
    @pl.when(pl.program_id(1) == 0)
    def _init():
        x = x_ref[...]
        hn_ref[...] = _rms(x, g_ref[...]).astype(BF16)
        out_ref[...] = x

    hn = hn_ref[...]
    act = (_silu(_dot(hn, wg_ref[...])) * _dot(hn, wu_ref[...])).astype(BF16)
    out_ref[...] += _dot(act, wd_ref[...])


def _ffn_dense(x, g, w_gu, w_down):
    n, D = x.shape
    FF = w_down.shape[0]
    tm = min(FFN_TM, n)
    fc = FFN_FC
    nf = FF // fc
    return pl.pallas_call(
        _ffn_dense_kernel,
        grid=(n // tm, nf),
        in_specs=[
            pl.BlockSpec((tm, D), lambda i, j: (i, 0)),
            pl.BlockSpec((1, D), lambda i, j: (0, 0)),
            pl.BlockSpec((D, fc), lambda i, j: (0, j)),
            pl.BlockSpec((D, fc), lambda i, j: (0, nf + j)),
            pl.BlockSpec((fc, D), lambda i, j: (j, 0)),
        ],
        out_specs=pl.BlockSpec((tm, D), lambda i, j: (i, 0)),
        out_shape=jax.ShapeDtypeStruct((n, D), F32),
        scratch_shapes=[pltpu.VMEM((tm, D), BF16)],
        compiler_params=_cparams(("arbitrary", "arbitrary")),
        name="ffn_dense",
    )(x, g, w_gu, w_gu, w_down)


def _router_kernel(x_ref, g_ref, wrt_ref, hn_ref, route_ref, cnt_ref, *, T, E, CH):
    hn = _rms(x_ref[...], g_ref[...])
    hn_ref[...] = hn
    logits = _dot_nt(wrt_ref[...], hn, precision=lax.Precision.HIGHEST)
    eidx = lax.broadcasted_iota(jnp.int32, (E, T), 0)
    m1 = jnp.max(logits, axis=0, keepdims=True)
    i1 = jnp.min(jnp.where(logits == m1, eidx, E), axis=0, keepdims=True)
    sel1 = eidx == i1
    rest = jnp.where(sel1, -jnp.inf, logits)
    m2 = jnp.max(rest, axis=0, keepdims=True)
    i2 = jnp.min(jnp.where(rest == m2, eidx, E), axis=0, keepdims=True)
    sel2 = eidx == i2
    e2 = jnp.exp(m2 - m1)
    w1 = 1.0 / (1.0 + e2)
    f1 = jnp.where(sel1, 1.0, 0.0)
    f2 = jnp.where(sel2, 1.0, 0.0)
    sel = f1 + f2
    selb = sel.astype(BF16)
    route_ref[0:1, :] = i1.astype(F32)
    route_ref[1:2, :] = i2.astype(F32)
    route_ref[4:5, :] = w1
    route_ref[5:6, :] = e2 * w1
    route_ref[6:8, :] = jnp.zeros((2, T), F32)
    for cidx in range(T // CH):
        cols = slice(cidx * CH, (cidx + 1) * CH)
        src = lax.broadcasted_iota(jnp.int32, (T, CH), 0)
        dst = lax.broadcasted_iota(jnp.int32, (T, CH), 1) + cidx * CH
        tri = jnp.where(src < dst, 1.0, 0.0).astype(BF16)
        pos = _dot(selb, tri)
        route_ref[2:3, cols] = jnp.sum(pos * f1[:, cols], axis=0, keepdims=True)
        route_ref[3:4, cols] = jnp.sum(pos * f2[:, cols], axis=0, keepdims=True)
    cnt = jnp.sum(sel, axis=1, keepdims=True)
    cnt_ref[...] = jnp.broadcast_to(cnt, cnt_ref.shape).astype(jnp.int32)


def _router(x, g, w_router_t, T):
    n, D = x.shape
    E = w_router_t.shape[0]
    nt = n // T
    kern = functools.partial(_router_kernel, T=T, E=E, CH=min(256, T))
    return pl.pallas_call(
        kern,
        grid=(nt,),
        in_specs=[
            pl.BlockSpec((T, D), lambda i: (i, 0)),
            pl.BlockSpec((1, D), lambda i: (0, 0)),
            pl.BlockSpec((E, D), lambda i: (0, 0)),
        ],
        out_specs=[
            pl.BlockSpec((T, D), lambda i: (i, 0)),
            pl.BlockSpec((8, T), lambda i: (0, i)),
            pl.BlockSpec((None, E, 128), lambda i: (i, 0, 0)),
        ],
        out_shape=(jax.ShapeDtypeStruct((n, D), F32),
                   jax.ShapeDtypeStruct((8, n), F32),
                   jax.ShapeDtypeStruct((nt, E, 128), jnp.int32)),
        compiler_params=_cparams(("arbitrary",)),
        name="moe_router",
    )(x, g, w_router_t)


def _route_plan(counts, tm, n_tiles_max):
    E = counts.shape[1]
    tot = jnp.sum(counts, axis=0)
    tiles_e = (tot + (tm - 1)) // tm
    tile_end = jnp.cumsum(tiles_e)
    tile_start = tile_end - tiles_e
    base = (tile_start * tm)[None, :] + jnp.cumsum(counts, axis=0) - counts
    r = jnp.arange(n_tiles_max, dtype=jnp.int32)
    tile_e = jnp.minimum(jnp.sum(r[:, None] >= tile_end[None, :], axis=1), E - 1).astype(jnp.int32)
    n_used = tile_end[E - 1:E].astype(jnp.int32)
    valid = jnp.clip(tot[tile_e] - (r - tile_start[tile_e]) * tm, 0, tm)
    valid = jnp.where(r < n_used[0], valid, 0).astype(jnp.int32)
    return base.astype(jnp.int32), tile_e, valid, n_used


def _global_slots(route, base, T):
    nt, E = base.shape
    idx = route[0:TOP_K].astype(jnp.int32).reshape(TOP_K, nt, T)
    slot = route[TOP_K:2 * TOP_K].astype(jnp.int32).reshape(TOP_K, nt, T)
    onehot = idx[..., None] == jnp.arange(E, dtype=jnp.int32)
    start = jnp.sum(jnp.where(onehot, base[None, :, None, :], 0), axis=-1)
    return (start + slot).reshape(TOP_K, nt * T).T.reshape(-1)


SUBLANES = 8


def _for_each_row(n_rows, slots_ref, fn):
    def body(q, carry):
        first = q * (SUBLANES * TOP_K)
        for u in range(SUBLANES):
            fn(q, u, [slots_ref[first + (u * TOP_K + k)] for k in range(TOP_K)])
        return carry
    lax.fori_loop(0, n_rows // SUBLANES, body, 0)


def _dispatch_kernel(g_ref, hn_ref, xs_in_ref, xs_ref, sem, *, TD):
    del xs_in_ref

    def send(q, u, slots):
        for k, dst in enumerate(slots):
            pltpu.make_async_copy(hn_ref.at[q, pl.ds(u, 1)], xs_ref.at[pl.ds(dst, 1)],
                                  sem).start(priority=k % 2)
    _for_each_row(TD, g_ref, send)
    for k in range(TOP_K):
        pltpu.make_async_copy(hn_ref, hn_ref, sem).wait()


def _dispatch(g, hn, xs):
    n, D = hn.shape
    TD = min(MOE_TD, n)
    return pl.pallas_call(
        functools.partial(_dispatch_kernel, TD=TD),
        grid=(n // TD,),
        in_specs=[
            pl.BlockSpec((TOP_K * TD,), lambda i: (i,), memory_space=pltpu.SMEM),
            pl.BlockSpec((TD // SUBLANES, SUBLANES, D), lambda i: (i, 0, 0)),
            pl.BlockSpec(memory_space=pl.ANY),
        ],
        out_specs=pl.BlockSpec(memory_space=pl.ANY),
        out_shape=jax.ShapeDtypeStruct(xs.shape, xs.dtype),
        scratch_shapes=[pltpu.SemaphoreType.DMA(())],
        input_output_aliases={2: 0},
        compiler_params=_cparams(("arbitrary",)),
        name="moe_dispatch",
    )(g, hn.reshape(n // SUBLANES, SUBLANES, D), xs)


def _experts_kernel(te_ref, valid_ref, nused_ref, x_ref, wg_ref, wu_ref, wd_ref, y_ref, *, TM, R):
    r, j = pl.program_id(0), pl.program_id(1)
    valid = jnp.where(r < nused_ref[0], valid_ref[r], 0)
    nch = lax.shift_right_logical(valid + (R - 1), R.bit_length() - 1)

    @pl.when((nch == 0) & (j == 0))
    def _idle():
        y_ref[...] = jnp.zeros_like(y_ref)

    def variant(rows):
        def part():
            xs = x_ref[0:rows, :].astype(BF16)
            act = (_silu(_dot(xs, wg_ref[...].astype(BF16)))
                   * _dot(xs, wu_ref[...].astype(BF16))).astype(BF16)
            return _dot(act, wd_ref[...].astype(BF16))

        @pl.when((nch == rows // R) & (j == 0))
        def _first():
            y_ref[0:rows, :] = part()
            if rows < TM:
                y_ref[rows:TM, :] = jnp.zeros((TM - rows, y_ref.shape[1]), F32)

        @pl.when((nch == rows // R) & (j > 0))
        def _rest():
            y_ref[0:rows, :] += part()

    for n in range(1, TM // R + 1):
        variant(n * R)


def _experts(tile_e, valid, n_used, xs, w_gu_all, w_down_all, layer):
    rows, D = xs.shape
    FF = w_down_all.shape[2]
    TM, R, fc = MOE_TM, MOE_ROWS, MOE_FC
    nf = FF // fc
    nt = rows // TM

    def tile_of(r, nu):
        return jnp.maximum(jnp.minimum(r, nu[0] - 1), 0)

    def chunk_of(r, j, nu):
        return jnp.where(r < nu[0], j, nf - 1)

    grid_spec = pltpu.PrefetchScalarGridSpec(
        num_scalar_prefetch=3,
        grid=(nt, nf),
        in_specs=[
            pl.BlockSpec((TM, D), lambda r, j, te, vl, nu: (tile_of(r, nu), 0)),
            pl.BlockSpec((None, None, D, fc),
                         lambda r, j, te, vl, nu: (layer, te[tile_of(r, nu)], 0, chunk_of(r, j, nu))),
            pl.BlockSpec((None, None, D, fc),
                         lambda r, j, te, vl, nu: (layer, te[tile_of(r, nu)], 0,
                                                   nf + chunk_of(r, j, nu))),
            pl.BlockSpec((None, None, fc, D),
                         lambda r, j, te, vl, nu: (layer, te[tile_of(r, nu)], chunk_of(r, j, nu), 0)),
        ],
        out_specs=pl.BlockSpec((TM, D), lambda r, j, te, vl, nu: (r, 0)),
    )
    return pl.pallas_call(
        functools.partial(_experts_kernel, TM=TM, R=R),
        grid_spec=grid_spec,
        out_shape=jax.ShapeDtypeStruct((rows, D), F32),
        compiler_params=_cparams(("arbitrary", "arbitrary")),
        name="moe_experts",
    )(tile_e, valid, n_used, xs, w_gu_all, w_gu_all, w_down_all)


def _ple_kernel(*refs, moe, final, tm):
    refs = list(refs)
    if moe:
        g_ref, gnext_ref, x_ref, w_ref, ys_ref = refs[0:5]
        refs = refs[5:]
        buf_ref, sem = refs[-2:]
        refs = refs[:-2]
    else:
        x_ref = refs.pop(0)
    p_ref, gn_ref, wgate_ref, wple_ref = refs[0:4]
    gfin_ref = refs[4] if final else None
    out_ref = refs[-1]

    x = x_ref[...]
    if moe:
        i = pl.program_id(0)
        slot = lax.rem(i, 2)

        def fetch(slots_ref, s):
            def get(q, u, slots):
                for k, src in enumerate(slots):
                    pltpu.make_async_copy(ys_ref.at[pl.ds(src, 1)],
                                          buf_ref.at[s, k, q, pl.ds(u, 1)],
                                          sem.at[s]).start(priority=k % 2)
            _for_each_row(tm, slots_ref, get)

        @pl.when(i == 0)
        def _first():
            fetch(g_ref, 0)

        @pl.when(i + 1 < pl.num_programs(0))
        def _ahead():
            fetch(gnext_ref, 1 - slot)

        for k in range(TOP_K):
            pltpu.make_async_copy(buf_ref.at[slot, k], buf_ref.at[slot, k], sem.at[slot]).wait()
        w = w_ref[...]
        for k in range(TOP_K):
            x = x + w[:, k:k + 1] * buf_ref[slot, k].reshape(x.shape)
    gate = _sigmoid(_dot(_rms(x, gn_ref[...]).astype(BF16), wgate_ref[...]))
    y = x + gate * _dot(p_ref[...].astype(BF16), wple_ref[...])
    if final:
        y = _rms(y, gfin_ref[...])
    out_ref[...] = y


def _ple(x, moe, p_all, layer, g, w_gate, w_ple, g_final):
    n, D = x.shape
    DP = p_all.shape[-1]
    tm = min(PLE_TM, n)
    tok = pl.BlockSpec((tm, D), lambda i: (i, 0))
    const = lambda *shape: pl.BlockSpec(shape, lambda i: (0,) * len(shape))
    args, specs, scratch = [], [], []
    if moe is not None:
        slots, weights, ys = moe
        last = n // tm - 1
        args += [slots, slots, x, weights, ys]
        specs += [pl.BlockSpec((TOP_K * tm,), lambda i: (i,), memory_space=pltpu.SMEM),
                  pl.BlockSpec((TOP_K * tm,), lambda i: (jnp.minimum(i + 1, last),),
                               memory_space=pltpu.SMEM),
                  tok, pl.BlockSpec((tm, TOP_K), lambda i: (i, 0)),
                  pl.BlockSpec(memory_space=pl.ANY)]
        scratch = [pltpu.VMEM((2, TOP_K, tm // SUBLANES, SUBLANES, D), F32),
                   pltpu.SemaphoreType.DMA((2,))]
    else:
        args.append(x)
        specs.append(tok)
    args += [p_all, g, w_gate, w_ple]
    specs += [pl.BlockSpec((None, tm, DP), lambda i: (layer, i, 0)), const(1, D),
              const(D, D), const(DP, D)]
    if g_final is not None:
        args.append(g_final)
        specs.append(const(1, D))
    kern = functools.partial(_ple_kernel, moe=moe is not None, final=g_final is not None, tm=tm)
    return pl.pallas_call(
        kern,
        grid=(n // tm,),
        in_specs=specs,
        out_specs=tok,
        out_shape=jax.ShapeDtypeStruct((n, D), F32),
        scratch_shapes=scratch,
        compiler_params=_cparams(("arbitrary",)),
        name="ple_moe" if moe is not None else "ple",
    )(*args)


def kernel(x_prompt, x_sample, state_conv, state_ret, p_prompt, p_sample, norm_mix_g, w_in,
           conv_w, w_conv_out, w_ret_out, w_o, norm_ffn_g, w_dense_gu, w_dense_down, w_router,
           w_exp_gu, w_exp_down, norm_ple_g, w_ple, w_ple_gate, norm_final_g):
    B, T, D = x_prompt.shape
    NP = B * T
    NS = x_sample.shape[0]
    depth = w_in.shape[0]
    _, _, H, DK, DV = state_ret.shape
    DC = conv_w.shape[-1]
    DR = H * DK
    E = w_router.shape[-1]
    assert x_sample.shape[1] == 1 and CONV_W - 1 == state_conv.shape[2]
    assert T % MIX_CHUNK == 0 and NS % DEC_BB == 0 and NP % ROUTE_TILE == 0

    half = DK // 2
    inv = ROPE_BASE ** (-jnp.arange(half, dtype=F32) / half)
    inv_dup = jnp.concatenate([inv, inv]).reshape(1, DK)
    sgn = jnp.concatenate([-jnp.ones((half,), F32), jnp.ones((half,), F32)]).reshape(1, DK)
    pos = jnp.arange(T, dtype=F32).reshape(T, 1)
    cos_p, sin_p = _rope_tables(pos, inv_dup, sgn)
    inv_t = jnp.broadcast_to(inv_dup.reshape(DK, 1), (DK, NS))

    n_tiles_max = (TOP_K * (NP + NS)) // MOE_TM + E

    row = lambda a: a.reshape(1, -1)
    xp = x_prompt
    xs = x_sample.reshape(NS, D)
    pp = p_prompt.reshape(depth, NP, -1)
    ps = p_sample.reshape(depth, NS, -1)
    conv_p, ret_p, conv_s = [], [], []
    ret_s = jnp.zeros(state_ret.shape, F32)
    w_in_b = w_in.astype(BF16)
    for i in range(depth):
        w_co_b = w_conv_out[i].astype(BF16)
        w_ro_b = w_ret_out[i].astype(BF16)
        w_o_b = w_o[i].astype(BF16)
        g_mix = row(norm_mix_g[i])

        xp, cst, rst = _mixer_prompt(xp, cos_p, sin_p, g_mix, w_in_b, i, conv_w[i], w_co_b,
                                     w_ro_b, w_o_b, H=H, DK=DK, DV=DV)
        conv_p.append(cst[:, 8 - (CONV_W - 1):, :])
        ret_p.append(rst)

        n_main = 3 * DC + 4 * DR
        w_qkt = w_in_b[i, :, 3 * DC:3 * DC + 2 * DR].T
        zm, zg, qkt = _dec_inproj(xs, g_mix, w_in_b, i, w_qkt, inv_t, n_main, H=H, DK=DK)
        qkt3 = qkt.reshape(2 * DR, NS // DEC_BB, DEC_BB).transpose(1, 0, 2)
        ain, rin, ncv, ret_s = _dec_state(zm, qkt3, state_conv[i].reshape(NS, -1), conv_w[i],
                                          state_ret, ret_s, i, H=H, DK=DK, DV=DV)
        xs = _dec_outproj(xs, zg, ain, rin, w_co_b, w_ro_b, w_o_b)
        conv_s.append(ncv.reshape(NS, CONV_W - 1, DC))

        g_ffn = row(norm_ffn_g[i])
        xp2 = xp.reshape(NP, D)
        if i % 2 == 0:
            w_gu_b = w_dense_gu[i // 2].astype(BF16)
            w_dn_b = w_dense_down[i // 2].astype(BF16)
            xp2 = _ffn_dense(xp2, g_ffn, w_gu_b, w_dn_b)
            xs = _ffn_dense(xs, g_ffn, w_gu_b, w_dn_b)
            moe_p = moe_s = None
        else:
            w_rt = w_router[i // 2].T
            hn_p, route_p, cnt_p = _router(xp2, g_ffn, w_rt, ROUTE_TILE)
            hn_s, route_s, cnt_s = _router(xs, g_ffn, w_rt, NS)
            counts = jnp.concatenate([cnt_p[:, :, 0], cnt_s[:, :, 0]], axis=0)
            base, tile_e, valid, n_used = _route_plan(counts, MOE_TM, n_tiles_max)
            ntp = NP // ROUTE_TILE
            slots_p = _global_slots(route_p, base[:ntp], ROUTE_TILE)
            slots_s = _global_slots(route_s, base[ntp:], NS)
            sorted_x = jnp.zeros((n_tiles_max * MOE_TM, D), F32)
            sorted_x = _dispatch(slots_p, hn_p, sorted_x)
            sorted_x = _dispatch(slots_s, hn_s, sorted_x)
            sorted_y = _experts(tile_e, valid, n_used, sorted_x, w_exp_gu, w_exp_down, i // 2)
            moe_p = (slots_p, route_p[4:6].T, sorted_y)
            moe_s = (slots_s, route_s[4:6].T, sorted_y)

        g_fin = row(norm_final_g) if i == depth - 1 else None
        w_pg_b = w_ple_gate[i].astype(BF16)
        w_pl_b = w_ple[i].astype(BF16)
        xp = _ple(xp2, moe_p, pp, i, row(norm_ple_g[i]), w_pg_b, w_pl_b, g_fin).reshape(B, T, D)
        xs = _ple(xs, moe_s, ps, i, row(norm_ple_g[i]), w_pg_b, w_pl_b, g_fin)

    return (xp, xs.reshape(NS, 1, D), jnp.stack(conv_p), jnp.stack(ret_p),
            jnp.stack(conv_s), ret_s)
```

```python
import functools
import math

import jax
import jax.numpy as jnp
from jax import lax
from jax.experimental import pallas as pl
from jax.experimental.pallas import tpu as pltpu

F32 = jnp.float32
BF16 = jnp.bfloat16

CONV_W = 3
TOP_K = 2
ROPE_BASE = 10000.0
EPS = 1e-6
PAST_LEN = 16384

V7X_VMEM_BYTES = 64 * 1024 * 1024
VMEM_LIMIT = V7X_VMEM_BYTES - 8 * 1024 * 1024

MIX_CHUNK = 512
RET_SUB = 256
FFN_TM = 1024
FFN_FC = 512
ROUTE_TILE = 2048
MOE_TM = 1024
MOE_ROWS = 256
MOE_FC = 512
MOE_TD = 1024
PLE_TM = 512
DEC_BB = 16


def _cparams(sem):
    return pltpu.CompilerParams(dimension_semantics=sem, vmem_limit_bytes=VMEM_LIMIT)


def _dot(a, b):
    return jnp.dot(a, b, preferred_element_type=F32)


def _dot_nt(a, b, precision=None):
    return lax.dot_general(a, b, (((1,), (1,)), ((), ())), precision=precision,
                           preferred_element_type=F32)


def _dot_tn(a, b):
    return lax.dot_general(a, b, (((0,), (0,)), ((), ())), preferred_element_type=F32)


def _rms(x, g):
    return x * lax.rsqrt(jnp.mean(x * x, axis=-1, keepdims=True) + EPS) * g


def _sigmoid(x):
    return 1.0 / (1.0 + jnp.exp(-x))


def _silu(x):
    return x * _sigmoid(x)


def _log_gammas(n_heads):
    return tuple(math.log(1.0 - 2.0 ** (-5.0 - h)) for h in range(n_heads))


def _cast_kernel(w_ref, o_ref):
    o_ref[...] = w_ref[...].astype(o_ref.dtype)


def _to_bf16(w, bn):
    depth, K, N = w.shape
    spec = pl.BlockSpec((None, K, bn), lambda l, j: (l, 0, j))
    return pl.pallas_call(
        _cast_kernel,
        grid=(depth, N // bn),
        in_specs=[spec],
        out_specs=spec,
        out_shape=jax.ShapeDtypeStruct(w.shape, BF16),
        compiler_params=_cparams(("arbitrary", "arbitrary")),
        name="cast_bf16",
    )(w)


def _rope_table_kernel(pos_ref, inv_ref, sgn_ref, cos_ref, sin_ref):
    ang = pos_ref[...] * inv_ref[...]
    cos_ref[...] = jnp.cos(ang)
    sin_ref[...] = jnp.sin(ang) * sgn_ref[...]


def _rope_tables(pos, inv_dup, sgn):
    n = pos.shape[0]
    dk = inv_dup.shape[1]
    return pl.pallas_call(
        _rope_table_kernel,
        out_shape=(jax.ShapeDtypeStruct((n, dk), F32), jax.ShapeDtypeStruct((n, dk), F32)),
        name="rope_tables",
    )(pos, inv_dup, sgn)


def _mixer_prompt_kernel(x_ref, cos_ref, sin_ref, g_ref, win_ref, cw_ref, wco_ref, wro_ref,
                         wo_ref, out_ref, cst_ref, rst_ref, tail_ref, state_ref, decay_ref,
                         *, LT, L, H, DK, DV, DC, D, log_gamma):
    c = pl.program_id(1)
    DR = H * DK

    @pl.when(c == 0)
    def _init():
        tail_ref[...] = jnp.zeros_like(tail_ref)
        state_ref[...] = jnp.zeros_like(state_ref)
        diff = (lax.broadcasted_iota(jnp.int32, (L, L), 0)
                - lax.broadcasted_iota(jnp.int32, (L, L), 1)).astype(F32)
        for h in range(H):
            decay_ref[h] = jnp.where(diff >= 0.0,
                                     jnp.exp(jnp.maximum(diff, 0.0) * log_gamma[h]), 0.0)

    x = x_ref[...]
    xn = _rms(x, g_ref[...]).astype(BF16)

    zc = _dot(xn, win_ref[:, 0:3 * DC])
    cb, u = zc[:, 0:DC], zc[:, DC:2 * DC] * zc[:, 2 * DC:3 * DC]
    tail = tail_ref[...]
    p1, p2 = tail[7:8, :], tail[6:7, :]
    row = lax.broadcasted_iota(jnp.int32, (LT, DC), 0)
    u1 = jnp.where(row == 0, p1, pltpu.roll(u, 1, axis=0))
    u2 = jnp.where(row == 0, p2, jnp.where(row == 1, p1, pltpu.roll(u, 2, axis=0)))
    cw = cw_ref[...]
    conv = cw[0:1, :] * u2 + cw[1:2, :] * u1 + cw[2:3, :] * u
    tail_ref[...] = u[LT - 8:LT, :]
    a = _dot((cb * conv).astype(BF16), wco_ref[...])

    zr = _dot(xn, win_ref[:, 3 * DC:3 * DC + 4 * DR])
    ridx_k = lax.broadcasted_iota(jnp.int32, (L, DK), 0).astype(F32)
    ridx_v = lax.broadcasted_iota(jnp.int32, (L, DV), 0).astype(F32)
    subs = []
    for s in range(LT // L):
        rows = slice(s * L, (s + 1) * L)
        cos, sin = cos_ref[rows, :], sin_ref[rows, :]
        heads = []
        for h in range(H):
            lg = log_gamma[h]
            qh = zr[rows, h * DK:(h + 1) * DK]
            kh = zr[rows, DR + h * DK:DR + (h + 1) * DK]
            vh = zr[rows, 2 * DR + h * DV:2 * DR + (h + 1) * DV]
            gh = zr[rows, 3 * DR + h * DV:3 * DR + (h + 1) * DV]
            qh = qh * cos + pltpu.roll(qh, DK // 2, axis=1) * sin
            kh = (kh * cos + pltpu.roll(kh, DK // 2, axis=1) * sin) * (DK ** -0.5)
            qb, vb = qh.astype(BF16), vh.astype(BF16)
            s_prev = state_ref[h]
            scores = _dot_nt(qb, kh.astype(BF16)) * decay_ref[h]
            o = _dot(scores.astype(BF16), vb)
            o = o + _dot(qb, s_prev.astype(BF16)) * jnp.exp((ridx_v + 1.0) * lg)
            k_dec = jnp.exp((L - 1.0 - ridx_k) * lg)
            state_ref[h] = math.exp(L * lg) * s_prev + _dot_tn((kh * k_dec).astype(BF16), vb)
            mu = jnp.mean(o, axis=-1, keepdims=True)
            d = o - mu
            var = jnp.mean(d * d, axis=-1, keepdims=True)
            heads.append(_silu(gh) * (d * lax.rsqrt(var + EPS)))
        subs.append(jnp.concatenate(heads, axis=-1).astype(BF16))
    r = _dot(jnp.concatenate(subs, axis=0), wro_ref[...])

    zg = _dot(xn, win_ref[:, 3 * DC + 4 * DR:3 * DC + 4 * DR + 2 * D])
    m = _sigmoid(zg[:, 0:D]) * a + _sigmoid(zg[:, D:2 * D]) * r
    out_ref[...] = x + _dot(m.astype(BF16), wo_ref[...])

    @pl.when(c == pl.num_programs(1) - 1)
    def _final():
        cst_ref[...] = u[LT - 8:LT, :]
        rst_ref[...] = state_ref[...]


def _mixer_prompt(x, cos, sin, g, w_in_all, layer, cw, w_co, w_ro, w_o, *, H, DK, DV):
    B, T, D = x.shape
    DC = cw.shape[1]
    LT, L = MIX_CHUNK, RET_SUB
    const = lambda *shape: pl.BlockSpec(shape, lambda b, c: (0,) * len(shape))
    kern = functools.partial(_mixer_prompt_kernel, LT=LT, L=L, H=H, DK=DK, DV=DV, DC=DC, D=D,
                             log_gamma=_log_gammas(H))
    return pl.pallas_call(
        kern,
        grid=(B, T // LT),
        in_specs=[
            pl.BlockSpec((None, LT, D), lambda b, c: (b, c, 0)),
            pl.BlockSpec((LT, DK), lambda b, c: (c, 0)),
            pl.BlockSpec((LT, DK), lambda b, c: (c, 0)),
            const(1, D),
            pl.BlockSpec((None,) + w_in_all.shape[1:], lambda b, c: (layer, 0, 0)),
            const(*cw.shape), const(*w_co.shape),
            const(*w_ro.shape), const(*w_o.shape),
        ],
        out_specs=[
            pl.BlockSpec((None, LT, D), lambda b, c: (b, c, 0)),
            pl.BlockSpec((None, 8, DC), lambda b, c: (b, 0, 0)),
            pl.BlockSpec((None, H, DK, DV), lambda b, c: (b, 0, 0, 0)),
        ],
        out_shape=(jax.ShapeDtypeStruct((B, T, D), F32),
                   jax.ShapeDtypeStruct((B, 8, DC), F32),
                   jax.ShapeDtypeStruct((B, H, DK, DV), F32)),
        scratch_shapes=[pltpu.VMEM((8, DC), F32), pltpu.VMEM((H, DK, DV), F32),
                        pltpu.VMEM((H, L, L), F32)],
        compiler_params=_cparams(("arbitrary", "arbitrary")),
        name="mixer_prompt",
    )(x, cos, sin, g, w_in_all, cw, w_co, w_ro, w_o)


def _dec_inproj_kernel(x_ref, g_ref, win_ref, wqkt_ref, invt_ref, zm_ref, zg_ref, qkt_ref,
                       *, H, DK, n_main, pos0):
    xn = _rms(x_ref[...], g_ref[...]).astype(BF16)
    zm_ref[...] = _dot(xn, win_ref[:, 0:n_main])
    zg_ref[...] = _dot(xn, win_ref[:, n_main:win_ref.shape[1]])
    qkt = _dot_nt(wqkt_ref[...], xn)
    ang = pos0 * invt_ref[...]
    cos, sin = jnp.cos(ang), jnp.sin(ang)
    half = DK // 2
    for hh in range(2 * H):
        blk = qkt[hh * DK:(hh + 1) * DK, :]
        x1, x2 = blk[0:half, :], blk[half:DK, :]
        scale = 1.0 if hh < H else DK ** -0.5
        qkt_ref[hh * DK:hh * DK + half, :] = (x1 * cos[0:half] - x2 * sin[0:half]) * scale
        qkt_ref[hh * DK + half:(hh + 1) * DK, :] = (x2 * cos[half:DK] + x1 * sin[half:DK]) * scale


def _dec_inproj(x, g, w_in_all, layer, w_qkt, inv_t, n_main, *, H, DK):
    n, D = x.shape
    n_in = w_in_all.shape[2]
    kern = functools.partial(_dec_inproj_kernel, H=H, DK=DK, n_main=n_main,
                             pos0=float(PAST_LEN))
    whole = lambda a: pl.BlockSpec(a.shape, lambda i: (0,) * a.ndim)
    out_shape = (jax.ShapeDtypeStruct((n, n_main), F32),
                 jax.ShapeDtypeStruct((n, n_in - n_main), F32),
                 jax.ShapeDtypeStruct((w_qkt.shape[0], n), F32))
    return pl.pallas_call(
        kern,
        grid=(1,),
        in_specs=[whole(x), whole(g), pl.BlockSpec((None, D, n_in), lambda i: (layer, 0, 0)),
                  whole(w_qkt), whole(inv_t)],
        out_specs=[whole(s) for s in out_shape],
        out_shape=out_shape,
        compiler_params=_cparams(("arbitrary",)),
        name="dec_inproj",
    )(x, g, w_in_all, w_qkt, inv_t)


def _dec_state_kernel(zm_ref, qkt_ref, conv_ref, cw_ref, s_ref, *rest, BB, H, DK, DV, DC,
                      log_gamma):
    ain_ref, rin_ref, nconv_ref, ns_ref, o_scr = rest[-5:]
    DR = H * DK
    zm = zm_ref[...]
    cb, u = zm[:, 0:DC], zm[:, DC:2 * DC] * zm[:, 2 * DC:3 * DC]
    buf = conv_ref[...]
    b0, b1 = buf[:, 0:DC], buf[:, DC:2 * DC]
    cw = cw_ref[...]
    conv = cw[0:1, :] * b0 + cw[1:2, :] * b1 + cw[2:3, :] * u
    nconv_ref[:, 0:DC] = b1
    nconv_ref[:, DC:2 * DC] = u
    ain_ref[...] = cb * conv

    qkt = qkt_ref[...]
    v = zm[:, 3 * DC + 2 * DR:3 * DC + 3 * DR]
    g = zm[:, 3 * DC + 3 * DR:3 * DC + 4 * DR]
    for j in range(BB):
        for h in range(H):
            gamma = math.exp(log_gamma[h])
            qc = qkt[h * DK:(h + 1) * DK, j:j + 1]
            kc = qkt[DR + h * DK:DR + (h + 1) * DK, j:j + 1]
            s_prev = s_ref[j, h]
            vrow = v[j:j + 1, h * DV:(h + 1) * DV]
            qk = jnp.sum(qc * kc, axis=0, keepdims=True)
            inter = jnp.sum(qc * s_prev, axis=0, keepdims=True) * gamma
            o_scr[j:j + 1, h * DV:(h + 1) * DV] = qk * vrow + inter
            ns_ref[j, h] = gamma * s_prev + kc * vrow
    o = o_scr[...]
    for h in range(H):
        oh = o[:, h * DV:(h + 1) * DV]
        mu = jnp.mean(oh, axis=-1, keepdims=True)
        d = oh - mu
        var = jnp.mean(d * d, axis=-1, keepdims=True)
        rin_ref[:, h * DV:(h + 1) * DV] = _silu(g[:, h * DV:(h + 1) * DV]) * (d * lax.rsqrt(var + EPS))


def _dec_state(zm, qkt3, conv2d, cw, state_all, new_states, layer, *, H, DK, DV):
    n = zm.shape[0]
    depth = state_all.shape[0]
    DC = cw.shape[1]
    BB = DEC_BB
    kern = functools.partial(_dec_state_kernel, BB=BB, H=H, DK=DK, DV=DV, DC=DC,
                             log_gamma=_log_gammas(H))
    args = [zm, qkt3, conv2d, cw, state_all, new_states]
    in_specs = [
        pl.BlockSpec((BB, zm.shape[1]), lambda b: (b, 0)),
        pl.BlockSpec((None, qkt3.shape[1], BB), lambda b: (b, 0, 0)),
        pl.BlockSpec((BB, 2 * DC), lambda b: (b, 0)),
        pl.BlockSpec(cw.shape, lambda b: (0, 0)),
        pl.BlockSpec((None, BB, H, DK, DV), lambda b: (layer, b, 0, 0, 0)),
        pl.BlockSpec(memory_space=pl.ANY),
    ]
    aliases = {5: 3}
    return pl.pallas_call(
        kern,
        grid=(n // BB,),
        in_specs=in_specs,
        out_specs=[
            pl.BlockSpec((BB, DC), lambda b: (b, 0)),
            pl.BlockSpec((BB, H * DV), lambda b: (b, 0)),
            pl.BlockSpec((BB, 2 * DC), lambda b: (b, 0)),
            pl.BlockSpec((None, BB, H, DK, DV), lambda b: (layer, b, 0, 0, 0)),
        ],
        out_shape=(jax.ShapeDtypeStruct((n, DC), F32),
                   jax.ShapeDtypeStruct((n, H * DV), F32),
                   jax.ShapeDtypeStruct((n, 2 * DC), F32),
                   jax.ShapeDtypeStruct((depth, n, H, DK, DV), F32)),
        scratch_shapes=[pltpu.VMEM((BB, H * DV), F32)],
        input_output_aliases=aliases,
        compiler_params=_cparams(("arbitrary",)),
        name="dec_state",
    )(*args)


def _dec_outproj_kernel(x_ref, zg_ref, ain_ref, rin_ref, wco_ref, wro_ref, wo_ref, out_ref, *, D):
    a = _dot(ain_ref[...].astype(BF16), wco_ref[...])
    r = _dot(rin_ref[...].astype(BF16), wro_ref[...])
    zg = zg_ref[...]
    m = _sigmoid(zg[:, 0:D]) * a + _sigmoid(zg[:, D:2 * D]) * r
    out_ref[...] = x_ref[...] + _dot(m.astype(BF16), wo_ref[...])


def _dec_outproj(x, zg, ain, rin, w_co, w_ro, w_o):
    n, D = x.shape
    return pl.pallas_call(
        functools.partial(_dec_outproj_kernel, D=D),
        out_shape=jax.ShapeDtypeStruct((n, D), F32),
        compiler_params=pltpu.CompilerParams(vmem_limit_bytes=VMEM_LIMIT),
        name="dec_outproj",
    )(x, zg, ain, rin, w_co, w_ro, w_o)


def _ffn_dense_kernel(x_ref, g_ref, wg_ref, wu_ref, wd_ref, out_ref, hn_ref):
    @pl.when(pl.program_id(1) == 0)
    def _init():
        x = x_ref[...]
        hn_ref[...] = _rms(x, g_ref[...]).astype(BF16)
        out_ref[...] = x

    hn = hn_ref[...]
    act = (_silu(_dot(hn, wg_ref[...])) * _dot(hn, wu_ref[...])).astype(BF16)
    out_ref[...] += _dot(act, wd_ref[...])


def _ffn_dense(x, g, w_gu, w_down):
    n, D = x.shape
    FF = w_down.shape[0]
    tm = min(FFN_TM, n)
    fc = FFN_FC
    nf = FF // fc
    return pl.pallas_call(
        _ffn_dense_kernel,
        grid=(n // tm, nf),
        in_specs=[
            pl.BlockSpec((tm, D), lambda i, j: (i, 0)),
            pl.BlockSpec((1, D), lambda i, j: (0, 0)),
            pl.BlockSpec((D, fc), lambda i, j: (0, j)),
            pl.BlockSpec((D, fc), lambda i, j: (0, nf + j)),
            pl.BlockSpec((fc, D), lambda i, j: (j, 0)),
        ],
        out_specs=pl.BlockSpec((tm, D), lambda i, j: (i, 0)),
        out_shape=jax.ShapeDtypeStruct((n, D), F32),
        scratch_shapes=[pltpu.VMEM((tm, D), BF16)],
        compiler_params=_cparams(("arbitrary", "arbitrary")),
        name="ffn_dense",
    )(x, g, w_gu, w_gu, w_down)


def _router_kernel(x_ref, g_ref, wrt_ref, hn_ref, route_ref, cnt_ref, *, T, E, CH):
    hn = _rms(x_ref[...], g_ref[...])
    hn_ref[...] = hn
    logits = _dot_nt(wrt_ref[...], hn, precision=lax.Precision.HIGHEST)
    eidx = lax.broadcasted_iota(jnp.int32, (E, T), 0)
    m1 = jnp.max(logits, axis=0, keepdims=True)
    i1 = jnp.min(jnp.where(logits == m1, eidx, E), axis=0, keepdims=True)
    sel1 = eidx == i1
    rest = jnp.where(sel1, -jnp.inf, logits)
    m2 = jnp.max(rest, axis=0, keepdims=True)
    i2 = jnp.min(jnp.where(rest == m2, eidx, E), axis=0, keepdims=True)
    sel2 = eidx == i2
    e2 = jnp.exp(m2 - m1)
    w1 = 1.0 / (1.0 + e2)
    f1 = jnp.where(sel1, 1.0, 0.0)
    f2 = jnp.where(sel2, 1.0, 0.0)
    sel = f1 + f2
    selb = sel.astype(BF16)
    route_ref[0:1, :] = i1.astype(F32)
    route_ref[1:2, :] = i2.astype(F32)
    route_ref[4:5, :] = w1
    route_ref[5:6, :] = e2 * w1
    route_ref[6:8, :] = jnp.zeros((2, T), F32)
    for cidx in range(T // CH):
        cols = slice(cidx * CH, (cidx + 1) * CH)
        src = lax.broadcasted_iota(jnp.int32, (T, CH), 0)
        dst = lax.broadcasted_iota(jnp.int32, (T, CH), 1) + cidx * CH
        tri = jnp.where(src < dst, 1.0, 0.0).astype(BF16)
        pos = _dot(selb, tri)
        route_ref[2:3, cols] = jnp.sum(pos * f1[:, cols], axis=0, keepdims=True)
        route_ref[3:4, cols] = jnp.sum(pos * f2[:, cols], axis=0, keepdims=True)
    cnt = jnp.sum(sel, axis=1, keepdims=True)
    cnt_ref[...] = jnp.broadcast_to(cnt, cnt_ref.shape).astype(jnp.int32)


def _router(x, g, w_router_t, T):
    n, D = x.shape
    E = w_router_t.shape[0]
    nt = n // T
    kern = functools.partial(_router_kernel, T=T, E=E, CH=min(256, T))
    return pl.pallas_call(
        kern,
        grid=(nt,),
        in_specs=[
            pl.BlockSpec((T, D), lambda i: (i, 0)),
            pl.BlockSpec((1, D), lambda i: (0, 0)),
            pl.BlockSpec((E, D), lambda i: (0, 0)),
        ],
        out_specs=[
            pl.BlockSpec((T, D), lambda i: (i, 0)),
            pl.BlockSpec((8, T), lambda i: (0, i)),
            pl.BlockSpec((None, E, 128), lambda i: (i, 0, 0)),
        ],
        out_shape=(jax.ShapeDtypeStruct((n, D), F32),
                   jax.ShapeDtypeStruct((8, n), F32),
                   jax.ShapeDtypeStruct((nt, E, 128), jnp.int32)),
        compiler_params=_cparams(("arbitrary",)),
        name="moe_router",
    )(x, g, w_router_t)


def _route_plan(counts, tm, n_tiles_max):
    E = counts.shape[1]
    tot = jnp.sum(counts, axis=0)
    tiles_e = (tot + (tm - 1)) // tm
    tile_end = jnp.cumsum(tiles_e)
    tile_start = tile_end - tiles_e
    base = (tile_start * tm)[None, :] + jnp.cumsum(counts, axis=0) - counts
    r = jnp.arange(n_tiles_max, dtype=jnp.int32)
    tile_e = jnp.minimum(jnp.sum(r[:, None] >= tile_end[None, :], axis=1), E - 1).astype(jnp.int32)
    n_used = tile_end[E - 1:E].astype(jnp.int32)
    valid = jnp.clip(tot[tile_e] - (r - tile_start[tile_e]) * tm, 0, tm)
    valid = jnp.where(r < n_used[0], valid, 0).astype(jnp.int32)
    return base.astype(jnp.int32), tile_e, valid, n_used


def _global_slots(route, base, T):
    nt, E = base.shape
    idx = route[0:TOP_K].astype(jnp.int32).reshape(TOP_K, nt, T)
    slot = route[TOP_K:2 * TOP_K].astype(jnp.int32).reshape(TOP_K, nt, T)
    onehot = idx[..., None] == jnp.arange(E, dtype=jnp.int32)
    start = jnp.sum(jnp.where(onehot, base[None, :, None, :], 0), axis=-1)
    return (start + slot).reshape(TOP_K, nt * T).T.reshape(-1)


SUBLANES = 8


def _for_each_row(n_rows, slots_ref, fn):
    def body(q, carry):
        first = q * (SUBLANES * TOP_K)
        for u in range(SUBLANES):
            fn(q, u, [slots_ref[first + (u * TOP_K + k)] for k in range(TOP_K)])
        return carry
    lax.fori_loop(0, n_rows // SUBLANES, body, 0)


def _dispatch_kernel(g_ref, hn_ref, xs_in_ref, xs_ref, sem, *, TD):
    del xs_in_ref

    def send(q, u, slots):
        for k, dst in enumerate(slots):
            pltpu.make_async_copy(hn_ref.at[q, pl.ds(u, 1)], xs_ref.at[pl.ds(dst, 1)],
                                  sem).start(priority=k % 2)
    _for_each_row(TD, g_ref, send)
    for k in range(TOP_K):
        pltpu.make_async_copy(hn_ref, hn_ref, sem).wait()


def _dispatch(g, hn, xs):
    n, D = hn.shape
    TD = min(MOE_TD, n)
    return pl.pallas_call(
        functools.partial(_dispatch_kernel, TD=TD),
        grid=(n // TD,),
        in_specs=[
            pl.BlockSpec((TOP_K * TD,), lambda i: (i,), memory_space=pltpu.SMEM),
            pl.BlockSpec((TD // SUBLANES, SUBLANES, D), lambda i: (i, 0, 0)),
            pl.BlockSpec(memory_space=pl.ANY),
        ],
        out_specs=pl.BlockSpec(memory_space=pl.ANY),
        out_shape=jax.ShapeDtypeStruct(xs.shape, xs.dtype),
        scratch_shapes=[pltpu.SemaphoreType.DMA(())],
        input_output_aliases={2: 0},
        compiler_params=_cparams(("arbitrary",)),
        name="moe_dispatch",
    )(g, hn.reshape(n // SUBLANES, SUBLANES, D), xs)


def _experts_kernel(te_ref, valid_ref, nused_ref, x_ref, wg_ref, wu_ref, wd_ref, y_ref, *, TM, R):
    r, j = pl.program_id(0), pl.program_id(1)
    valid = jnp.where(r < nused_ref[0], valid_ref[r], 0)
    nch = lax.shift_right_logical(valid + (R - 1), R.bit_length() - 1)

    @pl.when((nch == 0) & (j == 0))
    def _idle():
        y_ref[...] = jnp.zeros_like(y_ref)

    def variant(rows):
        def part():
            xs = x_ref[0:rows, :].astype(BF16)
            act = (_silu(_dot(xs, wg_ref[...].astype(BF16)))
                   * _dot(xs, wu_ref[...].astype(BF16))).astype(BF16)
            return _dot(act, wd_ref[...].astype(BF16))

        @pl.when((nch == rows // R) & (j == 0))
        def _first():
            y_ref[0:rows, :] = part()
            if rows < TM:
                y_ref[rows:TM, :] = jnp.zeros((TM - rows, y_ref.shape[1]), F32)

        @pl.when((nch == rows // R) & (j > 0))
        def _rest():
            y_ref[0:rows, :] += part()

    for n in range(1, TM // R + 1):
        variant(n * R)


def _experts(tile_e, valid, n_used, xs, w_gu_all, w_down_all, layer):
    rows, D = xs.shape
    FF = w_down_all.shape[2]
    TM, R, fc = MOE_TM, MOE_ROWS, MOE_FC
    nf = FF // fc
    nt = rows // TM

    def tile_of(r, nu):
        return jnp.maximum(jnp.minimum(r, nu[0] - 1), 0)

    def chunk_of(r, j, nu):
        return jnp.where(r < nu[0], j, nf - 1)

    grid_spec = pltpu.PrefetchScalarGridSpec(
        num_scalar_prefetch=3,
        grid=(nt, nf),
        in_specs=[
            pl.BlockSpec((TM, D), lambda r, j, te, vl, nu: (tile_of(r, nu), 0)),
            pl.BlockSpec((None, None, D, fc),
                         lambda r, j, te, vl, nu: (layer, te[tile_of(r, nu)], 0, chunk_of(r, j, nu))),
            pl.BlockSpec((None, None, D, fc),
                         lambda r, j, te, vl, nu: (layer, te[tile_of(r, nu)], 0,
                                                   nf + chunk_of(r, j, nu))),
            pl.BlockSpec((None, None, fc, D),
                         lambda r, j, te, vl, nu: (layer, te[tile_of(r, nu)], chunk_of(r, j, nu), 0)),
        ],
        out_specs=pl.BlockSpec((TM, D), lambda r, j, te, vl, nu: (r, 0)),
    )
    return pl.pallas_call(
        functools.partial(_experts_kernel, TM=TM, R=R),
        grid_spec=grid_spec,
        out_shape=jax.ShapeDtypeStruct((rows, D), F32),
        compiler_params=_cparams(("arbitrary", "arbitrary")),
        name="moe_experts",
    )(tile_e, valid, n_used, xs, w_gu_all, w_gu_all, w_down_all)


def _ple_kernel(*refs, moe, final, tm):
    refs = list(refs)
    if moe:
        g_ref, gnext_ref, x_ref, w_ref, ys_ref = refs[0:5]
        refs = refs[5:]
        buf_ref, sem = refs[-2:]
        refs = refs[:-2]
    else:
        x_ref = refs.pop(0)
    p_ref, gn_ref, wgate_ref, wple_ref = refs[0:4]
    gfin_ref = refs[4] if final else None
    out_ref = refs[-1]

    if moe:
        i = pl.program_id(0)
        slot = lax.rem(i, 2)

        def fetch(slots_ref, s):
            def get(q, u, slots):
                for k, src in enumerate(slots):
                    pltpu.make_async_copy(ys_ref.at[pl.ds(src, 1)],
                                          buf_ref.at[s, k, q, pl.ds(u, 1)],
                                          sem.at[s]).start(priority=k % 2)
            _for_each_row(tm, slots_ref, get)

        @pl.when(i == 0)
        def _first():
            fetch(g_ref, 0)

        @pl.when(i + 1 < pl.num_programs(0))
        def _ahead():
            fetch(gnext_ref, 1 - slot)

        for k in range(TOP_K):
            pltpu.make_async_copy(buf_ref.at[slot, k], buf_ref.at[slot, k], sem.at[slot]).wait()

    n_parts = 2 if tm % (2 * SUBLANES) == 0 and tm >= 256 else 1
    rows_per = tm // n_parts
    for part in range(n_parts):
        rows = slice(part * rows_per, (part + 1) * rows_per)
        x = x_ref[rows, :]
        if moe:
            groups = slice(part * rows_per // SUBLANES, (part + 1) * rows_per // SUBLANES)
            w = w_ref[rows, :]
            for k in range(TOP_K):
                x = x + w[:, k:k + 1] * buf_ref[slot, k, groups].reshape(x.shape)
        gate = _sigmoid(_dot(_rms(x, gn_ref[...]).astype(BF16), wgate_ref[...]))
        y = x + gate * _dot(p_ref[rows, :].astype(BF16), wple_ref[...])
        if final:
            y = _rms(y, gfin_ref[...])
        out_ref[rows, :] = y


def _ple(x, moe, p_all, layer, g, w_gate, w_ple, g_final):
    n, D = x.shape
    DP = p_all.shape[-1]
    tm = min(PLE_TM, n)
    tok = pl.BlockSpec((tm, D), lambda i: (i, 0))
    const = lambda *shape: pl.BlockSpec(shape, lambda i: (0,) * len(shape))
    args, specs, scratch = [], [], []
    if moe is not None:
        slots, weights, ys = moe
        last = n // tm - 1
        args += [slots, slots, x, weights, ys]
        specs += [pl.BlockSpec((TOP_K * tm,), lambda i: (i,), memory_space=pltpu.SMEM),
                  pl.BlockSpec((TOP_K * tm,), lambda i: (jnp.minimum(i + 1, last),),
                               memory_space=pltpu.SMEM),
                  tok, pl.BlockSpec((tm, TOP_K), lambda i: (i, 0)),
                  pl.BlockSpec(memory_space=pl.ANY)]
        scratch = [pltpu.VMEM((2, TOP_K, tm // SUBLANES, SUBLANES, D), F32),
                   pltpu.SemaphoreType.DMA((2,))]
    else:
        args.append(x)
        specs.append(tok)
    args += [p_all, g, w_gate, w_ple]
    specs += [pl.BlockSpec((None, tm, DP), lambda i: (layer, i, 0)), const(1, D),
              const(D, D), const(DP, D)]
    if g_final is not None:
        args.append(g_final)
        specs.append(const(1, D))
    kern = functools.partial(_ple_kernel, moe=moe is not None, final=g_final is not None, tm=tm)
    return pl.pallas_call(
        kern,
        grid=(n // tm,),
        in_specs=specs,
        out_specs=tok,
        out_shape=jax.ShapeDtypeStruct((n, D), F32),
        scratch_shapes=scratch,
        compiler_params=_cparams(("arbitrary",)),
        name="ple_moe" if moe is not None else "ple",
    )(*args)


def kernel(x_prompt, x_sample, state_conv, state_ret, p_prompt, p_sample, norm_mix_g, w_in,
           conv_w, w_conv_out, w_ret_out, w_o, norm_ffn_g, w_dense_gu, w_dense_down, w_router,
           w_exp_gu, w_exp_down, norm_ple_g, w_ple, w_ple_gate, norm_final_g):
    B, T, D = x_prompt.shape
    NP = B * T
    NS = x_sample.shape[0]
    depth = w_in.shape[0]
    _, _, H, DK, DV = state_ret.shape
    DC = conv_w.shape[-1]
    DR = H * DK
    E = w_router.shape[-1]
    assert x_sample.shape[1] == 1 and CONV_W - 1 == state_conv.shape[2]
    assert T % MIX_CHUNK == 0 and NS % DEC_BB == 0 and NP % ROUTE_TILE == 0

    half = DK // 2
    inv = ROPE_BASE ** (-jnp.arange(half, dtype=F32) / half)
    inv_dup = jnp.concatenate([inv, inv]).reshape(1, DK)
    sgn = jnp.concatenate([-jnp.ones((half,), F32), jnp.ones((half,), F32)]).reshape(1, DK)
    pos = jnp.arange(T, dtype=F32).reshape(T, 1)
    cos_p, sin_p = _rope_tables(pos, inv_dup, sgn)
    inv_t = jnp.broadcast_to(inv_dup.reshape(DK, 1), (DK, NS))

    n_tiles_max = (TOP_K * (NP + NS)) // MOE_TM + E

    row = lambda a: a.reshape(1, -1)
    xp = x_prompt
    xs = x_sample.reshape(NS, D)
    pp = p_prompt.reshape(depth, NP, -1)
    ps = p_sample.reshape(depth, NS, -1)
    conv_p, ret_p, conv_s = [], [], []
    ret_s = jnp.zeros(state_ret.shape, F32)
    w_in_b = _to_bf16(w_in, 512)
    for i in range(depth):
        w_co_b = w_conv_out[i].astype(BF16)
        w_ro_b = w_ret_out[i].astype(BF16)
        w_o_b = w_o[i].astype(BF16)
        g_mix = row(norm_mix_g[i])

        xp, cst, rst = _mixer_prompt(xp, cos_p, sin_p, g_mix, w_in_b, i, conv_w[i], w_co_b,
                                     w_ro_b, w_o_b, H=H, DK=DK, DV=DV)
        conv_p.append(cst[:, 8 - (CONV_W - 1):, :])
        ret_p.append(rst)

        n_main = 3 * DC + 4 * DR
        w_qkt = w_in_b[i, :, 3 * DC:3 * DC + 2 * DR].T
        zm, zg, qkt = _dec_inproj(xs, g_mix, w_in_b, i, w_qkt, inv_t, n_main, H=H, DK=DK)
        qkt3 = qkt.reshape(2 * DR, NS // DEC_BB, DEC_BB).transpose(1, 0, 2)
        ain, rin, ncv, ret_s = _dec_state(zm, qkt3, state_conv[i].reshape(NS, -1), conv_w[i],
                                          state_ret, ret_s, i, H=H, DK=DK, DV=DV)
        xs = _dec_outproj(xs, zg, ain, rin, w_co_b, w_ro_b, w_o_b)
        conv_s.append(ncv.reshape(NS, CONV_W - 1, DC))

        g_ffn = row(norm_ffn_g[i])
        xp2 = xp.reshape(NP, D)
        if i % 2 == 0:
            w_gu_b = w_dense_gu[i // 2].astype(BF16)
            w_dn_b = w_dense_down[i // 2].astype(BF16)
            xp2 = _ffn_dense(xp2, g_ffn, w_gu_b, w_dn_b)
            xs = _ffn_dense(xs, g_ffn, w_gu_b, w_dn_b)
            moe_p = moe_s = None
        else:
            w_rt = w_router[i // 2].T
            hn_p, route_p, cnt_p = _router(xp2, g_ffn, w_rt, ROUTE_TILE)
            hn_s, route_s, cnt_s = _router(xs, g_ffn, w_rt, NS)
            counts = jnp.concatenate([cnt_p[:, :, 0], cnt_s[:, :, 0]], axis=0)
            base, tile_e, valid, n_used = _route_plan(counts, MOE_TM, n_tiles_max)
            ntp = NP // ROUTE_TILE
            slots_p = _global_slots(route_p, base[:ntp], ROUTE_TILE)
            slots_s = _global_slots(route_s, base[ntp:], NS)
            sorted_x = jnp.zeros((n_tiles_max * MOE_TM, D), F32)
            sorted_x = _dispatch(slots_p, hn_p, sorted_x)
            sorted_x = _dispatch(slots_s, hn_s, sorted_x)
            sorted_y = _experts(tile_e, valid, n_used, sorted_x, w_exp_gu, w_exp_down, i // 2)
            moe_p = (slots_p, route_p[4:6].T, sorted_y)
            moe_s = (slots_s, route_s[4:6].T, sorted_y)

        g_fin = row(norm_final_g) if i == depth - 1 else None
        w_pg_b = w_ple_gate[i].astype(BF16)
        w_pl_b = w_ple[i].astype(BF16)
        xp = _ple(xp2, moe_p, pp, i, row(norm_ple_g[i]), w_pg_b, w_pl_b, g_fin).reshape(B, T, D)
        xs = _ple(xs, moe_s, ps, i, row(norm_ple_g[i]), w_pg_b, w_pl_b, g_fin)

    return (xp, xs.reshape(NS, 1, D), jnp.stack(conv_p), jnp.stack(ret_p),
            jnp.stack(conv_s), ret_s)
```

```python
import functools
import math

import jax
import jax.numpy as jnp
from jax import lax
from jax.experimental import pallas as pl
from jax.experimental.pallas import tpu as pltpu

F32 = jnp.float32
BF16 = jnp.bfloat16

CONV_W = 3
TOP_K = 2
ROPE_BASE = 10000.0
EPS = 1e-6
PAST_LEN = 16384

V7X_VMEM_BYTES = 64 * 1024 * 1024
VMEM_LIMIT = V7X_VMEM_BYTES - 8 * 1024 * 1024

MIX_CHUNK = 1024
RET_SUB = 256
FFN_TM = 1024
FFN_FC = 512
ROUTE_TILE = 2048
MOE_TM = 2048
MOE_ROWS = 512
MOE_SUB = 1024
MOE_FC = 512
MOE_TD = 1024
PLE_TM = 512
DEC_BB = 16


def _cparams(sem):
    return pltpu.CompilerParams(dimension_semantics=sem, vmem_limit_bytes=VMEM_LIMIT)


def _dot(a, b):
    return jnp.dot(a, b, preferred_element_type=F32)


def _dot_nt(a, b, precision=None):
    return lax.dot_general(a, b, (((1,), (1,)), ((), ())), precision=precision,
                           preferred_element_type=F32)


def _dot_tn(a, b):
    return lax.dot_general(a, b, (((0,), (0,)), ((), ())), preferred_element_type=F32)


def _rms(x, g):
    return x * lax.rsqrt(jnp.mean(x * x, axis=-1, keepdims=True) + EPS) * g


def _sigmoid(x):
    return 1.0 / (1.0 + jnp.exp(-x))


def _silu(x):
    return x * _sigmoid(x)


def _log_gammas(n_heads):
    return tuple(math.log(1.0 - 2.0 ** (-5.0 - h)) for h in range(n_heads))


def _cast_kernel(w_ref, o_ref):
    o_ref[...] = w_ref[...].astype(o_ref.dtype)


def _to_bf16(w, bn):
    depth, K, N = w.shape
    spec = pl.BlockSpec((None, K, bn), lambda l, j: (l, 0, j))
    return pl.pallas_call(
        _cast_kernel,
        grid=(depth, N // bn),
        in_specs=[spec],
        out_specs=spec,
        out_shape=jax.ShapeDtypeStruct(w.shape, BF16),
        compiler_params=_cparams(("arbitrary", "arbitrary")),
        name="cast_bf16",
    )(w)


def _rope_table_kernel(pos_ref, inv_ref, sgn_ref, cos_ref, sin_ref):
    ang = pos_ref[...] * inv_ref[...]
    cos_ref[...] = jnp.cos(ang)
    sin_ref[...] = jnp.sin(ang) * sgn_ref[...]


def _rope_tables(pos, inv_dup, sgn):
    n = pos.shape[0]
    dk = inv_dup.shape[1]
    return pl.pallas_call(
        _rope_table_kernel,
        out_shape=(jax.ShapeDtypeStruct((n, dk), F32), jax.ShapeDtypeStruct((n, dk), F32)),
        name="rope_tables",
    )(pos, inv_dup, sgn)


def _mixer_prompt_kernel(x_ref, cos_ref, sin_ref, g_ref, win_ref, cw_ref, wco_ref, wro_ref,
                         wo_ref, out_ref, cst_ref, rst_ref, tail_ref, state_ref, decay_ref,
                         *, LT, L, H, DK, DV, DC, D, log_gamma):
    c = pl.program_id(1)
    DR = H * DK

    @pl.when(c == 0)
    def _init():
        tail_ref[...] = jnp.zeros_like(tail_ref)
        state_ref[...] = jnp.zeros_like(state_ref)
        diff = (lax.broadcasted_iota(jnp.int32, (L, L), 0)
                - lax.broadcasted_iota(jnp.int32, (L, L), 1)).astype(F32)
        for h in range(H):
            decay_ref[h] = jnp.where(diff >= 0.0,
                                     jnp.exp(jnp.maximum(diff, 0.0) * log_gamma[h]), 0.0)

    x = x_ref[...]
    xn = _rms(x, g_ref[...]).astype(BF16)

    zc = _dot(xn, win_ref[:, 0:3 * DC])
    cb, u = zc[:, 0:DC], zc[:, DC:2 * DC] * zc[:, 2 * DC:3 * DC]
    tail = tail_ref[...]
    p1, p2 = tail[7:8, :], tail[6:7, :]
    row = lax.broadcasted_iota(jnp.int32, (LT, DC), 0)
    u1 = jnp.where(row == 0, p1, pltpu.roll(u, 1, axis=0))
    u2 = jnp.where(row == 0, p2, jnp.where(row == 1, p1, pltpu.roll(u, 2, axis=0)))
    cw = cw_ref[...]
    conv = cw[0:1, :] * u2 + cw[1:2, :] * u1 + cw[2:3, :] * u
    tail_ref[...] = u[LT - 8:LT, :]
    a = _dot((cb * conv).astype(BF16), wco_ref[...])

    zr = _dot(xn, win_ref[:, 3 * DC:3 * DC + 4 * DR])
    ridx_k = lax.broadcasted_iota(jnp.int32, (L, DK), 0).astype(F32)
    ridx_v = lax.broadcasted_iota(jnp.int32, (L, DV), 0).astype(F32)
    subs = []
    for s in range(LT // L):
        rows = slice(s * L, (s + 1) * L)
        cos, sin = cos_ref[rows, :], sin_ref[rows, :]
        heads = []
        for h in range(H):
            lg = log_gamma[h]
            qh = zr[rows, h * DK:(h + 1) * DK]
            kh = zr[rows, DR + h * DK:DR + (h + 1) * DK]
            vh = zr[rows, 2 * DR + h * DV:2 * DR + (h + 1) * DV]
            gh = zr[rows, 3 * DR + h * DV:3 * DR + (h + 1) * DV]
            qh = qh * cos + pltpu.roll(qh, DK // 2, axis=1) * sin
            kh = (kh * cos + pltpu.roll(kh, DK // 2, axis=1) * sin) * (DK ** -0.5)
            qb, vb = qh.astype(BF16), vh.astype(BF16)
            s_prev = state_ref[h]
            scores = _dot_nt(qb, kh.astype(BF16)) * decay_ref[h]
            o = _dot(scores.astype(BF16), vb)
            o = o + _dot(qb, s_prev.astype(BF16)) * jnp.exp((ridx_v + 1.0) * lg)
            k_dec = jnp.exp((L - 1.0 - ridx_k) * lg)
            state_ref[h] = math.exp(L * lg) * s_prev + _dot_tn((kh * k_dec).astype(BF16), vb)
            mu = jnp.mean(o, axis=-1, keepdims=True)
            d = o - mu
            var = jnp.mean(d * d, axis=-1, keepdims=True)
            heads.append(_silu(gh) * (d * lax.rsqrt(var + EPS)))
        subs.append(jnp.concatenate(heads, axis=-1).astype(BF16))
    r = _dot(jnp.concatenate(subs, axis=0), wro_ref[...])

    zg = _dot(xn, win_ref[:, 3 * DC + 4 * DR:3 * DC + 4 * DR + 2 * D])
    m = _sigmoid(zg[:, 0:D]) * a + _sigmoid(zg[:, D:2 * D]) * r
    out_ref[...] = x + _dot(m.astype(BF16), wo_ref[...])

    @pl.when(c == pl.num_programs(1) - 1)
    def _final():
        cst_ref[...] = u[LT - 8:LT, :]
        rst_ref[...] = state_ref[...]


def _mixer_prompt(x, cos, sin, g, w_in_all, layer, cw, w_co, w_ro, w_o, *, H, DK, DV):
    B, T, D = x.shape
    DC = cw.shape[1]
    LT, L = MIX_CHUNK, RET_SUB
    const = lambda *shape: pl.BlockSpec(shape, lambda b, c: (0,) * len(shape))
    kern = functools.partial(_mixer_prompt_kernel, LT=LT, L=L, H=H, DK=DK, DV=DV, DC=DC, D=D,
                             log_gamma=_log_gammas(H))
    return pl.pallas_call(
        kern,
        grid=(B, T // LT),
        in_specs=[
            pl.BlockSpec((None, LT, D), lambda b, c: (b, c, 0)),
            pl.BlockSpec((LT, DK), lambda b, c: (c, 0)),
            pl.BlockSpec((LT, DK), lambda b, c: (c, 0)),
            const(1, D),
            pl.BlockSpec((None,) + w_in_all.shape[1:], lambda b, c: (layer, 0, 0)),
            const(*cw.shape), const(*w_co.shape),
            const(*w_ro.shape), const(*w_o.shape),
        ],
        out_specs=[
            pl.BlockSpec((None, LT, D), lambda b, c: (b, c, 0)),
            pl.BlockSpec((None, 8, DC), lambda b, c: (b, 0, 0)),
            pl.BlockSpec((None, H, DK, DV), lambda b, c: (b, 0, 0, 0)),
        ],
        out_shape=(jax.ShapeDtypeStruct((B, T, D), F32),
                   jax.ShapeDtypeStruct((B, 8, DC), F32),
                   jax.ShapeDtypeStruct((B, H, DK, DV), F32)),
        scratch_shapes=[pltpu.VMEM((8, DC), F32), pltpu.VMEM((H, DK, DV), F32),
                        pltpu.VMEM((H, L, L), F32)],
        compiler_params=_cparams(("arbitrary", "arbitrary")),
        name="mixer_prompt",
    )(x, cos, sin, g, w_in_all, cw, w_co, w_ro, w_o)


def _dec_inproj_kernel(x_ref, g_ref, win_ref, wqkt_ref, invt_ref, zm_ref, zg_ref, qkt_ref,
                       *, H, DK, n_main, pos0):
    xn = _rms(x_ref[...], g_ref[...]).astype(BF16)
    zm_ref[...] = _dot(xn, win_ref[:, 0:n_main])
    zg_ref[...] = _dot(xn, win_ref[:, n_main:win_ref.shape[1]])
    qkt = _dot_nt(wqkt_ref[...], xn)
    ang = pos0 * invt_ref[...]
    cos, sin = jnp.cos(ang), jnp.sin(ang)
    half = DK // 2
    for hh in range(2 * H):
        blk = qkt[hh * DK:(hh + 1) * DK, :]
        x1, x2 = blk[0:half, :], blk[half:DK, :]
        scale = 1.0 if hh < H else DK ** -0.5
        qkt_ref[hh * DK:hh * DK + half, :] = (x1 * cos[0:half] - x2 * sin[0:half]) * scale
        qkt_ref[hh * DK + half:(hh + 1) * DK, :] = (x2 * cos[half:DK] + x1 * sin[half:DK]) * scale


def _dec_inproj(x, g, w_in_all, layer, w_qkt, inv_t, n_main, *, H, DK):
    n, D = x.shape
    n_in = w_in_all.shape[2]
    kern = functools.partial(_dec_inproj_kernel, H=H, DK=DK, n_main=n_main,
                             pos0=float(PAST_LEN))
    whole = lambda a: pl.BlockSpec(a.shape, lambda i: (0,) * a.ndim)
    out_shape = (jax.ShapeDtypeStruct((n, n_main), F32),
                 jax.ShapeDtypeStruct((n, n_in - n_main), F32),
                 jax.ShapeDtypeStruct((w_qkt.shape[0], n), F32))
    return pl.pallas_call(
        kern,
        grid=(1,),
        in_specs=[whole(x), whole(g), pl.BlockSpec((None, D, n_in), lambda i: (layer, 0, 0)),
                  whole(w_qkt), whole(inv_t)],
        out_specs=[whole(s) for s in out_shape],
        out_shape=out_shape,
        compiler_params=_cparams(("arbitrary",)),
        name="dec_inproj",
    )(x, g, w_in_all, w_qkt, inv_t)


def _dec_state_kernel(zm_ref, qkt_ref, conv_ref, cw_ref, s_ref, *rest, BB, H, DK, DV, DC,
                      log_gamma):
    ain_ref, rin_ref, nconv_ref, ns_ref, o_scr = rest[-5:]
    DR = H * DK
    zm = zm_ref[...]
    cb, u = zm[:, 0:DC], zm[:, DC:2 * DC] * zm[:, 2 * DC:3 * DC]
    buf = conv_ref[...]
    b0, b1 = buf[:, 0:DC], buf[:, DC:2 * DC]
    cw = cw_ref[...]
    conv = cw[0:1, :] * b0 + cw[1:2, :] * b1 + cw[2:3, :] * u
    nconv_ref[:, 0:DC] = b1
    nconv_ref[:, DC:2 * DC] = u
    ain_ref[...] = cb * conv

    qkt = qkt_ref[...]
    v = zm[:, 3 * DC + 2 * DR:3 * DC + 3 * DR]
    g = zm[:, 3 * DC + 3 * DR:3 * DC + 4 * DR]
    for j in range(BB):
        for h in range(H):
            gamma = math.exp(log_gamma[h])
            qc = qkt[h * DK:(h + 1) * DK, j:j + 1]
            kc = qkt[DR + h * DK:DR + (h + 1) * DK, j:j + 1]
            s_prev = s_ref[j, h]
            vrow = v[j:j + 1, h * DV:(h + 1) * DV]
            qk = jnp.sum(qc * kc, axis=0, keepdims=True)
            inter = jnp.sum(qc * s_prev, axis=0, keepdims=True) * gamma
            o_scr[j:j + 1, h * DV:(h + 1) * DV] = qk * vrow + inter
            ns_ref[j, h] = gamma * s_prev + kc * vrow
    o = o_scr[...]
    for h in range(H):
        oh = o[:, h * DV:(h + 1) * DV]
        mu = jnp.mean(oh, axis=-1, keepdims=True)
        d = oh - mu
        var = jnp.mean(d * d, axis=-1, keepdims=True)
        rin_ref[:, h * DV:(h + 1) * DV] = _silu(g[:, h * DV:(h + 1) * DV]) * (d * lax.rsqrt(var + EPS))


def _dec_state(zm, qkt3, conv2d, cw, state_all, new_states, layer, *, H, DK, DV):
    n = zm.shape[0]
    depth = state_all.shape[0]
    DC = cw.shape[1]
    BB = DEC_BB
    kern = functools.partial(_dec_state_kernel, BB=BB, H=H, DK=DK, DV=DV, DC=DC,
                             log_gamma=_log_gammas(H))
    args = [zm, qkt3, conv2d, cw, state_all, new_states]
    in_specs = [
        pl.BlockSpec((BB, zm.shape[1]), lambda b: (b, 0)),
        pl.BlockSpec((None, qkt3.shape[1], BB), lambda b: (b, 0, 0)),
        pl.BlockSpec((BB, 2 * DC), lambda b: (b, 0)),
        pl.BlockSpec(cw.shape, lambda b: (0, 0)),
        pl.BlockSpec((None, BB, H, DK, DV), lambda b: (layer, b, 0, 0, 0)),
        pl.BlockSpec(memory_space=pl.ANY),
    ]
    aliases = {5: 3}
    return pl.pallas_call(
        kern,
        grid=(n // BB,),
        in_specs=in_specs,
        out_specs=[
            pl.BlockSpec((BB, DC), lambda b: (b, 0)),
            pl.BlockSpec((BB, H * DV), lambda b: (b, 0)),
            pl.BlockSpec((BB, 2 * DC), lambda b: (b, 0)),
            pl.BlockSpec((None, BB, H, DK, DV), lambda b: (layer, b, 0, 0, 0)),
        ],
        out_shape=(jax.ShapeDtypeStruct((n, DC), F32),
                   jax.ShapeDtypeStruct((n, H * DV), F32),
                   jax.ShapeDtypeStruct((n, 2 * DC), F32),
                   jax.ShapeDtypeStruct((depth, n, H, DK, DV), F32)),
        scratch_shapes=[pltpu.VMEM((BB, H * DV), F32)],
        input_output_aliases=aliases,
        compiler_params=_cparams(("arbitrary",)),
        name="dec_state",
    )(*args)


def _dec_outproj_kernel(x_ref, zg_ref, ain_ref, rin_ref, wco_ref, wro_ref, wo_ref, out_ref, *, D):
    a = _dot(ain_ref[...].astype(BF16), wco_ref[...])
    r = _dot(rin_ref[...].astype(BF16), wro_ref[...])
    zg = zg_ref[...]
    m = _sigmoid(zg[:, 0:D]) * a + _sigmoid(zg[:, D:2 * D]) * r
    out_ref[...] = x_ref[...] + _dot(m.astype(BF16), wo_ref[...])


def _dec_outproj(x, zg, ain, rin, w_co, w_ro, w_o):
    n, D = x.shape
    return pl.pallas_call(
        functools.partial(_dec_outproj_kernel, D=D),
        out_shape=jax.ShapeDtypeStruct((n, D), F32),
        compiler_params=pltpu.CompilerParams(vmem_limit_bytes=VMEM_LIMIT),
        name="dec_outproj",
    )(x, zg, ain, rin, w_co, w_ro, w_o)


def _ffn_dense_kernel(x_ref, g_ref, wg_ref, wu_ref, wd_ref, out_ref, hn_ref):
    @pl.when(pl.program_id(1) == 0)
    def _init():
        x = x_ref[...]
        hn_ref[...] = _rms(x, g_ref[...]).astype(BF16)
        out_ref[...] = x

    hn = hn_ref[...]
    act = (_silu(_dot(hn, wg_ref[...])) * _dot(hn, wu_ref[...])).astype(BF16)
    out_ref[...] += _dot(act, wd_ref[...])


def _ffn_dense(x, g, w_gu, w_down):
    n, D = x.shape
    FF = w_down.shape[0]
    tm = min(FFN_TM, n)
    fc = FFN_FC
    nf = FF // fc
    return pl.pallas_call(
        _ffn_dense_kernel,
        grid=(n // tm, nf),
        in_specs=[
            pl.BlockSpec((tm, D), lambda i, j: (i, 0)),
            pl.BlockSpec((1, D), lambda i, j: (0, 0)),
            pl.BlockSpec((D, fc), lambda i, j: (0, j)),
            pl.BlockSpec((D, fc), lambda i, j: (0, nf + j)),
            pl.BlockSpec((fc, D), lambda i, j: (j, 0)),
        ],
        out_specs=pl.BlockSpec((tm, D), lambda i, j: (i, 0)),
        out_shape=jax.ShapeDtypeStruct((n, D), F32),
        scratch_shapes=[pltpu.VMEM((tm, D), BF16)],
        compiler_params=_cparams(("arbitrary", "arbitrary")),
        name="ffn_dense",
    )(x, g, w_gu, w_gu, w_down)


def _router_kernel(x_ref, g_ref, wrt_ref, hn_ref, route_ref, cnt_ref, *, T, E, CH):
    hn = _rms(x_ref[...], g_ref[...])
    hn_ref[...] = hn
    logits = _dot_nt(wrt_ref[...], hn, precision=lax.Precision.HIGHEST)
    eidx = lax.broadcasted_iota(jnp.int32, (E, T), 0)
    m1 = jnp.max(logits, axis=0, keepdims=True)
    i1 = jnp.min(jnp.where(logits == m1, eidx, E), axis=0, keepdims=True)
    sel1 = eidx == i1
    rest = jnp.where(sel1, -jnp.inf, logits)
    m2 = jnp.max(rest, axis=0, keepdims=True)
    i2 = jnp.min(jnp.where(rest == m2, eidx, E), axis=0, keepdims=True)
    sel2 = eidx == i2
    e2 = jnp.exp(m2 - m1)
    w1 = 1.0 / (1.0 + e2)
    f1 = jnp.where(sel1, 1.0, 0.0)
    f2 = jnp.where(sel2, 1.0, 0.0)
    sel = f1 + f2
    selb = sel.astype(BF16)
    route_ref[0:1, :] = i1.astype(F32)
    route_ref[1:2, :] = i2.astype(F32)
    route_ref[4:5, :] = w1
    route_ref[5:6, :] = e2 * w1
    route_ref[6:8, :] = jnp.zeros((2, T), F32)
    src = lax.broadcasted_iota(jnp.int32, (CH, CH), 0)
    dst = lax.broadcasted_iota(jnp.int32, (CH, CH), 1)
    tri = jnp.where(src < dst, 1.0, 0.0).astype(BF16)
    cnt = jnp.zeros((E, 1), F32)
    for cidx in range(T // CH):
        cols = slice(cidx * CH, (cidx + 1) * CH)
        pos = _dot(selb[:, cols], tri) + cnt
        route_ref[2:3, cols] = jnp.sum(pos * f1[:, cols], axis=0, keepdims=True)
        route_ref[3:4, cols] = jnp.sum(pos * f2[:, cols], axis=0, keepdims=True)
        cnt = cnt + jnp.sum(sel[:, cols], axis=1, keepdims=True)
    cnt_ref[...] = jnp.broadcast_to(cnt, cnt_ref.shape).astype(jnp.int32)


def _router(x, g, w_router_t, T):
    n, D = x.shape
    E = w_router_t.shape[0]
    nt = n // T
    kern = functools.partial(_router_kernel, T=T, E=E, CH=min(256, T))
    return pl.pallas_call(
        kern,
        grid=(nt,),
        in_specs=[
            pl.BlockSpec((T, D), lambda i: (i, 0)),
            pl.BlockSpec((1, D), lambda i: (0, 0)),
            pl.BlockSpec((E, D), lambda i: (0, 0)),
        ],
        out_specs=[
            pl.BlockSpec((T, D), lambda i: (i, 0)),
            pl.BlockSpec((8, T), lambda i: (0, i)),
            pl.BlockSpec((None, E, 128), lambda i: (i, 0, 0)),
        ],
        out_shape=(jax.ShapeDtypeStruct((n, D), F32),
                   jax.ShapeDtypeStruct((8, n), F32),
                   jax.ShapeDtypeStruct((nt, E, 128), jnp.int32)),
        compiler_params=_cparams(("arbitrary",)),
        name="moe_router",
    )(x, g, w_router_t)


def _route_plan(counts, tm, n_tiles_max):
    E = counts.shape[1]
    tot = jnp.sum(counts, axis=0)
    tiles_e = (tot + (tm - 1)) // tm
    tile_end = jnp.cumsum(tiles_e)
    tile_start = tile_end - tiles_e
    base = (tile_start * tm)[None, :] + jnp.cumsum(counts, axis=0) - counts
    r = jnp.arange(n_tiles_max, dtype=jnp.int32)
    tile_e = jnp.minimum(jnp.sum(r[:, None] >= tile_end[None, :], axis=1), E - 1).astype(jnp.int32)
    n_used = tile_end[E - 1:E].astype(jnp.int32)
    valid = jnp.clip(tot[tile_e] - (r - tile_start[tile_e]) * tm, 0, tm)
    valid = jnp.where(r < n_used[0], valid, 0).astype(jnp.int32)
    return base.astype(jnp.int32), tile_e, valid, n_used


def _global_slots(route, base, T):
    nt, E = base.shape
    idx = route[0:TOP_K].astype(jnp.int32).reshape(TOP_K, nt, T)
    slot = route[TOP_K:2 * TOP_K].astype(jnp.int32).reshape(TOP_K, nt, T)
    onehot = idx[..., None] == jnp.arange(E, dtype=jnp.int32)
    start = jnp.sum(jnp.where(onehot, base[None, :, None, :], 0), axis=-1)
    return (start + slot).reshape(TOP_K, nt * T).T.reshape(-1)


SUBLANES = 8


def _for_each_row(n_rows, slots_ref, fn):
    def body(q, carry):
        first = q * (SUBLANES * TOP_K)
        for u in range(SUBLANES):
            fn(q, u, [slots_ref[first + (u * TOP_K + k)] for k in range(TOP_K)])
        return carry
    lax.fori_loop(0, n_rows // SUBLANES, body, 0)


def _dispatch_kernel(g_ref, hn_ref, xs_in_ref, xs_ref, sem, *, TD):
    del xs_in_ref

    def send(q, u, slots):
        for k, dst in enumerate(slots):
            pltpu.make_async_copy(hn_ref.at[q, pl.ds(u, 1)], xs_ref.at[pl.ds(dst, 1)],
                                  sem).start(priority=k % 2)
    _for_each_row(TD, g_ref, send)
    for k in range(TOP_K):
        pltpu.make_async_copy(hn_ref, hn_ref, sem).wait()


def _dispatch(g, hn, xs):
    n, D = hn.shape
    TD = min(MOE_TD, n)
    return pl.pallas_call(
        functools.partial(_dispatch_kernel, TD=TD),
        grid=(n // TD,),
        in_specs=[
            pl.BlockSpec((TOP_K * TD,), lambda i: (i,), memory_space=pltpu.SMEM),
            pl.BlockSpec((TD // SUBLANES, SUBLANES, D), lambda i: (i, 0, 0)),
            pl.BlockSpec(memory_space=pl.ANY),
        ],
        out_specs=pl.BlockSpec(memory_space=pl.ANY),
        out_shape=jax.ShapeDtypeStruct(xs.shape, xs.dtype),
        scratch_shapes=[pltpu.SemaphoreType.DMA(())],
        input_output_aliases={2: 0},
        compiler_params=_cparams(("arbitrary",)),
        name="moe_dispatch",
    )(g, hn.reshape(n // SUBLANES, SUBLANES, D), xs)


def _experts_kernel(te_ref, valid_ref, nused_ref, x_ref, wg_ref, wu_ref, wd_ref, y_ref,
                    *, TM, R, SUB):
    r, j = pl.program_id(0), pl.program_id(1)
    valid = jnp.where(r < nused_ref[0], valid_ref[r], 0)
    nch = lax.shift_right_logical(valid + (R - 1), R.bit_length() - 1)

    @pl.when((nch == 0) & (j == 0))
    def _idle():
        y_ref[...] = jnp.zeros_like(y_ref)

    def variant(rows):
        def run(first):
            wg, wu, wd = (w[...].astype(BF16) for w in (wg_ref, wu_ref, wd_ref))
            for lo in range(0, rows, SUB):
                hi = min(lo + SUB, rows)
                xs = x_ref[lo:hi, :].astype(BF16)
                y = _dot((_silu(_dot(xs, wg)) * _dot(xs, wu)).astype(BF16), wd)
                if first:
                    y_ref[lo:hi, :] = y
                else:
                    y_ref[lo:hi, :] += y
            if first and rows < TM:
                y_ref[rows:TM, :] = jnp.zeros((TM - rows, y_ref.shape[1]), F32)

        @pl.when((nch == rows // R) & (j == 0))
        def _first():
            run(True)

        @pl.when((nch == rows // R) & (j > 0))
        def _rest():
            run(False)

    for n in range(1, TM // R + 1):
        variant(n * R)


def _experts(tile_e, valid, n_used, xs, w_gu_all, w_down_all, layer):
    rows, D = xs.shape
    FF = w_down_all.shape[2]
    TM, R, fc = MOE_TM, MOE_ROWS, MOE_FC
    nf = FF // fc
    nt = rows // TM

    def tile_of(r, nu):
        return jnp.maximum(jnp.minimum(r, nu[0] - 1), 0)

    def chunk_of(r, j, nu):
        return jnp.where(r < nu[0], j, nf - 1)

    grid_spec = pltpu.PrefetchScalarGridSpec(
        num_scalar_prefetch=3,
        grid=(nt, nf),
        in_specs=[
            pl.BlockSpec((TM, D), lambda r, j, te, vl, nu: (tile_of(r, nu), 0)),
            pl.BlockSpec((None, None, D, fc),
                         lambda r, j, te, vl, nu: (layer, te[tile_of(r, nu)], 0, chunk_of(r, j, nu))),
            pl.BlockSpec((None, None, D, fc),
                         lambda r, j, te, vl, nu: (layer, te[tile_of(r, nu)], 0,
                                                   nf + chunk_of(r, j, nu))),
            pl.BlockSpec((None, None, fc, D),
                         lambda r, j, te, vl, nu: (layer, te[tile_of(r, nu)], chunk_of(r, j, nu), 0)),
        ],
        out_specs=pl.BlockSpec((TM, D), lambda r, j, te, vl, nu: (r, 0)),
    )
    return pl.pallas_call(
        functools.partial(_experts_kernel, TM=TM, R=R, SUB=MOE_SUB),
        grid_spec=grid_spec,
        out_shape=jax.ShapeDtypeStruct((rows, D), F32),
        compiler_params=_cparams(("arbitrary", "arbitrary")),
        name="moe_experts",
    )(tile_e, valid, n_used, xs, w_gu_all, w_gu_all, w_down_all)


def _ple_kernel(*refs, moe, final, tm):
    refs = list(refs)
    if moe:
        g_ref, gnext_ref, x_ref, w_ref, ys_ref = refs[0:5]
        refs = refs[5:]
        buf_ref, sem = refs[-2:]
        refs = refs[:-2]
    else:
        x_ref = refs.pop(0)
    p_ref, gn_ref, wgate_ref, wple_ref = refs[0:4]
    gfin_ref = refs[4] if final else None
    out_ref = refs[-1]

    if moe:
        i = pl.program_id(0)
        slot = lax.rem(i, 2)

        def fetch(slots_ref, s):
            def get(q, u, slots):
                for k, src in enumerate(slots):
                    pltpu.make_async_copy(ys_ref.at[pl.ds(src, 1)],
                                          buf_ref.at[s, k, q, pl.ds(u, 1)],
                                          sem.at[s]).start(priority=k % 2)
            _for_each_row(tm, slots_ref, get)

        @pl.when(i == 0)
        def _first():
            fetch(g_ref, 0)

        @pl.when(i + 1 < pl.num_programs(0))
        def _ahead():
            fetch(gnext_ref, 1 - slot)

        for k in range(TOP_K):
            pltpu.make_async_copy(buf_ref.at[slot, k], buf_ref.at[slot, k], sem.at[slot]).wait()

    n_parts = 2 if tm % (2 * SUBLANES) == 0 and tm >= 256 else 1
    rows_per = tm // n_parts
    for part in range(n_parts):
        rows = slice(part * rows_per, (part + 1) * rows_per)
        x = x_ref[rows, :]
        if moe:
            groups = slice(part * rows_per // SUBLANES, (part + 1) * rows_per // SUBLANES)
            w = w_ref[rows, :]
            for k in range(TOP_K):
                x = x + w[:, k:k + 1] * buf_ref[slot, k, groups].reshape(x.shape)
        gate = _sigmoid(_dot(_rms(x, gn_ref[...]).astype(BF16), wgate_ref[...]))
        y = x + gate * _dot(p_ref[rows, :].astype(BF16), wple_ref[...])
        if final:
            y = _rms(y, gfin_ref[...])
        out_ref[rows, :] = y


def _ple(x, moe, p_all, layer, g, w_gate, w_ple, g_final):
    n, D = x.shape
    DP = p_all.shape[-1]
    tm = min(PLE_TM, n)
    tok = pl.BlockSpec((tm, D), lambda i: (i, 0))
    const = lambda *shape: pl.BlockSpec(shape, lambda i: (0,) * len(shape))
    args, specs, scratch = [], [], []
    if moe is not None:
        slots, weights, ys = moe
        last = n // tm - 1
        args += [slots, slots, x, weights, ys]
        specs += [pl.BlockSpec((TOP_K * tm,), lambda i: (i,), memory_space=pltpu.SMEM),
                  pl.BlockSpec((TOP_K * tm,), lambda i: (jnp.minimum(i + 1, last),),
                               memory_space=pltpu.SMEM),
                  tok, pl.BlockSpec((tm, TOP_K), lambda i: (i, 0)),
                  pl.BlockSpec(memory_space=pl.ANY)]
        scratch = [pltpu.VMEM((2, TOP_K, tm // SUBLANES, SUBLANES, D), F32),
                   pltpu.SemaphoreType.DMA((2,))]
    else:
        args.append(x)
        specs.append(tok)
    args += [p_all, g, w_gate, w_ple]
    specs += [pl.BlockSpec((None, tm, DP), lambda i: (layer, i, 0)), const(1, D),
              const(D, D), const(DP, D)]
    if g_final is not None:
        args.append(g_final)
        specs.append(const(1, D))
    kern = functools.partial(_ple_kernel, moe=moe is not None, final=g_final is not None, tm=tm)
    return pl.pallas_call(
        kern,
        grid=(n // tm,),
        in_specs=specs,
        out_specs=tok,
        out_shape=jax.ShapeDtypeStruct((n, D), F32),
        scratch_shapes=scratch,
        compiler_params=_cparams(("arbitrary",)),
        name="ple_moe" if moe is not None else "ple",
    )(*args)


def kernel(x_prompt, x_sample, state_conv, state_ret, p_prompt, p_sample, norm_mix_g, w_in,
           conv_w, w_conv_out, w_ret_out, w_o, norm_ffn_g, w_dense_gu, w_dense_down, w_router,
           w_exp_gu, w_exp_down, norm_ple_g, w_ple, w_ple_gate, norm_final_g):
    B, T, D = x_prompt.shape
    NP = B * T
    NS = x_sample.shape[0]
    depth = w_in.shape[0]
    _, _, H, DK, DV = state_ret.shape
    DC = conv_w.shape[-1]
    DR = H * DK
    E = w_router.shape[-1]
    assert x_sample.shape[1] == 1 and CONV_W - 1 == state_conv.shape[2]
    assert T % MIX_CHUNK == 0 and NS % DEC_BB == 0 and NP % ROUTE_TILE == 0

    half = DK // 2
    inv = ROPE_BASE ** (-jnp.arange(half, dtype=F32) / half)
    inv_dup = jnp.concatenate([inv, inv]).reshape(1, DK)
    sgn = jnp.concatenate([-jnp.ones((half,), F32), jnp.ones((half,), F32)]).reshape(1, DK)
    pos = jnp.arange(T, dtype=F32).reshape(T, 1)
    cos_p, sin_p = _rope_tables(pos, inv_dup, sgn)
    inv_t = jnp.broadcast_to(inv_dup.reshape(DK, 1), (DK, NS))

    n_tiles_max = (TOP_K * (NP + NS)) // MOE_TM + E

    row = lambda a: a.reshape(1, -1)
    xp = x_prompt
    xs = x_sample.reshape(NS, D)
    pp = p_prompt.reshape(depth, NP, -1)
    ps = p_sample.reshape(depth, NS, -1)
    conv_p, ret_p, conv_s = [], [], []
    ret_s = jnp.zeros(state_ret.shape, F32)
    w_in_b = _to_bf16(w_in, 512)
    for i in range(depth):
        w_co_b = w_conv_out[i].astype(BF16)
        w_ro_b = w_ret_out[i].astype(BF16)
        w_o_b = w_o[i].astype(BF16)
        g_mix = row(norm_mix_g[i])

        xp, cst, rst = _mixer_prompt(xp, cos_p, sin_p, g_mix, w_in_b, i, conv_w[i], w_co_b,
                                     w_ro_b, w_o_b, H=H, DK=DK, DV=DV)
        conv_p.append(cst[:, 8 - (CONV_W - 1):, :])
        ret_p.append(rst)

        n_main = 3 * DC + 4 * DR
        w_qkt = w_in_b[i, :, 3 * DC:3 * DC + 2 * DR].T
        zm, zg, qkt = _dec_inproj(xs, g_mix, w_in_b, i, w_qkt, inv_t, n_main, H=H, DK=DK)
        qkt3 = qkt.reshape(2 * DR, NS // DEC_BB, DEC_BB).transpose(1, 0, 2)
        ain, rin, ncv, ret_s = _dec_state(zm, qkt3, state_conv[i].reshape(NS, -1), conv_w[i],
                                          state_ret, ret_s, i, H=H, DK=DK, DV=DV)
        xs = _dec_outproj(xs, zg, ain, rin, w_co_b, w_ro_b, w_o_b)
        conv_s.append(ncv.reshape(NS, CONV_W - 1, DC))

        g_ffn = row(norm_ffn_g[i])
        xp2 = xp.reshape(NP, D)
        if i % 2 == 0:
            w_gu_b = w_dense_gu[i // 2].astype(BF16)
            w_dn_b = w_dense_down[i // 2].astype(BF16)
            xp2 = _ffn_dense(xp2, g_ffn, w_gu_b, w_dn_b)
            xs = _ffn_dense(xs, g_ffn, w_gu_b, w_dn_b)
            moe_p = moe_s = None
        else:
            w_rt = w_router[i // 2].T
            hn_p, route_p, cnt_p = _router(xp2, g_ffn, w_rt, ROUTE_TILE)
            hn_s, route_s, cnt_s = _router(xs, g_ffn, w_rt, NS)
            counts = jnp.concatenate([cnt_p[:, :, 0], cnt_s[:, :, 0]], axis=0)
            base, tile_e, valid, n_used = _route_plan(counts, MOE_TM, n_tiles_max)
            ntp = NP // ROUTE_TILE
            slots_p = _global_slots(route_p, base[:ntp], ROUTE_TILE)
            slots_s = _global_slots(route_s, base[ntp:], NS)
            sorted_x = jnp.zeros((n_tiles_max * MOE_TM, D), F32)
            sorted_x = _dispatch(slots_p, hn_p, sorted_x)
            sorted_x = _dispatch(slots_s, hn_s, sorted_x)
            sorted_y = _experts(tile_e, valid, n_used, sorted_x, w_exp_gu, w_exp_down, i // 2)
            moe_p = (slots_p, route_p[4:6].T, sorted_y)
            moe_s = (slots_s, route_s[4:6].T, sorted_y)

        g_fin = row(norm_final_g) if i == depth - 1 else None
        w_pg_b = w_ple_gate[i].astype(BF16)
        w_pl_b = w_ple[i].astype(BF16)
        xp = _ple(xp2, moe_p, pp, i, row(norm_ple_g[i]), w_pg_b, w_pl_b, g_fin).reshape(B, T, D)
        xs = _ple(xs, moe_s, ps, i, row(norm_ple_g[i]), w_pg_b, w_pl_b, g_fin)

    return (xp, xs.reshape(NS, 1, D), jnp.stack(conv_p), jnp.stack(ret_p),
            jnp.stack(conv_s), ret_s)
```

```python
import functools
import math

import jax
import jax.numpy as jnp
from jax import lax
from jax.experimental import pallas as pl
from jax.experimental.pallas import tpu as pltpu

F32 = jnp.float32
BF16 = jnp.bfloat16

CONV_W = 3
TOP_K = 2
ROPE_BASE = 10000.0
EPS = 1e-6
PAST_LEN = 16384

V7X_VMEM_BYTES = 64 * 1024 * 1024
VMEM_LIMIT = V7X_VMEM_BYTES - 8 * 1024 * 1024

MIX_CHUNK = 1024
RET_SUB = 256
FFN_TM = 1024
FFN_FC = 512
ROUTE_TILE = 2048
MOE_TM = 1024
MOE_ROWS = 256
MOE_SUB = 1024
MOE_FC = 512
MOE_TD = 1024
PLE_TM = 512
DEC_BB = 16


def _cparams(sem):
    return pltpu.CompilerParams(dimension_semantics=sem, vmem_limit_bytes=VMEM_LIMIT)


def _dot(a, b):
    return jnp.dot(a, b, preferred_element_type=F32)


def _dot_nt(a, b, precision=None):
    return lax.dot_general(a, b, (((1,), (1,)), ((), ())), precision=precision,
                           preferred_element_type=F32)


def _dot_tn(a, b):
    return lax.dot_general(a, b, (((0,), (0,)), ((), ())), preferred_element_type=F32)


def _rms(x, g):
    return x * lax.rsqrt(jnp.mean(x * x, axis=-1, keepdims=True) + EPS) * g


def _sigmoid(x):
    return 1.0 / (1.0 + jnp.exp(-x))


def _silu(x):
    return x * _sigmoid(x)


def _log_gammas(n_heads):
    return tuple(math.log(1.0 - 2.0 ** (-5.0 - h)) for h in range(n_heads))


def _cast_kernel(w_ref, o_ref):
    o_ref[...] = w_ref[...].astype(o_ref.dtype)


def _to_bf16(w, bn):
    depth, K, N = w.shape
    spec = pl.BlockSpec((None, K, bn), lambda l, j: (l, 0, j))
    return pl.pallas_call(
        _cast_kernel,
        grid=(depth, N // bn),
        in_specs=[spec],
        out_specs=spec,
        out_shape=jax.ShapeDtypeStruct(w.shape, BF16),
        compiler_params=_cparams(("arbitrary", "arbitrary")),
        name="cast_bf16",
    )(w)


def _rope_table_kernel(pos_ref, inv_ref, sgn_ref, cos_ref, sin_ref):
    ang = pos_ref[...] * inv_ref[...]
    cos_ref[...] = jnp.cos(ang)
    sin_ref[...] = jnp.sin(ang) * sgn_ref[...]


def _rope_tables(pos, inv_dup, sgn):
    n = pos.shape[0]
    dk = inv_dup.shape[1]
    return pl.pallas_call(
        _rope_table_kernel,
        out_shape=(jax.ShapeDtypeStruct((n, dk), F32), jax.ShapeDtypeStruct((n, dk), F32)),
        name="rope_tables",
    )(pos, inv_dup, sgn)


def _mixer_prompt_kernel(x_ref, cos_ref, sin_ref, g_ref, win_ref, cw_ref, wco_ref, wro_ref,
                         wo_ref, out_ref, cst_ref, rst_ref, tail_ref, state_ref, decay_ref,
                         *, LT, L, H, DK, DV, DC, D, log_gamma):
    c = pl.program_id(1)
    DR = H * DK

    @pl.when(c == 0)
    def _init():
        tail_ref[...] = jnp.zeros_like(tail_ref)
        state_ref[...] = jnp.zeros_like(state_ref)
        diff = (lax.broadcasted_iota(jnp.int32, (L, L), 0)
                - lax.broadcasted_iota(jnp.int32, (L, L), 1)).astype(F32)
        for h in range(H):
            decay_ref[h] = jnp.where(diff >= 0.0,
                                     jnp.exp(jnp.maximum(diff, 0.0) * log_gamma[h]), 0.0)

    x = x_ref[...]
    xn = _rms(x, g_ref[...]).astype(BF16)

    zc = _dot(xn, win_ref[:, 0:3 * DC])
    cb, u = zc[:, 0:DC], zc[:, DC:2 * DC] * zc[:, 2 * DC:3 * DC]
    tail = tail_ref[...]
    p1, p2 = tail[7:8, :], tail[6:7, :]
    row = lax.broadcasted_iota(jnp.int32, (LT, DC), 0)
    u1 = jnp.where(row == 0, p1, pltpu.roll(u, 1, axis=0))
    u2 = jnp.where(row == 0, p2, jnp.where(row == 1, p1, pltpu.roll(u, 2, axis=0)))
    cw = cw_ref[...]
    conv = cw[0:1, :] * u2 + cw[1:2, :] * u1 + cw[2:3, :] * u
    tail_ref[...] = u[LT - 8:LT, :]
    a = _dot((cb * conv).astype(BF16), wco_ref[...])

    zr = _dot(xn, win_ref[:, 3 * DC:3 * DC + 4 * DR])
    ridx_k = lax.broadcasted_iota(jnp.int32, (L, DK), 0).astype(F32)
    ridx_v = lax.broadcasted_iota(jnp.int32, (L, DV), 0).astype(F32)
    subs = []
    for s in range(LT // L):
        rows = slice(s * L, (s + 1) * L)
        cos, sin = cos_ref[rows, :], sin_ref[rows, :]
        heads = []
        for h in range(H):
            lg = log_gamma[h]
            qh = zr[rows, h * DK:(h + 1) * DK]
            kh = zr[rows, DR + h * DK:DR + (h + 1) * DK]
            vh = zr[rows, 2 * DR + h * DV:2 * DR + (h + 1) * DV]
            gh = zr[rows, 3 * DR + h * DV:3 * DR + (h + 1) * DV]
            qh = qh * cos + pltpu.roll(qh, DK // 2, axis=1) * sin
            kh = (kh * cos + pltpu.roll(kh, DK // 2, axis=1) * sin) * (DK ** -0.5)
            qb, vb = qh.astype(BF16), vh.astype(BF16)
            s_prev = state_ref[h]
            scores = _dot_nt(qb, kh.astype(BF16)) * decay_ref[h]
            o = _dot(scores.astype(BF16), vb)
            o = o + _dot(qb, s_prev.astype(BF16)) * jnp.exp((ridx_v + 1.0) * lg)
            k_dec = jnp.exp((L - 1.0 - ridx_k) * lg)
            state_ref[h] = math.exp(L * lg) * s_prev + _dot_tn((kh * k_dec).astype(BF16), vb)
            mu = jnp.mean(o, axis=-1, keepdims=True)
            d = o - mu
            var = jnp.mean(d * d, axis=-1, keepdims=True)
            heads.append(_silu(gh) * (d * lax.rsqrt(var + EPS)))
        subs.append(jnp.concatenate(heads, axis=-1).astype(BF16))
    r = _dot(jnp.concatenate(subs, axis=0), wro_ref[...])

    zg = _dot(xn, win_ref[:, 3 * DC + 4 * DR:3 * DC + 4 * DR + 2 * D])
    m = _sigmoid(zg[:, 0:D]) * a + _sigmoid(zg[:, D:2 * D]) * r
    out_ref[...] = x + _dot(m.astype(BF16), wo_ref[...])

    @pl.when(c == pl.num_programs(1) - 1)
    def _final():
        cst_ref[...] = u[LT - 8:LT, :]
        rst_ref[...] = state_ref[...]


def _mixer_prompt(x, cos, sin, g, w_in_all, layer, cw, w_co, w_ro, w_o, *, H, DK, DV):
    B, T, D = x.shape
    DC = cw.shape[1]
    LT, L = MIX_CHUNK, RET_SUB
    const = lambda *shape: pl.BlockSpec(shape, lambda b, c: (0,) * len(shape))
    kern = functools.partial(_mixer_prompt_kernel, LT=LT, L=L, H=H, DK=DK, DV=DV, DC=DC, D=D,
                             log_gamma=_log_gammas(H))
    return pl.pallas_call(
        kern,
        grid=(B, T // LT),
        in_specs=[
            pl.BlockSpec((None, LT, D), lambda b, c: (b, c, 0)),
            pl.BlockSpec((LT, DK), lambda b, c: (c, 0)),
            pl.BlockSpec((LT, DK), lambda b, c: (c, 0)),
            const(1, D),
            pl.BlockSpec((None,) + w_in_all.shape[1:], lambda b, c: (layer, 0, 0)),
            const(*cw.shape), const(*w_co.shape),
            const(*w_ro.shape), const(*w_o.shape),
        ],
        out_specs=[
            pl.BlockSpec((None, LT, D), lambda b, c: (b, c, 0)),
            pl.BlockSpec((None, 8, DC), lambda b, c: (b, 0, 0)),
            pl.BlockSpec((None, H, DK, DV), lambda b, c: (b, 0, 0, 0)),
        ],
        out_shape=(jax.ShapeDtypeStruct((B, T, D), F32),
                   jax.ShapeDtypeStruct((B, 8, DC), F32),
                   jax.ShapeDtypeStruct((B, H, DK, DV), F32)),
        scratch_shapes=[pltpu.VMEM((8, DC), F32), pltpu.VMEM((H, DK, DV), F32),
                        pltpu.VMEM((H, L, L), F32)],
        compiler_params=_cparams(("arbitrary", "arbitrary")),
        name="mixer_prompt",
    )(x, cos, sin, g, w_in_all, cw, w_co, w_ro, w_o)


def _dec_inproj_kernel(x_ref, g_ref, win_ref, wqkt_ref, invt_ref, zm_ref, zg_ref, qkt_ref,
                       *, H, DK, n_main, pos0):
    xn = _rms(x_ref[...], g_ref[...]).astype(BF16)
    zm_ref[...] = _dot(xn, win_ref[:, 0:n_main])
    zg_ref[...] = _dot(xn, win_ref[:, n_main:win_ref.shape[1]])
    qkt = _dot_nt(wqkt_ref[...], xn)
    ang = pos0 * invt_ref[...]
    cos, sin = jnp.cos(ang), jnp.sin(ang)
    half = DK // 2
    for hh in range(2 * H):
        blk = qkt[hh * DK:(hh + 1) * DK, :]
        x1, x2 = blk[0:half, :], blk[half:DK, :]
        scale = 1.0 if hh < H else DK ** -0.5
        qkt_ref[hh * DK:hh * DK + half, :] = (x1 * cos[0:half] - x2 * sin[0:half]) * scale
        qkt_ref[hh * DK + half:(hh + 1) * DK, :] = (x2 * cos[half:DK] + x1 * sin[half:DK]) * scale


def _dec_inproj(x, g, w_in_all, layer, w_qkt, inv_t, n_main, *, H, DK):
    n, D = x.shape
    n_in = w_in_all.shape[2]
    kern = functools.partial(_dec_inproj_kernel, H=H, DK=DK, n_main=n_main,
                             pos0=float(PAST_LEN))
    whole = lambda a: pl.BlockSpec(a.shape, lambda i: (0,) * a.ndim)
    out_shape = (jax.ShapeDtypeStruct((n, n_main), F32),
                 jax.ShapeDtypeStruct((n, n_in - n_main), F32),
                 jax.ShapeDtypeStruct((w_qkt.shape[0], n), F32))
    return pl.pallas_call(
        kern,
        grid=(1,),
        in_specs=[whole(x), whole(g), pl.BlockSpec((None, D, n_in), lambda i: (layer, 0, 0)),
                  whole(w_qkt), whole(inv_t)],
        out_specs=[whole(s) for s in out_shape],
        out_shape=out_shape,
        compiler_params=_cparams(("arbitrary",)),
        name="dec_inproj",
    )(x, g, w_in_all, w_qkt, inv_t)


def _dec_state_kernel(zm_ref, qkt_ref, conv_ref, cw_ref, s_ref, *rest, BB, H, DK, DV, DC,
                      log_gamma):
    ain_ref, rin_ref, nconv_ref, ns_ref, o_scr = rest[-5:]
    DR = H * DK
    zm = zm_ref[...]
    cb, u = zm[:, 0:DC], zm[:, DC:2 * DC] * zm[:, 2 * DC:3 * DC]
    buf = conv_ref[...]
    b0, b1 = buf[:, 0:DC], buf[:, DC:2 * DC]
    cw = cw_ref[...]
    conv = cw[0:1, :] * b0 + cw[1:2, :] * b1 + cw[2:3, :] * u
    nconv_ref[:, 0:DC] = b1
    nconv_ref[:, DC:2 * DC] = u
    ain_ref[...] = cb * conv

    qkt = qkt_ref[...]
    v = zm[:, 3 * DC + 2 * DR:3 * DC + 3 * DR]
    g = zm[:, 3 * DC + 3 * DR:3 * DC + 4 * DR]
    for j in range(BB):
        for h in range(H):
            gamma = math.exp(log_gamma[h])
            qc = qkt[h * DK:(h + 1) * DK, j:j + 1]
            kc = qkt[DR + h * DK:DR + (h + 1) * DK, j:j + 1]
            s_prev = s_ref[j, h]
            vrow = v[j:j + 1, h * DV:(h + 1) * DV]
            qk = jnp.sum(qc * kc, axis=0, keepdims=True)
            inter = jnp.sum(qc * s_prev, axis=0, keepdims=True) * gamma
            o_scr[j:j + 1, h * DV:(h + 1) * DV] = qk * vrow + inter
            ns_ref[j, h] = gamma * s_prev + kc * vrow
    o = o_scr[...]
    for h in range(H):
        oh = o[:, h * DV:(h + 1) * DV]
        mu = jnp.mean(oh, axis=-1, keepdims=True)
        d = oh - mu
        var = jnp.mean(d * d, axis=-1, keepdims=True)
        rin_ref[:, h * DV:(h + 1) * DV] = _silu(g[:, h * DV:(h + 1) * DV]) * (d * lax.rsqrt(var + EPS))


def _dec_state(zm, qkt3, conv2d, cw, state_all, new_states, layer, *, H, DK, DV):
    n = zm.shape[0]
    depth = state_all.shape[0]
    DC = cw.shape[1]
    BB = DEC_BB
    kern = functools.partial(_dec_state_kernel, BB=BB, H=H, DK=DK, DV=DV, DC=DC,
                             log_gamma=_log_gammas(H))
    args = [zm, qkt3, conv2d, cw, state_all, new_states]
    in_specs = [
        pl.BlockSpec((BB, zm.shape[1]), lambda b: (b, 0)),
        pl.BlockSpec((None, qkt3.shape[1], BB), lambda b: (b, 0, 0)),
        pl.BlockSpec((BB, 2 * DC), lambda b: (b, 0)),
        pl.BlockSpec(cw.shape, lambda b: (0, 0)),
        pl.BlockSpec((None, BB, H, DK, DV), lambda b: (layer, b, 0, 0, 0)),
        pl.BlockSpec(memory_space=pl.ANY),
    ]
    aliases = {5: 3}
    return pl.pallas_call(
        kern,
        grid=(n // BB,),
        in_specs=in_specs,
        out_specs=[
            pl.BlockSpec((BB, DC), lambda b: (b, 0)),
            pl.BlockSpec((BB, H * DV), lambda b: (b, 0)),
            pl.BlockSpec((BB, 2 * DC), lambda b: (b, 0)),
            pl.BlockSpec((None, BB, H, DK, DV), lambda b: (layer, b, 0, 0, 0)),
        ],
        out_shape=(jax.ShapeDtypeStruct((n, DC), F32),
                   jax.ShapeDtypeStruct((n, H * DV), F32),
                   jax.ShapeDtypeStruct((n, 2 * DC), F32),
                   jax.ShapeDtypeStruct((depth, n, H, DK, DV), F32)),
        scratch_shapes=[pltpu.VMEM((BB, H * DV), F32)],
        input_output_aliases=aliases,
        compiler_params=_cparams(("arbitrary",)),
        name="dec_state",
    )(*args)


def _dec_outproj_kernel(x_ref, zg_ref, ain_ref, rin_ref, wco_ref, wro_ref, wo_ref, out_ref, *, D):
    a = _dot(ain_ref[...].astype(BF16), wco_ref[...])
    r = _dot(rin_ref[...].astype(BF16), wro_ref[...])
    zg = zg_ref[...]
    m = _sigmoid(zg[:, 0:D]) * a + _sigmoid(zg[:, D:2 * D]) * r
    out_ref[...] = x_ref[...] + _dot(m.astype(BF16), wo_ref[...])


def _dec_outproj(x, zg, ain, rin, w_co, w_ro, w_o):
    n, D = x.shape
    return pl.pallas_call(
        functools.partial(_dec_outproj_kernel, D=D),
        out_shape=jax.ShapeDtypeStruct((n, D), F32),
        compiler_params=pltpu.CompilerParams(vmem_limit_bytes=VMEM_LIMIT),
        name="dec_outproj",
    )(x, zg, ain, rin, w_co, w_ro, w_o)


def _ffn_dense_kernel(x_ref, g_ref, wg_ref, wu_ref, wd_ref, out_ref, hn_ref):
    @pl.when(pl.program_id(1) == 0)
    def _init():
        x = x_ref[...]
        hn_ref[...] = _rms(x, g_ref[...]).astype(BF16)
        out_ref[...] = x

    hn = hn_ref[...]
    act = (_silu(_dot(hn, wg_ref[...])) * _dot(hn, wu_ref[...])).astype(BF16)
    out_ref[...] += _dot(act, wd_ref[...])


def _ffn_dense(x, g, w_gu, w_down):
    n, D = x.shape
    FF = w_down.shape[0]
    tm = min(FFN_TM, n)
    fc = FFN_FC
    nf = FF // fc
    return pl.pallas_call(
        _ffn_dense_kernel,
        grid=(n // tm, nf),
        in_specs=[
            pl.BlockSpec((tm, D), lambda i, j: (i, 0)),
            pl.BlockSpec((1, D), lambda i, j: (0, 0)),
            pl.BlockSpec((D, fc), lambda i, j: (0, j)),
            pl.BlockSpec((D, fc), lambda i, j: (0, nf + j)),
            pl.BlockSpec((fc, D), lambda i, j: (j, 0)),
        ],
        out_specs=pl.BlockSpec((tm, D), lambda i, j: (i, 0)),
        out_shape=jax.ShapeDtypeStruct((n, D), F32),
        scratch_shapes=[pltpu.VMEM((tm, D), BF16)],
        compiler_params=_cparams(("arbitrary", "arbitrary")),
        name="ffn_dense",
    )(x, g, w_gu, w_gu, w_down)


def _router_kernel(x_ref, g_ref, wrt_ref, hn_ref, route_ref, cnt_ref, *, T, E, CH):
    hn = _rms(x_ref[...], g_ref[...])
    hn_ref[...] = hn
    logits = _dot_nt(wrt_ref[...], hn, precision=lax.Precision.HIGHEST)
    eidx = lax.broadcasted_iota(jnp.int32, (E, T), 0)
    m1 = jnp.max(logits, axis=0, keepdims=True)
    i1 = jnp.min(jnp.where(logits == m1, eidx, E), axis=0, keepdims=True)
    sel1 = eidx == i1
    rest = jnp.where(sel1, -jnp.inf, logits)
    m2 = jnp.max(rest, axis=0, keepdims=True)
    i2 = jnp.min(jnp.where(rest == m2, eidx, E), axis=0, keepdims=True)
    sel2 = eidx == i2
    e2 = jnp.exp(m2 - m1)
    w1 = 1.0 / (1.0 + e2)
    f1 = jnp.where(sel1, 1.0, 0.0)
    f2 = jnp.where(sel2, 1.0, 0.0)
    sel = f1 + f2
    selb = sel.astype(BF16)
    route_ref[0:1, :] = i1.astype(F32)
    route_ref[1:2, :] = i2.astype(F32)
    route_ref[4:5, :] = w1
    route_ref[5:6, :] = e2 * w1
    route_ref[6:8, :] = jnp.zeros((2, T), F32)
    src = lax.broadcasted_iota(jnp.int32, (CH, CH), 0)
    dst = lax.broadcasted_iota(jnp.int32, (CH, CH), 1)
    tri = jnp.where(src < dst, 1.0, 0.0).astype(BF16)
    cnt = jnp.zeros((E, 1), F32)
    for cidx in range(T // CH):
        cols = slice(cidx * CH, (cidx + 1) * CH)
        pos = _dot(selb[:, cols], tri) + cnt
        route_ref[2:3, cols] = jnp.sum(pos * f1[:, cols], axis=0, keepdims=True)
        route_ref[3:4, cols] = jnp.sum(pos * f2[:, cols], axis=0, keepdims=True)
        cnt = cnt + jnp.sum(sel[:, cols], axis=1, keepdims=True)
    cnt_ref[...] = jnp.broadcast_to(cnt, cnt_ref.shape).astype(jnp.int32)


def _router(x, g, w_router_t, T):
    n, D = x.shape
    E = w_router_t.shape[0]
    nt = n // T
    kern = functools.partial(_router_kernel, T=T, E=E, CH=min(256, T))
    return pl.pallas_call(
        kern,
        grid=(nt,),
        in_specs=[
            pl.BlockSpec((T, D), lambda i: (i, 0)),
            pl.BlockSpec((1, D), lambda i: (0, 0)),
            pl.BlockSpec((E, D), lambda i: (0, 0)),
        ],
        out_specs=[
            pl.BlockSpec((T, D), lambda i: (i, 0)),
            pl.BlockSpec((8, T), lambda i: (0, i)),
            pl.BlockSpec((None, E, 128), lambda i: (i, 0, 0)),
        ],
        out_shape=(jax.ShapeDtypeStruct((n, D), F32),
                   jax.ShapeDtypeStruct((8, n), F32),
                   jax.ShapeDtypeStruct((nt, E, 128), jnp.int32)),
        compiler_params=_cparams(("arbitrary",)),
        name="moe_router",
    )(x, g, w_router_t)


def _route_plan(counts, tm, n_tiles_max):
    E = counts.shape[1]
    tot = jnp.sum(counts, axis=0)
    tiles_e = (tot + (tm - 1)) // tm
    tile_end = jnp.cumsum(tiles_e)
    tile_start = tile_end - tiles_e
    base = (tile_start * tm)[None, :] + jnp.cumsum(counts, axis=0) - counts
    r = jnp.arange(n_tiles_max, dtype=jnp.int32)
    tile_e = jnp.minimum(jnp.sum(r[:, None] >= tile_end[None, :], axis=1), E - 1).astype(jnp.int32)
    n_used = tile_end[E - 1:E].astype(jnp.int32)
    valid = jnp.clip(tot[tile_e] - (r - tile_start[tile_e]) * tm, 0, tm)
    valid = jnp.where(r < n_used[0], valid, 0).astype(jnp.int32)
    return base.astype(jnp.int32), tile_e, valid, n_used


def _global_slots(route, base, T):
    nt, E = base.shape
    idx = route[0:TOP_K].astype(jnp.int32).reshape(TOP_K, nt, T)
    slot = route[TOP_K:2 * TOP_K].astype(jnp.int32).reshape(TOP_K, nt, T)
    onehot = idx[..., None] == jnp.arange(E, dtype=jnp.int32)
    start = jnp.sum(jnp.where(onehot, base[None, :, None, :], 0), axis=-1)
    return (start + slot).reshape(TOP_K, nt * T).T.reshape(-1)


SUBLANES = 8


def _for_each_row(n_rows, slots_ref, fn):
    def body(q, carry):
        first = q * (SUBLANES * TOP_K)
        for u in range(SUBLANES):
            fn(q, u, [slots_ref[first + (u * TOP_K + k)] for k in range(TOP_K)])
        return carry
    lax.fori_loop(0, n_rows // SUBLANES, body, 0)


def _dispatch_kernel(g_ref, hn_ref, xs_in_ref, xs_ref, sem, *, TD):
    del xs_in_ref

    def send(q, u, slots):
        for k, dst in enumerate(slots):
            pltpu.make_async_copy(hn_ref.at[q, pl.ds(u, 1)], xs_ref.at[pl.ds(dst, 1)],
                                  sem).start(priority=k % 2)
    _for_each_row(TD, g_ref, send)
    for k in range(TOP_K):
        pltpu.make_async_copy(hn_ref, hn_ref, sem).wait()


def _dispatch(g, hn, xs):
    n, D = hn.shape
    TD = min(MOE_TD, n)
    return pl.pallas_call(
        functools.partial(_dispatch_kernel, TD=TD),
        grid=(n // TD,),
        in_specs=[
            pl.BlockSpec((TOP_K * TD,), lambda i: (i,), memory_space=pltpu.SMEM),
            pl.BlockSpec((TD // SUBLANES, SUBLANES, D), lambda i: (i, 0, 0)),
            pl.BlockSpec(memory_space=pl.ANY),
        ],
        out_specs=pl.BlockSpec(memory_space=pl.ANY),
        out_shape=jax.ShapeDtypeStruct(xs.shape, xs.dtype),
        scratch_shapes=[pltpu.SemaphoreType.DMA(())],
        input_output_aliases={2: 0},
        compiler_params=_cparams(("arbitrary",)),
        name="moe_dispatch",
    )(g, hn.reshape(n // SUBLANES, SUBLANES, D), xs)


def _experts_kernel(te_ref, valid_ref, nused_ref, x_ref, wg_ref, wu_ref, wd_ref, y_ref,
                    *, TM, R, SUB):
    r, j = pl.program_id(0), pl.program_id(1)
    valid = jnp.where(r < nused_ref[0], valid_ref[r], 0)
    nch = lax.shift_right_logical(valid + (R - 1), R.bit_length() - 1)

    @pl.when((nch == 0) & (j == 0))
    def _idle():
        y_ref[...] = jnp.zeros_like(y_ref)

    def variant(rows):
        def run(first):
            wg, wu, wd = (w[...].astype(BF16) for w in (wg_ref, wu_ref, wd_ref))
            for lo in range(0, rows, SUB):
                hi = min(lo + SUB, rows)
                xs = x_ref[lo:hi, :].astype(BF16)
                y = _dot((_silu(_dot(xs, wg)) * _dot(xs, wu)).astype(BF16), wd)
                if first:
                    y_ref[lo:hi, :] = y
                else:
                    y_ref[lo:hi, :] += y
            if first and rows < TM:
                y_ref[rows:TM, :] = jnp.zeros((TM - rows, y_ref.shape[1]), F32)

        @pl.when((nch == rows // R) & (j == 0))
        def _first():
            run(True)

        @pl.when((nch == rows // R) & (j > 0))
        def _rest():
            run(False)

    for n in range(1, TM // R + 1):
        variant(n * R)


def _experts(tile_e, valid, n_used, xs, w_gu_all, w_down_all, layer):
    rows, D = xs.shape
    FF = w_down_all.shape[2]
    TM, R, fc = MOE_TM, MOE_ROWS, MOE_FC
    nf = FF // fc
    nt = rows // TM

    def tile_of(r, nu):
        return jnp.maximum(jnp.minimum(r, nu[0] - 1), 0)

    def chunk_of(r, j, nu):
        return jnp.where(r < nu[0], j, nf - 1)

    grid_spec = pltpu.PrefetchScalarGridSpec(
        num_scalar_prefetch=3,
        grid=(nt, nf),
        in_specs=[
            pl.BlockSpec((TM, D), lambda r, j, te, vl, nu: (tile_of(r, nu), 0)),
            pl.BlockSpec((None, None, D, fc),
                         lambda r, j, te, vl, nu: (layer, te[tile_of(r, nu)], 0, chunk_of(r, j, nu))),
            pl.BlockSpec((None, None, D, fc),
                         lambda r, j, te, vl, nu: (layer, te[tile_of(r, nu)], 0,
                                                   nf + chunk_of(r, j, nu))),
            pl.BlockSpec((None, None, fc, D),
                         lambda r, j, te, vl, nu: (layer, te[tile_of(r, nu)], chunk_of(r, j, nu), 0)),
        ],
        out_specs=pl.BlockSpec((TM, D), lambda r, j, te, vl, nu: (r, 0)),
    )
    return pl.pallas_call(
        functools.partial(_experts_kernel, TM=TM, R=R, SUB=MOE_SUB),
        grid_spec=grid_spec,
        out_shape=jax.ShapeDtypeStruct((rows, D), F32),
        compiler_params=_cparams(("arbitrary", "arbitrary")),
        name="moe_experts",
    )(tile_e, valid, n_used, xs, w_gu_all, w_gu_all, w_down_all)


def _ple_kernel(*refs, moe, final, tm):
    refs = list(refs)
    if moe:
        g_ref, gnext_ref, x_ref, w_ref, ys_ref = refs[0:5]
        refs = refs[5:]
        buf_ref, sem = refs[-2:]
        refs = refs[:-2]
    else:
        x_ref = refs.pop(0)
    p_ref, gn_ref, wgate_ref, wple_ref = refs[0:4]
    gfin_ref = refs[4] if final else None
    out_ref = refs[-1]

    if moe:
        i = pl.program_id(0)
        slot = lax.rem(i, 2)

        def fetch(slots_ref, s):
            def get(q, u, slots):
                for k, src in enumerate(slots):
                    pltpu.make_async_copy(ys_ref.at[pl.ds(src, 1)],
                                          buf_ref.at[s, k, q, pl.ds(u, 1)],
                                          sem.at[s]).start(priority=k % 2)
            _for_each_row(tm, slots_ref, get)

        @pl.when(i == 0)
        def _first():
            fetch(g_ref, 0)

        @pl.when(i + 1 < pl.num_programs(0))
        def _ahead():
            fetch(gnext_ref, 1 - slot)

        for k in range(TOP_K):
            pltpu.make_async_copy(buf_ref.at[slot, k], buf_ref.at[slot, k], sem.at[slot]).wait()

    n_parts = 2 if tm % (2 * SUBLANES) == 0 and tm >= 256 else 1
    rows_per = tm // n_parts
    for part in range(n_parts):
        rows = slice(part * rows_per, (part + 1) * rows_per)
        x = x_ref[rows, :]
        if moe:
            groups = slice(part * rows_per // SUBLANES, (part + 1) * rows_per // SUBLANES)
            w = w_ref[rows, :]
            for k in range(TOP_K):
                x = x + w[:, k:k + 1] * buf_ref[slot, k, groups].reshape(x.shape)
        gate = _sigmoid(_dot(_rms(x, gn_ref[...]).astype(BF16), wgate_ref[...]))
        y = x + gate * _dot(p_ref[rows, :].astype(BF16), wple_ref[...])
        if final:
            y = _rms(y, gfin_ref[...])
        out_ref[rows, :] = y


def _ple(x, moe, p_all, layer, g, w_gate, w_ple, g_final):
    n, D = x.shape
    DP = p_all.shape[-1]
    tm = min(PLE_TM, n)
    tok = pl.BlockSpec((tm, D), lambda i: (i, 0))
    const = lambda *shape: pl.BlockSpec(shape, lambda i: (0,) * len(shape))
    args, specs, scratch = [], [], []
    if moe is not None:
        slots, weights, ys = moe
        last = n // tm - 1
        args += [slots, slots, x, weights, ys]
        specs += [pl.BlockSpec((TOP_K * tm,), lambda i: (i,), memory_space=pltpu.SMEM),
                  pl.BlockSpec((TOP_K * tm,), lambda i: (jnp.minimum(i + 1, last),),
                               memory_space=pltpu.SMEM),
                  tok, pl.BlockSpec((tm, TOP_K), lambda i: (i, 0)),
                  pl.BlockSpec(memory_space=pl.ANY)]
        scratch = [pltpu.VMEM((2, TOP_K, tm // SUBLANES, SUBLANES, D), F32),
                   pltpu.SemaphoreType.DMA((2,))]
    else:
        args.append(x)
        specs.append(tok)
    args += [p_all, g, w_gate, w_ple]
    specs += [pl.BlockSpec((None, tm, DP), lambda i: (layer, i, 0)), const(1, D),
              const(D, D), const(DP, D)]
    if g_final is not None:
        args.append(g_final)
        specs.append(const(1, D))
    kern = functools.partial(_ple_kernel, moe=moe is not None, final=g_final is not None, tm=tm)
    return pl.pallas_call(
        kern,
        grid=(n // tm,),
        in_specs=specs,
        out_specs=tok,
        out_shape=jax.ShapeDtypeStruct((n, D), F32),
        scratch_shapes=scratch,
        compiler_params=_cparams(("arbitrary",)),
        name="ple_moe" if moe is not None else "ple",
    )(*args)


def kernel(x_prompt, x_sample, state_conv, state_ret, p_prompt, p_sample, norm_mix_g, w_in,
           conv_w, w_conv_out, w_ret_out, w_o, norm_ffn_g, w_dense_gu, w_dense_down, w_router,
           w_exp_gu, w_exp_down, norm_ple_g, w_ple, w_ple_gate, norm_final_g):
    B, T, D = x_prompt.shape
    NP = B * T
    NS = x_sample.shape[0]
    depth = w_in.shape[0]
    _, _, H, DK, DV = state_ret.shape
    DC = conv_w.shape[-1]
    DR = H * DK
    E = w_router.shape[-1]
    assert x_sample.shape[1] == 1 and CONV_W - 1 == state_conv.shape[2]
    assert T % MIX_CHUNK == 0 and NS % DEC_BB == 0 and NP % ROUTE_TILE == 0

    half = DK // 2
    inv = ROPE_BASE ** (-jnp.arange(half, dtype=F32) / half)
    inv_dup = jnp.concatenate([inv, inv]).reshape(1, DK)
    sgn = jnp.concatenate([-jnp.ones((half,), F32), jnp.ones((half,), F32)]).reshape(1, DK)
    pos = jnp.arange(T, dtype=F32).reshape(T, 1)
    cos_p, sin_p = _rope_tables(pos, inv_dup, sgn)
    inv_t = jnp.broadcast_to(inv_dup.reshape(DK, 1), (DK, NS))

    n_tiles_max = (TOP_K * (NP + NS)) // MOE_TM + E

    row = lambda a: a.reshape(1, -1)
    xp = x_prompt
    xs = x_sample.reshape(NS, D)
    pp = p_prompt.reshape(depth, NP, -1)
    ps = p_sample.reshape(depth, NS, -1)
    conv_p, ret_p, conv_s = [], [], []
    sorted_x = None
    ret_s = jnp.zeros(state_ret.shape, F32)
    w_in_b = _to_bf16(w_in, 512)
    for i in range(depth):
        w_co_b = w_conv_out[i].astype(BF16)
        w_ro_b = w_ret_out[i].astype(BF16)
        w_o_b = w_o[i].astype(BF16)
        g_mix = row(norm_mix_g[i])

        xp, cst, rst = _mixer_prompt(xp, cos_p, sin_p, g_mix, w_in_b, i, conv_w[i], w_co_b,
                                     w_ro_b, w_o_b, H=H, DK=DK, DV=DV)
        conv_p.append(cst[:, 8 - (CONV_W - 1):, :])
        ret_p.append(rst)

        n_main = 3 * DC + 4 * DR
        w_qkt = w_in_b[i, :, 3 * DC:3 * DC + 2 * DR].T
        zm, zg, qkt = _dec_inproj(xs, g_mix, w_in_b, i, w_qkt, inv_t, n_main, H=H, DK=DK)
        qkt3 = qkt.reshape(2 * DR, NS // DEC_BB, DEC_BB).transpose(1, 0, 2)
        ain, rin, ncv, ret_s = _dec_state(zm, qkt3, state_conv[i].reshape(NS, -1), conv_w[i],
                                          state_ret, ret_s, i, H=H, DK=DK, DV=DV)
        xs = _dec_outproj(xs, zg, ain, rin, w_co_b, w_ro_b, w_o_b)
        conv_s.append(ncv.reshape(NS, CONV_W - 1, DC))

        g_ffn = row(norm_ffn_g[i])
        xp2 = xp.reshape(NP, D)
        if i % 2 == 0:
            w_gu_b = w_dense_gu[i // 2].astype(BF16)
            w_dn_b = w_dense_down[i // 2].astype(BF16)
            xp2 = _ffn_dense(xp2, g_ffn, w_gu_b, w_dn_b)
            xs = _ffn_dense(xs, g_ffn, w_gu_b, w_dn_b)
            moe_p = moe_s = None
        else:
            w_rt = w_router[i // 2].T
            hn_p, route_p, cnt_p = _router(xp2, g_ffn, w_rt, ROUTE_TILE)
            hn_s, route_s, cnt_s = _router(xs, g_ffn, w_rt, NS)
            counts = jnp.concatenate([cnt_p[:, :, 0], cnt_s[:, :, 0]], axis=0)
            base, tile_e, valid, n_used = _route_plan(counts, MOE_TM, n_tiles_max)
            ntp = NP // ROUTE_TILE
            slots_p = _global_slots(route_p, base[:ntp], ROUTE_TILE)
            slots_s = _global_slots(route_s, base[ntp:], NS)
            if sorted_x is None:
                sorted_x = jnp.zeros((n_tiles_max * MOE_TM, D), F32)
            sorted_x = _dispatch(slots_p, hn_p, sorted_x)
            sorted_x = _dispatch(slots_s, hn_s, sorted_x)
            sorted_y = _experts(tile_e, valid, n_used, sorted_x, w_exp_gu, w_exp_down, i // 2)
            moe_p = (slots_p, route_p[4:6].T, sorted_y)
            moe_s = (slots_s, route_s[4:6].T, sorted_y)

        g_fin = row(norm_final_g) if i == depth - 1 else None
        w_pg_b = w_ple_gate[i].astype(BF16)
        w_pl_b = w_ple[i].astype(BF16)
        xp = _ple(xp2, moe_p, pp, i, row(norm_ple_g[i]), w_pg_b, w_pl_b, g_fin).reshape(B, T, D)
        xs = _ple(xs, moe_s, ps, i, row(norm_ple_g[i]), w_pg_b, w_pl_b, g_fin)

    return (xp, xs.reshape(NS, 1, D), jnp.stack(conv_p), jnp.stack(ret_p),
            jnp.stack(conv_s), ret_s)
```

```python
import functools
import math

import jax
import jax.numpy as jnp
from jax import lax
from jax.experimental import pallas as pl
from jax.experimental.pallas import tpu as pltpu

F32 = jnp.float32
BF16 = jnp.bfloat16

CONV_W = 3
TOP_K = 2
ROPE_BASE = 10000.0
EPS = 1e-6
PAST_LEN = 16384

V7X_VMEM_BYTES = 64 * 1024 * 1024
VMEM_LIMIT = V7X_VMEM_BYTES - 8 * 1024 * 1024

MIX_CHUNK = 1024
RET_SUB = 256
FFN_TM = 1024
FFN_FC = 512
ROUTE_TILE = 2048
MOE_TM = 1024
MOE_ROWS = 256
MOE_SUB = 1024
MOE_FC = 512
MOE_TD = 1024
PLE_TM = 512
DEC_BB = 16


def _cparams(sem):
    return pltpu.CompilerParams(dimension_semantics=sem, vmem_limit_bytes=VMEM_LIMIT)


def _dot(a, b):
    return jnp.dot(a, b, preferred_element_type=F32)


def _dot_nt(a, b, precision=None):
    return lax.dot_general(a, b, (((1,), (1,)), ((), ())), precision=precision,
                           preferred_element_type=F32)


def _dot_tn(a, b):
    return lax.dot_general(a, b, (((0,), (0,)), ((), ())), preferred_element_type=F32)


def _rms(x, g):
    return x * lax.rsqrt(jnp.mean(x * x, axis=-1, keepdims=True) + EPS) * g


def _sigmoid(x):
    return 1.0 / (1.0 + jnp.exp(-x))


def _silu(x):
    return x * _sigmoid(x)


def _log_gammas(n_heads):
    return tuple(math.log(1.0 - 2.0 ** (-5.0 - h)) for h in range(n_heads))


def _cast_kernel(w_ref, o_ref):
    o_ref[...] = w_ref[...].astype(o_ref.dtype)


def _to_bf16(w, bn):
    depth, K, N = w.shape
    spec = pl.BlockSpec((None, K, bn), lambda l, j: (l, 0, j))
    return pl.pallas_call(
        _cast_kernel,
        grid=(depth, N // bn),
        in_specs=[spec],
        out_specs=spec,
        out_shape=jax.ShapeDtypeStruct(w.shape, BF16),
        compiler_params=_cparams(("arbitrary", "arbitrary")),
        name="cast_bf16",
    )(w)


def _rope_table_kernel(pos_ref, inv_ref, sgn_ref, cos_ref, sin_ref):
    ang = pos_ref[...] * inv_ref[...]
    cos_ref[...] = jnp.cos(ang)
    sin_ref[...] = jnp.sin(ang) * sgn_ref[...]


def _rope_tables(pos, inv_dup, sgn):
    n = pos.shape[0]
    dk = inv_dup.shape[1]
    return pl.pallas_call(
        _rope_table_kernel,
        out_shape=(jax.ShapeDtypeStruct((n, dk), F32), jax.ShapeDtypeStruct((n, dk), F32)),
        name="rope_tables",
    )(pos, inv_dup, sgn)


def _mixer_prompt_kernel(x_ref, cos_ref, sin_ref, g_ref, win_ref, cw_ref, wco_ref, wro_ref,
                         wo_ref, out_ref, cst_ref, rst_ref, tail_ref, state_ref, decay_ref,
                         *, LT, L, H, DK, DV, DC, D, log_gamma):
    c = pl.program_id(1)
    DR = H * DK

    @pl.when(c == 0)
    def _init():
        tail_ref[...] = jnp.zeros_like(tail_ref)
        state_ref[...] = jnp.zeros_like(state_ref)
        diff = (lax.broadcasted_iota(jnp.int32, (L, L), 0)
                - lax.broadcasted_iota(jnp.int32, (L, L), 1)).astype(F32)
        for h in range(H):
            decay_ref[h] = jnp.where(diff >= 0.0,
                                     jnp.exp(jnp.maximum(diff, 0.0) * log_gamma[h]), 0.0)

    x = x_ref[...]
    xn = _rms(x, g_ref[...]).astype(BF16)

    zc = _dot(xn, win_ref[:, 0:3 * DC])
    cb, u = zc[:, 0:DC], zc[:, DC:2 * DC] * zc[:, 2 * DC:3 * DC]
    tail = tail_ref[...]
    p1, p2 = tail[7:8, :], tail[6:7, :]
    row = lax.broadcasted_iota(jnp.int32, (LT, DC), 0)
    u1 = jnp.where(row == 0, p1, pltpu.roll(u, 1, axis=0))
    u2 = jnp.where(row == 0, p2, jnp.where(row == 1, p1, pltpu.roll(u, 2, axis=0)))
    cw = cw_ref[...]
    conv = cw[0:1, :] * u2 + cw[1:2, :] * u1 + cw[2:3, :] * u
    tail_ref[...] = u[LT - 8:LT, :]
    a = _dot((cb * conv).astype(BF16), wco_ref[...])

    zr = _dot(xn, win_ref[:, 3 * DC:3 * DC + 4 * DR])
    ridx_k = lax.broadcasted_iota(jnp.int32, (L, DK), 0).astype(F32)
    ridx_v = lax.broadcasted_iota(jnp.int32, (L, DV), 0).astype(F32)
    subs = []
    for s in range(LT // L):
        rows = slice(s * L, (s + 1) * L)
        cos, sin = cos_ref[rows, :], sin_ref[rows, :]
        heads = []
        for h in range(H):
            lg = log_gamma[h]
            qh = zr[rows, h * DK:(h + 1) * DK]
            kh = zr[rows, DR + h * DK:DR + (h + 1) * DK]
            vh = zr[rows, 2 * DR + h * DV:2 * DR + (h + 1) * DV]
            gh = zr[rows, 3 * DR + h * DV:3 * DR + (h + 1) * DV]
            qh = qh * cos + pltpu.roll(qh, DK // 2, axis=1) * sin
            kh = (kh * cos + pltpu.roll(kh, DK // 2, axis=1) * sin) * (DK ** -0.5)
            qb, vb = qh.astype(BF16), vh.astype(BF16)
            s_prev = state_ref[h]
            scores = _dot_nt(qb, kh.astype(BF16)) * decay_ref[h]
            o = _dot(scores.astype(BF16), vb)
            o = o + _dot(qb, s_prev.astype(BF16)) * jnp.exp((ridx_v + 1.0) * lg)
            k_dec = jnp.exp((L - 1.0 - ridx_k) * lg)
            state_ref[h] = math.exp(L * lg) * s_prev + _dot_tn((kh * k_dec).astype(BF16), vb)
            mu = jnp.mean(o, axis=-1, keepdims=True)
            d = o - mu
            var = jnp.mean(d * d, axis=-1, keepdims=True)
            heads.append(_silu(gh) * (d * lax.rsqrt(var + EPS)))
        subs.append(jnp.concatenate(heads, axis=-1).astype(BF16))
    r = _dot(jnp.concatenate(subs, axis=0), wro_ref[...])

    zg = _dot(xn, win_ref[:, 3 * DC + 4 * DR:3 * DC + 4 * DR + 2 * D])
    m = _sigmoid(zg[:, 0:D]) * a + _sigmoid(zg[:, D:2 * D]) * r
    out_ref[...] = x + _dot(m.astype(BF16), wo_ref[...])

    @pl.when(c == pl.num_programs(1) - 1)
    def _final():
        cst_ref[...] = u[LT - 8:LT, :]
        rst_ref[...] = state_ref[...]


def _mixer_prompt(x, cos, sin, g, w_in_all, layer, cw, w_co, w_ro, w_o, *, H, DK, DV):
    B, T, D = x.shape
    DC = cw.shape[1]
    LT, L = MIX_CHUNK, RET_SUB
    const = lambda *shape: pl.BlockSpec(shape, lambda b, c: (0,) * len(shape))
    kern = functools.partial(_mixer_prompt_kernel, LT=LT, L=L, H=H, DK=DK, DV=DV, DC=DC, D=D,
                             log_gamma=_log_gammas(H))
    return pl.pallas_call(
        kern,
        grid=(B, T // LT),
        in_specs=[
            pl.BlockSpec((None, LT, D), lambda b, c: (b, c, 0)),
            pl.BlockSpec((LT, DK), lambda b, c: (c, 0)),
            pl.BlockSpec((LT, DK), lambda b, c: (c, 0)),
            const(1, D),
            pl.BlockSpec((None,) + w_in_all.shape[1:], lambda b, c: (layer, 0, 0)),
            const(*cw.shape), const(*w_co.shape),
            const(*w_ro.shape), const(*w_o.shape),
        ],
        out_specs=[
            pl.BlockSpec((None, LT, D), lambda b, c: (b, c, 0)),
            pl.BlockSpec((None, 8, DC), lambda b, c: (b, 0, 0)),
            pl.BlockSpec((None, H, DK, DV), lambda b, c: (b, 0, 0, 0)),
        ],
        out_shape=(jax.ShapeDtypeStruct((B, T, D), F32),
                   jax.ShapeDtypeStruct((B, 8, DC), F32),
                   jax.ShapeDtypeStruct((B, H, DK, DV), F32)),
        scratch_shapes=[pltpu.VMEM((8, DC), F32), pltpu.VMEM((H, DK, DV), F32),
                        pltpu.VMEM((H, L, L), F32)],
        compiler_params=_cparams(("arbitrary", "arbitrary")),
        name="mixer_prompt",
    )(x, cos, sin, g, w_in_all, cw, w_co, w_ro, w_o)


def _dec_inproj_kernel(x_ref, g_ref, win_ref, wqkt_ref, invt_ref, zm_ref, zg_ref, qkt_ref,
                       *, H, DK, n_main, pos0):
    xn = _rms(x_ref[...], g_ref[...]).astype(BF16)
    zm_ref[...] = _dot(xn, win_ref[:, 0:n_main])
    zg_ref[...] = _dot(xn, win_ref[:, n_main:win_ref.shape[1]])
    qkt = _dot_nt(wqkt_ref[...], xn)
    ang = pos0 * invt_ref[...]
    cos, sin = jnp.cos(ang), jnp.sin(ang)
    half = DK // 2
    for hh in range(2 * H):
        blk = qkt[hh * DK:(hh + 1) * DK, :]
        x1, x2 = blk[0:half, :], blk[half:DK, :]
        scale = 1.0 if hh < H else DK ** -0.5
        qkt_ref[hh * DK:hh * DK + half, :] = (x1 * cos[0:half] - x2 * sin[0:half]) * scale
        qkt_ref[hh * DK + half:(hh + 1) * DK, :] = (x2 * cos[half:DK] + x1 * sin[half:DK]) * scale


def _dec_inproj(x, g, w_in_all, layer, w_qkt, inv_t, n_main, *, H, DK):
    n, D = x.shape
    n_in = w_in_all.shape[2]
    kern = functools.partial(_dec_inproj_kernel, H=H, DK=DK, n_main=n_main,
                             pos0=float(PAST_LEN))
    whole = lambda a: pl.BlockSpec(a.shape, lambda i: (0,) * a.ndim)
    out_shape = (jax.ShapeDtypeStruct((n, n_main), F32),
                 jax.ShapeDtypeStruct((n, n_in - n_main), F32),
                 jax.ShapeDtypeStruct((w_qkt.shape[0], n), F32))
    return pl.pallas_call(
        kern,
        grid=(1,),
        in_specs=[whole(x), whole(g), pl.BlockSpec((None, D, n_in), lambda i: (layer, 0, 0)),
                  whole(w_qkt), whole(inv_t)],
        out_specs=[whole(s) for s in out_shape],
        out_shape=out_shape,
        compiler_params=_cparams(("arbitrary",)),
        name="dec_inproj",
    )(x, g, w_in_all, w_qkt, inv_t)


def _dec_state_kernel(zm_ref, qkt_ref, conv_ref, cw_ref, s_ref, *rest, BB, H, DK, DV, DC,
                      log_gamma):
    ain_ref, rin_ref, nconv_ref, ns_ref, o_scr = rest[-5:]
    DR = H * DK
    zm = zm_ref[...]
    cb, u = zm[:, 0:DC], zm[:, DC:2 * DC] * zm[:, 2 * DC:3 * DC]
    buf = conv_ref[...]
    b0, b1 = buf[:, 0:DC], buf[:, DC:2 * DC]
    cw = cw_ref[...]
    conv = cw[0:1, :] * b0 + cw[1:2, :] * b1 + cw[2:3, :] * u
    nconv_ref[:, 0:DC] = b1
    nconv_ref[:, DC:2 * DC] = u
    ain_ref[...] = cb * conv

    qkt = qkt_ref[...]
    v = zm[:, 3 * DC + 2 * DR:3 * DC + 3 * DR]
    g = zm[:, 3 * DC + 3 * DR:3 * DC + 4 * DR]
    for j in range(BB):
        for h in range(H):
            gamma = math.exp(log_gamma[h])
            qc = qkt[h * DK:(h + 1) * DK, j:j + 1]
            kc = qkt[DR + h * DK:DR + (h + 1) * DK, j:j + 1]
            s_prev = s_ref[j, h]
            vrow = v[j:j + 1, h * DV:(h + 1) * DV]
            qk = jnp.sum(qc * kc, axis=0, keepdims=True)
            inter = jnp.sum(qc * s_prev, axis=0, keepdims=True) * gamma
            o_scr[j:j + 1, h * DV:(h + 1) * DV] = qk * vrow + inter
            ns_ref[j, h] = gamma * s_prev + kc * vrow
    o = o_scr[...]
    for h in range(H):
        oh = o[:, h * DV:(h + 1) * DV]
        mu = jnp.mean(oh, axis=-1, keepdims=True)
        d = oh - mu
        var = jnp.mean(d * d, axis=-1, keepdims=True)
        rin_ref[:, h * DV:(h + 1) * DV] = _silu(g[:, h * DV:(h + 1) * DV]) * (d * lax.rsqrt(var + EPS))


def _dec_state(zm, qkt3, conv2d, cw, state_all, new_states, layer, *, H, DK, DV):
    n = zm.shape[0]
    depth = state_all.shape[0]
    DC = cw.shape[1]
    BB = DEC_BB
    kern = functools.partial(_dec_state_kernel, BB=BB, H=H, DK=DK, DV=DV, DC=DC,
                             log_gamma=_log_gammas(H))
    args = [zm, qkt3, conv2d, cw, state_all, new_states]
    in_specs = [
        pl.BlockSpec((BB, zm.shape[1]), lambda b: (b, 0)),
        pl.BlockSpec((None, qkt3.shape[1], BB), lambda b: (b, 0, 0)),
        pl.BlockSpec((BB, 2 * DC), lambda b: (b, 0)),
        pl.BlockSpec(cw.shape, lambda b: (0, 0)),
        pl.BlockSpec((None, BB, H, DK, DV), lambda b: (layer, b, 0, 0, 0)),
        pl.BlockSpec(memory_space=pl.ANY),
    ]
    aliases = {5: 3}
    return pl.pallas_call(
        kern,
        grid=(n // BB,),
        in_specs=in_specs,
        out_specs=[
            pl.BlockSpec((BB, DC), lambda b: (b, 0)),
            pl.BlockSpec((BB, H * DV), lambda b: (b, 0)),
            pl.BlockSpec((BB, 2 * DC), lambda b: (b, 0)),
            pl.BlockSpec((None, BB, H, DK, DV), lambda b: (layer, b, 0, 0, 0)),
        ],
        out_shape=(jax.ShapeDtypeStruct((n, DC), F32),
                   jax.ShapeDtypeStruct((n, H * DV), F32),
                   jax.ShapeDtypeStruct((n, 2 * DC), F32),
                   jax.ShapeDtypeStruct((depth, n, H, DK, DV), F32)),
        scratch_shapes=[pltpu.VMEM((BB, H * DV), F32)],
        input_output_aliases=aliases,
        compiler_params=_cparams(("arbitrary",)),
        name="dec_state",
    )(*args)


def _dec_outproj_kernel(x_ref, zg_ref, ain_ref, rin_ref, wco_ref, wro_ref, wo_ref, out_ref, *, D):
    a = _dot(ain_ref[...].astype(BF16), wco_ref[...])
    r = _dot(rin_ref[...].astype(BF16), wro_ref[...])
    zg = zg_ref[...]
    m = _sigmoid(zg[:, 0:D]) * a + _sigmoid(zg[:, D:2 * D]) * r
    out_ref[...] = x_ref[...] + _dot(m.astype(BF16), wo_ref[...])


def _dec_outproj(x, zg, ain, rin, w_co, w_ro, w_o):
    n, D = x.shape
    return pl.pallas_call(
        functools.partial(_dec_outproj_kernel, D=D),
        out_shape=jax.ShapeDtypeStruct((n, D), F32),
        compiler_params=pltpu.CompilerParams(vmem_limit_bytes=VMEM_LIMIT),
        name="dec_outproj",
    )(x, zg, ain, rin, w_co, w_ro, w_o)


def _ffn_dense_kernel(x_ref, g_ref, wg_ref, wu_ref, wd_ref, out_ref, hn_ref):
    @pl.when(pl.program_id(1) == 0)
    def _init():
        x = x_ref[...]
        hn_ref[...] = _rms(x, g_ref[...]).astype(BF16)
        out_ref[...] = x

    hn = hn_ref[...]
    act = (_silu(_dot(hn, wg_ref[...].astype(BF16)))
           * _dot(hn, wu_ref[...].astype(BF16))).astype(BF16)
    out_ref[...] += _dot(act, wd_ref[...].astype(BF16))


def _ffn_dense(x, g, w_gu_all, w_down_all, layer):
    n, D = x.shape
    FF = w_down_all.shape[1]
    tm = min(FFN_TM, n)
    fc = FFN_FC
    nf = FF // fc
    return pl.pallas_call(
        _ffn_dense_kernel,
        grid=(n // tm, nf),
        in_specs=[
            pl.BlockSpec((tm, D), lambda i, j: (i, 0)),
            pl.BlockSpec((1, D), lambda i, j: (0, 0)),
            pl.BlockSpec((None, D, fc), lambda i, j: (layer, 0, j)),
            pl.BlockSpec((None, D, fc), lambda i, j: (layer, 0, nf + j)),
            pl.BlockSpec((None, fc, D), lambda i, j: (layer, j, 0)),
        ],
        out_specs=pl.BlockSpec((tm, D), lambda i, j: (i, 0)),
        out_shape=jax.ShapeDtypeStruct((n, D), F32),
        scratch_shapes=[pltpu.VMEM((tm, D), BF16)],
        compiler_params=_cparams(("arbitrary", "arbitrary")),
        name="ffn_dense",
    )(x, g, w_gu_all, w_gu_all, w_down_all)


def _router_kernel(x_ref, g_ref, wrt_ref, hn_ref, route_ref, cnt_ref, *, T, E, CH):
    hn = _rms(x_ref[...], g_ref[...])
    hn_ref[...] = hn
    logits = _dot_nt(wrt_ref[...], hn, precision=lax.Precision.HIGHEST)
    eidx = lax.broadcasted_iota(jnp.int32, (E, T), 0)
    m1 = jnp.max(logits, axis=0, keepdims=True)
    i1 = jnp.min(jnp.where(logits == m1, eidx, E), axis=0, keepdims=True)
    sel1 = eidx == i1
    rest = jnp.where(sel1, -jnp.inf, logits)
    m2 = jnp.max(rest, axis=0, keepdims=True)
    i2 = jnp.min(jnp.where(rest == m2, eidx, E), axis=0, keepdims=True)
    sel2 = eidx == i2
    e2 = jnp.exp(m2 - m1)
    w1 = 1.0 / (1.0 + e2)
    f1 = jnp.where(sel1, 1.0, 0.0)
    f2 = jnp.where(sel2, 1.0, 0.0)
    sel = f1 + f2
    selb = sel.astype(BF16)
    route_ref[0:1, :] = i1.astype(F32)
    route_ref[1:2, :] = i2.astype(F32)
    route_ref[4:5, :] = w1
    route_ref[5:6, :] = e2 * w1
    route_ref[6:8, :] = jnp.zeros((2, T), F32)
    src = lax.broadcasted_iota(jnp.int32, (CH, CH), 0)
    dst = lax.broadcasted_iota(jnp.int32, (CH, CH), 1)
    tri = jnp.where(src < dst, 1.0, 0.0).astype(BF16)
    cnt = jnp.zeros((E, 1), F32)
    for cidx in range(T // CH):
        cols = slice(cidx * CH, (cidx + 1) * CH)
        pos = _dot(selb[:, cols], tri) + cnt
        route_ref[2:3, cols] = jnp.sum(pos * f1[:, cols], axis=0, keepdims=True)
        route_ref[3:4, cols] = jnp.sum(pos * f2[:, cols], axis=0, keepdims=True)
        cnt = cnt + jnp.sum(sel[:, cols], axis=1, keepdims=True)
    cnt_ref[...] = jnp.broadcast_to(cnt, cnt_ref.shape).astype(jnp.int32)


def _router(x, g, w_router_t, T):
    n, D = x.shape
    E = w_router_t.shape[0]
    nt = n // T
    kern = functools.partial(_router_kernel, T=T, E=E, CH=min(256, T))
    return pl.pallas_call(
        kern,
        grid=(nt,),
        in_specs=[
            pl.BlockSpec((T, D), lambda i: (i, 0)),
            pl.BlockSpec((1, D), lambda i: (0, 0)),
            pl.BlockSpec((E, D), lambda i: (0, 0)),
        ],
        out_specs=[
            pl.BlockSpec((T, D), lambda i: (i, 0)),
            pl.BlockSpec((8, T), lambda i: (0, i)),
            pl.BlockSpec((None, E, 128), lambda i: (i, 0, 0)),
        ],
        out_shape=(jax.ShapeDtypeStruct((n, D), F32),
                   jax.ShapeDtypeStruct((8, n), F32),
                   jax.ShapeDtypeStruct((nt, E, 128), jnp.int32)),
        compiler_params=_cparams(("arbitrary",)),
        name="moe_router",
    )(x, g, w_router_t)


def _route_plan(counts, tm, n_tiles_max):
    E = counts.shape[1]
    tot = jnp.sum(counts, axis=0)
    tiles_e = (tot + (tm - 1)) // tm
    tile_end = jnp.cumsum(tiles_e)
    tile_start = tile_end - tiles_e
    base = (tile_start * tm)[None, :] + jnp.cumsum(counts, axis=0) - counts
    r = jnp.arange(n_tiles_max, dtype=jnp.int32)
    tile_e = jnp.minimum(jnp.sum(r[:, None] >= tile_end[None, :], axis=1), E - 1).astype(jnp.int32)
    n_used = tile_end[E - 1:E].astype(jnp.int32)
    valid = jnp.clip(tot[tile_e] - (r - tile_start[tile_e]) * tm, 0, tm)
    valid = jnp.where(r < n_used[0], valid, 0).astype(jnp.int32)
    return base.astype(jnp.int32), tile_e, valid, n_used


def _global_slots(route, base, T):
    nt, E = base.shape
    idx = route[0:TOP_K].astype(jnp.int32).reshape(TOP_K, nt, T)
    slot = route[TOP_K:2 * TOP_K].astype(jnp.int32).reshape(TOP_K, nt, T)
    onehot = idx[..., None] == jnp.arange(E, dtype=jnp.int32)
    start = jnp.sum(jnp.where(onehot, base[None, :, None, :], 0), axis=-1)
    return (start + slot).reshape(TOP_K, nt * T).T.reshape(-1)


SUBLANES = 8


def _for_each_row(n_rows, slots_ref, fn):
    def body(q, carry):
        first = q * (SUBLANES * TOP_K)
        for u in range(SUBLANES):
            fn(q, u, [slots_ref[first + (u * TOP_K + k)] for k in range(TOP_K)])
        return carry
    lax.fori_loop(0, n_rows // SUBLANES, body, 0)


def _dispatch_kernel(g_ref, hn_ref, xs_in_ref, xs_ref, sem, *, TD):
    del xs_in_ref

    def send(q, u, slots):
        for k, dst in enumerate(slots):
            pltpu.make_async_copy(hn_ref.at[q, pl.ds(u, 1)], xs_ref.at[pl.ds(dst, 1)],
                                  sem).start(priority=k % 2)
    _for_each_row(TD, g_ref, send)
    for k in range(TOP_K):
        pltpu.make_async_copy(hn_ref, hn_ref, sem).wait()


def _dispatch(g, hn, xs):
    n, D = hn.shape
    TD = min(MOE_TD, n)
    return pl.pallas_call(
        functools.partial(_dispatch_kernel, TD=TD),
        grid=(n // TD,),
        in_specs=[
            pl.BlockSpec((TOP_K * TD,), lambda i: (i,), memory_space=pltpu.SMEM),
            pl.BlockSpec((TD // SUBLANES, SUBLANES, D), lambda i: (i, 0, 0)),
            pl.BlockSpec(memory_space=pl.ANY),
        ],
        out_specs=pl.BlockSpec(memory_space=pl.ANY),
        out_shape=jax.ShapeDtypeStruct(xs.shape, xs.dtype),
        scratch_shapes=[pltpu.SemaphoreType.DMA(())],
        input_output_aliases={2: 0},
        compiler_params=_cparams(("arbitrary",)),
        name="moe_dispatch",
    )(g, hn.reshape(n // SUBLANES, SUBLANES, D), xs)


def _experts_kernel(te_ref, valid_ref, nused_ref, x_ref, wg_ref, wu_ref, wd_ref, y_ref,
                    *, TM, R, SUB):
    r, j = pl.program_id(0), pl.program_id(1)
    valid = jnp.where(r < nused_ref[0], valid_ref[r], 0)
    nch = lax.shift_right_logical(valid + (R - 1), R.bit_length() - 1)

    @pl.when((nch == 0) & (j == 0))
    def _idle():
        y_ref[...] = jnp.zeros_like(y_ref)

    def variant(rows):
        def run(first):
            wg, wu, wd = (w[...].astype(BF16) for w in (wg_ref, wu_ref, wd_ref))
            for lo in range(0, rows, SUB):
                hi = min(lo + SUB, rows)
                xs = x_ref[lo:hi, :].astype(BF16)
                y = _dot((_silu(_dot(xs, wg)) * _dot(xs, wu)).astype(BF16), wd)
                if first:
                    y_ref[lo:hi, :] = y
                else:
                    y_ref[lo:hi, :] += y
            if first and rows < TM:
                y_ref[rows:TM, :] = jnp.zeros((TM - rows, y_ref.shape[1]), F32)

        @pl.when((nch == rows // R) & (j == 0))
        def _first():
            run(True)

        @pl.when((nch == rows // R) & (j > 0))
        def _rest():
            run(False)

    for n in range(1, TM // R + 1):
        variant(n * R)


def _experts(tile_e, valid, n_used, xs, w_gu_all, w_down_all, layer):
    rows, D = xs.shape
    FF = w_down_all.shape[2]
    TM, R, fc = MOE_TM, MOE_ROWS, MOE_FC
    nf = FF // fc
    nt = rows // TM

    def tile_of(r, nu):
        return jnp.maximum(jnp.minimum(r, nu[0] - 1), 0)

    def chunk_of(r, j, nu):
        return jnp.where(r < nu[0], j, nf - 1)

    grid_spec = pltpu.PrefetchScalarGridSpec(
        num_scalar_prefetch=3,
        grid=(nt, nf),
        in_specs=[
            pl.BlockSpec((TM, D), lambda r, j, te, vl, nu: (tile_of(r, nu), 0)),
            pl.BlockSpec((None, None, D, fc),
                         lambda r, j, te, vl, nu: (layer, te[tile_of(r, nu)], 0, chunk_of(r, j, nu))),
            pl.BlockSpec((None, None, D, fc),
                         lambda r, j, te, vl, nu: (layer, te[tile_of(r, nu)], 0,
                                                   nf + chunk_of(r, j, nu))),
            pl.BlockSpec((None, None, fc, D),
                         lambda r, j, te, vl, nu: (layer, te[tile_of(r, nu)], chunk_of(r, j, nu), 0)),
        ],
        out_specs=pl.BlockSpec((TM, D), lambda r, j, te, vl, nu: (r, 0)),
    )
    return pl.pallas_call(
        functools.partial(_experts_kernel, TM=TM, R=R, SUB=MOE_SUB),
        grid_spec=grid_spec,
        out_shape=jax.ShapeDtypeStruct((rows, D), F32),
        compiler_params=_cparams(("arbitrary", "arbitrary")),
        name="moe_experts",
    )(tile_e, valid, n_used, xs, w_gu_all, w_gu_all, w_down_all)


def _ple_kernel(*refs, moe, final, tm):
    refs = list(refs)
    if moe:
        g_ref, gnext_ref, x_ref, w_ref, ys_ref = refs[0:5]
        refs = refs[5:]
        buf_ref, sem = refs[-2:]
        refs = refs[:-2]
    else:
        x_ref = refs.pop(0)
    p_ref, gn_ref, wgate_ref, wple_ref = refs[0:4]
    gfin_ref = refs[4] if final else None
    out_ref = refs[-1]

    if moe:
        i = pl.program_id(0)
        slot = lax.rem(i, 2)

        def fetch(slots_ref, s):
            def get(q, u, slots):
                for k, src in enumerate(slots):
                    pltpu.make_async_copy(ys_ref.at[pl.ds(src, 1)],
                                          buf_ref.at[s, k, q, pl.ds(u, 1)],
                                          sem.at[s]).start(priority=k % 2)
            _for_each_row(tm, slots_ref, get)

        @pl.when(i == 0)
        def _first():
            fetch(g_ref, 0)

        @pl.when(i + 1 < pl.num_programs(0))
        def _ahead():
            fetch(gnext_ref, 1 - slot)

        for k in range(TOP_K):
            pltpu.make_async_copy(buf_ref.at[slot, k], buf_ref.at[slot, k], sem.at[slot]).wait()

    n_parts = 2 if tm % (2 * SUBLANES) == 0 and tm >= 256 else 1
    rows_per = tm // n_parts
    for part in range(n_parts):
        rows = slice(part * rows_per, (part + 1) * rows_per)
        x = x_ref[rows, :]
        if moe:
            groups = slice(part * rows_per // SUBLANES, (part + 1) * rows_per // SUBLANES)
            w = w_ref[rows, :]
            for k in range(TOP_K):
                x = x + w[:, k:k + 1] * buf_ref[slot, k, groups].reshape(x.shape)
        gate = _sigmoid(_dot(_rms(x, gn_ref[...]).astype(BF16), wgate_ref[...]))
        y = x + gate * _dot(p_ref[rows, :].astype(BF16), wple_ref[...])
        if final:
            y = _rms(y, gfin_ref[...])
        out_ref[rows, :] = y


def _ple(x, moe, p_all, layer, g, w_gate, w_ple, g_final):
    n, D = x.shape
    DP = p_all.shape[-1]
    tm = min(PLE_TM, n)
    tok = pl.BlockSpec((tm, D), lambda i: (i, 0))
    const = lambda *shape: pl.BlockSpec(shape, lambda i: (0,) * len(shape))
    args, specs, scratch = [], [], []
    if moe is not None:
        slots, weights, ys = moe
        last = n // tm - 1
        args += [slots, slots, x, weights, ys]
        specs += [pl.BlockSpec((TOP_K * tm,), lambda i: (i,), memory_space=pltpu.SMEM),
                  pl.BlockSpec((TOP_K * tm,), lambda i: (jnp.minimum(i + 1, last),),
                               memory_space=pltpu.SMEM),
                  tok, pl.BlockSpec((tm, TOP_K), lambda i: (i, 0)),
                  pl.BlockSpec(memory_space=pl.ANY)]
        scratch = [pltpu.VMEM((2, TOP_K, tm // SUBLANES, SUBLANES, D), F32),
                   pltpu.SemaphoreType.DMA((2,))]
    else:
        args.append(x)
        specs.append(tok)
    args += [p_all, g, w_gate, w_ple]
    specs += [pl.BlockSpec((None, tm, DP), lambda i: (layer, i, 0)), const(1, D),
              const(D, D), const(DP, D)]
    if g_final is not None:
        args.append(g_final)
        specs.append(const(1, D))
    kern = functools.partial(_ple_kernel, moe=moe is not None, final=g_final is not None, tm=tm)
    return pl.pallas_call(
        kern,
        grid=(n // tm,),
        in_specs=specs,
        out_specs=tok,
        out_shape=jax.ShapeDtypeStruct((n, D), F32),
        scratch_shapes=scratch,
        compiler_params=_cparams(("arbitrary",)),
        name="ple_moe" if moe is not None else "ple",
    )(*args)


def kernel(x_prompt, x_sample, state_conv, state_ret, p_prompt, p_sample, norm_mix_g, w_in,
           conv_w, w_conv_out, w_ret_out, w_o, norm_ffn_g, w_dense_gu, w_dense_down, w_router,
           w_exp_gu, w_exp_down, norm_ple_g, w_ple, w_ple_gate, norm_final_g):
    B, T, D = x_prompt.shape
    NP = B * T
    NS = x_sample.shape[0]
    depth = w_in.shape[0]
    _, _, H, DK, DV = state_ret.shape
    DC = conv_w.shape[-1]
    DR = H * DK
    E = w_router.shape[-1]
    assert x_sample.shape[1] == 1 and CONV_W - 1 == state_conv.shape[2]
    assert T % MIX_CHUNK == 0 and NS % DEC_BB == 0 and NP % ROUTE_TILE == 0

    half = DK // 2
    inv = ROPE_BASE ** (-jnp.arange(half, dtype=F32) / half)
    inv_dup = jnp.concatenate([inv, inv]).reshape(1, DK)
    sgn = jnp.concatenate([-jnp.ones((half,), F32), jnp.ones((half,), F32)]).reshape(1, DK)
    pos = jnp.arange(T, dtype=F32).reshape(T, 1)
    cos_p, sin_p = _rope_tables(pos, inv_dup, sgn)
    inv_t = jnp.broadcast_to(inv_dup.reshape(DK, 1), (DK, NS))

    n_tiles_max = (TOP_K * (NP + NS)) // MOE_TM + E

    row = lambda a: a.reshape(1, -1)
    xp = x_prompt
    xs = x_sample.reshape(NS, D)
    pp = p_prompt.reshape(depth, NP, -1)
    ps = p_sample.reshape(depth, NS, -1)
    conv_p, ret_p, conv_s = [], [], []
    sorted_x = None
    ret_s = jnp.zeros(state_ret.shape, F32)
    w_in_b = _to_bf16(w_in, 512)
    for i in range(depth):
        w_co_b = w_conv_out[i].astype(BF16)
        w_ro_b = w_ret_out[i].astype(BF16)
        w_o_b = w_o[i].astype(BF16)
        g_mix = row(norm_mix_g[i])

        xp, cst, rst = _mixer_prompt(xp, cos_p, sin_p, g_mix, w_in_b, i, conv_w[i], w_co_b,
                                     w_ro_b, w_o_b, H=H, DK=DK, DV=DV)
        conv_p.append(cst[:, 8 - (CONV_W - 1):, :])
        ret_p.append(rst)

        n_main = 3 * DC + 4 * DR
        w_qkt = w_in_b[i, :, 3 * DC:3 * DC + 2 * DR].T
        zm, zg, qkt = _dec_inproj(xs, g_mix, w_in_b, i, w_qkt, inv_t, n_main, H=H, DK=DK)
        qkt3 = qkt.reshape(2 * DR, NS // DEC_BB, DEC_BB).transpose(1, 0, 2)
        ain, rin, ncv, ret_s = _dec_state(zm, qkt3, state_conv[i].reshape(NS, -1), conv_w[i],
                                          state_ret, ret_s, i, H=H, DK=DK, DV=DV)
        xs = _dec_outproj(xs, zg, ain, rin, w_co_b, w_ro_b, w_o_b)
        conv_s.append(ncv.reshape(NS, CONV_W - 1, DC))

        g_ffn = row(norm_ffn_g[i])
        xp2 = xp.reshape(NP, D)
        if i % 2 == 0:
            xp2 = _ffn_dense(xp2, g_ffn, w_dense_gu, w_dense_down, i // 2)
            xs = _ffn_dense(xs, g_ffn, w_dense_gu, w_dense_down, i // 2)
            moe_p = moe_s = None
        else:
            w_rt = w_router[i // 2].T
            hn_p, route_p, cnt_p = _router(xp2, g_ffn, w_rt, ROUTE_TILE)
            hn_s, route_s, cnt_s = _router(xs, g_ffn, w_rt, NS)
            counts = jnp.concatenate([cnt_p[:, :, 0], cnt_s[:, :, 0]], axis=0)
            base, tile_e, valid, n_used = _route_plan(counts, MOE_TM, n_tiles_max)
            ntp = NP // ROUTE_TILE
            slots_p = _global_slots(route_p, base[:ntp], ROUTE_TILE)
            slots_s = _global_slots(route_s, base[ntp:], NS)
            if sorted_x is None:
                sorted_x = jnp.zeros((n_tiles_max * MOE_TM, D), F32)
            sorted_x = _dispatch(slots_p, hn_p, sorted_x)
            sorted_x = _dispatch(slots_s, hn_s, sorted_x)
            sorted_y = _experts(tile_e, valid, n_used, sorted_x, w_exp_gu, w_exp_down, i // 2)
            moe_p = (slots_p, route_p[4:6].T, sorted_y)
            moe_s = (slots_s, route_s[4:6].T, sorted_y)

        g_fin = row(norm_final_g) if i == depth - 1 else None
        w_pg_b = w_ple_gate[i].astype(BF16)
        w_pl_b = w_ple[i].astype(BF16)
        xp = _ple(xp2, moe_p, pp, i, row(norm_ple_g[i]), w_pg_b, w_pl_b, g_fin).reshape(B, T, D)
        xs = _ple(xs, moe_s, ps, i, row(norm_ple_g[i]), w_pg_b, w_pl_b, g_fin)

    return (xp, xs.reshape(NS, 1, D), jnp.stack(conv_p), jnp.stack(ret_p),
            jnp.stack(conv_s), ret_s)
```

```python
import functools
import math

import jax
import jax.numpy as jnp
from jax import lax
from jax.experimental import pallas as pl
from jax.experimental.pallas import tpu as pltpu

F32 = jnp.float32
BF16 = jnp.bfloat16

CONV_W = 3
TOP_K = 2
ROPE_BASE = 10000.0
EPS = 1e-6
PAST_LEN = 16384

V7X_VMEM_BYTES = 64 * 1024 * 1024
VMEM_LIMIT = V7X_VMEM_BYTES - 8 * 1024 * 1024

MIX_CHUNK = 1024
RET_SUB = 256
FFN_TM = 1024
FFN_FC = 512
ROUTE_TILE = 2048
MOE_TM = 1024
MOE_ROWS = 256
MOE_SUB = 1024
MOE_FC = 512
MOE_TD = 1024
PLE_TM = 512
DEC_BB = 16


def _cparams(sem):
    return pltpu.CompilerParams(dimension_semantics=sem, vmem_limit_bytes=VMEM_LIMIT)


def _dot(a, b):
    return jnp.dot(a, b, preferred_element_type=F32)


def _dot_nt(a, b, precision=None):
    return lax.dot_general(a, b, (((1,), (1,)), ((), ())), precision=precision,
                           preferred_element_type=F32)


def _dot_tn(a, b):
    return lax.dot_general(a, b, (((0,), (0,)), ((), ())), preferred_element_type=F32)


def _rms(x, g):
    return x * lax.rsqrt(jnp.mean(x * x, axis=-1, keepdims=True) + EPS) * g


def _sigmoid(x):
    return 1.0 / (1.0 + jnp.exp(-x))


def _silu(x):
    return x * _sigmoid(x)


def _log_gammas(n_heads):
    return tuple(math.log(1.0 - 2.0 ** (-5.0 - h)) for h in range(n_heads))


def _cast_kernel(w_ref, o_ref):
    o_ref[...] = w_ref[...].astype(o_ref.dtype)


def _to_bf16(w, bn):
    depth, K, N = w.shape
    spec = pl.BlockSpec((None, K, bn), lambda l, j: (l, 0, j))
    return pl.pallas_call(
        _cast_kernel,
        grid=(depth, N // bn),
        in_specs=[spec],
        out_specs=spec,
        out_shape=jax.ShapeDtypeStruct(w.shape, BF16),
        compiler_params=_cparams(("arbitrary", "arbitrary")),
        name="cast_bf16",
    )(w)


def _rope_table_kernel(pos_ref, inv_ref, sgn_ref, cos_ref, sin_ref):
    ang = pos_ref[...] * inv_ref[...]
    cos_ref[...] = jnp.cos(ang)
    sin_ref[...] = jnp.sin(ang) * sgn_ref[...]


def _rope_tables(pos, inv_dup, sgn):
    n = pos.shape[0]
    dk = inv_dup.shape[1]
    return pl.pallas_call(
        _rope_table_kernel,
        out_shape=(jax.ShapeDtypeStruct((n, dk), F32), jax.ShapeDtypeStruct((n, dk), F32)),
        name="rope_tables",
    )(pos, inv_dup, sgn)


def _mixer_prompt_kernel(x_ref, cos_ref, sin_ref, g_ref, win_ref, cw_ref, wco_ref, wro_ref,
                         wo_ref, out_ref, cst_ref, rst_ref, tail_ref, state_ref, decay_ref,
                         *, LT, L, H, DK, DV, DC, D, log_gamma):
    c = pl.program_id(1)
    DR = H * DK

    @pl.when(c == 0)
    def _init():
        tail_ref[...] = jnp.zeros_like(tail_ref)
        state_ref[...] = jnp.zeros_like(state_ref)
        diff = (lax.broadcasted_iota(jnp.int32, (L, L), 0)
                - lax.broadcasted_iota(jnp.int32, (L, L), 1)).astype(F32)
        for h in range(H):
            decay_ref[h] = jnp.where(diff >= 0.0,
                                     jnp.exp(jnp.maximum(diff, 0.0) * log_gamma[h]), 0.0)

    x = x_ref[...]
    xn = _rms(x, g_ref[...]).astype(BF16)

    zc = _dot(xn, win_ref[:, 0:3 * DC])
    cb, u = zc[:, 0:DC], zc[:, DC:2 * DC] * zc[:, 2 * DC:3 * DC]
    tail = tail_ref[...]
    p1, p2 = tail[7:8, :], tail[6:7, :]
    row = lax.broadcasted_iota(jnp.int32, (LT, DC), 0)
    u1 = jnp.where(row == 0, p1, pltpu.roll(u, 1, axis=0))
    u2 = jnp.where(row == 0, p2, jnp.where(row == 1, p1, pltpu.roll(u, 2, axis=0)))
    cw = cw_ref[...]
    conv = cw[0:1, :] * u2 + cw[1:2, :] * u1 + cw[2:3, :] * u
    tail_ref[...] = u[LT - 8:LT, :]
    a = _dot((cb * conv).astype(BF16), wco_ref[...])

    zr = _dot(xn, win_ref[:, 3 * DC:3 * DC + 4 * DR])
    ridx_k = lax.broadcasted_iota(jnp.int32, (L, DK), 0).astype(F32)
    ridx_v = lax.broadcasted_iota(jnp.int32, (L, DV), 0).astype(F32)
    subs = []
    for s in range(LT // L):
        rows = slice(s * L, (s + 1) * L)
        cos, sin = cos_ref[rows, :], sin_ref[rows, :]
        heads = []
        for h in range(H):
            lg = log_gamma[h]
            qh = zr[rows, h * DK:(h + 1) * DK]
            kh = zr[rows, DR + h * DK:DR + (h + 1) * DK]
            vh = zr[rows, 2 * DR + h * DV:2 * DR + (h + 1) * DV]
            gh = zr[rows, 3 * DR + h * DV:3 * DR + (h + 1) * DV]
            qh = qh * cos + pltpu.roll(qh, DK // 2, axis=1) * sin
            kh = (kh * cos + pltpu.roll(kh, DK // 2, axis=1) * sin) * (DK ** -0.5)
            qb, vb = qh.astype(BF16), vh.astype(BF16)
            s_prev = state_ref[h]
            scores = _dot_nt(qb, kh.astype(BF16)) * decay_ref[h]
            o = _dot(scores.astype(BF16), vb)
            o = o + _dot(qb, s_prev.astype(BF16)) * jnp.exp((ridx_v + 1.0) * lg)
            k_dec = jnp.exp((L - 1.0 - ridx_k) * lg)
            state_ref[h] = math.exp(L * lg) * s_prev + _dot_tn((kh * k_dec).astype(BF16), vb)
            mu = jnp.mean(o, axis=-1, keepdims=True)
            d = o - mu
            var = jnp.mean(d * d, axis=-1, keepdims=True)
            heads.append(_silu(gh) * (d * lax.rsqrt(var + EPS)))
        subs.append(jnp.concatenate(heads, axis=-1).astype(BF16))
    r = _dot(jnp.concatenate(subs, axis=0), wro_ref[...])

    zg = _dot(xn, win_ref[:, 3 * DC + 4 * DR:3 * DC + 4 * DR + 2 * D])
    m = _sigmoid(zg[:, 0:D]) * a + _sigmoid(zg[:, D:2 * D]) * r
    out_ref[...] = x + _dot(m.astype(BF16), wo_ref[...])

    @pl.when(c == pl.num_programs(1) - 1)
    def _final():
        cst_ref[...] = u[LT - 8:LT, :]
        rst_ref[...] = state_ref[...]


def _mixer_prompt(x, cos, sin, g, w_in_all, layer, cw, w_co, w_ro, w_o, *, H, DK, DV):
    B, T, D = x.shape
    DC = cw.shape[1]
    LT, L = MIX_CHUNK, RET_SUB
    const = lambda *shape: pl.BlockSpec(shape, lambda b, c: (0,) * len(shape))
    kern = functools.partial(_mixer_prompt_kernel, LT=LT, L=L, H=H, DK=DK, DV=DV, DC=DC, D=D,
                             log_gamma=_log_gammas(H))
    return pl.pallas_call(
        kern,
        grid=(B, T // LT),
        in_specs=[
            pl.BlockSpec((None, LT, D), lambda b, c: (b, c, 0)),
            pl.BlockSpec((LT, DK), lambda b, c: (c, 0)),
            pl.BlockSpec((LT, DK), lambda b, c: (c, 0)),
            const(1, D),
            pl.BlockSpec((None,) + w_in_all.shape[1:], lambda b, c: (layer, 0, 0)),
            const(*cw.shape), const(*w_co.shape),
            const(*w_ro.shape), const(*w_o.shape),
        ],
        out_specs=[
            pl.BlockSpec((None, LT, D), lambda b, c: (b, c, 0)),
            pl.BlockSpec((None, 8, DC), lambda b, c: (b, 0, 0)),
            pl.BlockSpec((None, H, DK, DV), lambda b, c: (b, 0, 0, 0)),
        ],
        out_shape=(jax.ShapeDtypeStruct((B, T, D), F32),
                   jax.ShapeDtypeStruct((B, 8, DC), F32),
                   jax.ShapeDtypeStruct((B, H, DK, DV), F32)),
        scratch_shapes=[pltpu.VMEM((8, DC), F32), pltpu.VMEM((H, DK, DV), F32),
                        pltpu.VMEM((H, L, L), F32)],
        compiler_params=_cparams(("arbitrary", "arbitrary")),
        name="mixer_prompt",
    )(x, cos, sin, g, w_in_all, cw, w_co, w_ro, w_o)


def _dec_inproj_kernel(x_ref, g_ref, win_ref, wqkt_ref, invt_ref, zm_ref, zg_ref, qkt_ref,
                       *, H, DK, n_main, pos0):
    xn = _rms(x_ref[...], g_ref[...]).astype(BF16)
    zm_ref[...] = _dot(xn, win_ref[:, 0:n_main])
    zg_ref[...] = _dot(xn, win_ref[:, n_main:win_ref.shape[1]])
    qkt = _dot_nt(wqkt_ref[...], xn)
    ang = pos0 * invt_ref[...]
    cos, sin = jnp.cos(ang), jnp.sin(ang)
    half = DK // 2
    for hh in range(2 * H):
        blk = qkt[hh * DK:(hh + 1) * DK, :]
        x1, x2 = blk[0:half, :], blk[half:DK, :]
        scale = 1.0 if hh < H else DK ** -0.5
        qkt_ref[hh * DK:hh * DK + half, :] = (x1 * cos[0:half] - x2 * sin[0:half]) * scale
        qkt_ref[hh * DK + half:(hh + 1) * DK, :] = (x2 * cos[half:DK] + x1 * sin[half:DK]) * scale


def _dec_inproj(x, g, w_in_all, layer, w_qkt, inv_t, n_main, *, H, DK):
    n, D = x.shape
    n_in = w_in_all.shape[2]
    kern = functools.partial(_dec_inproj_kernel, H=H, DK=DK, n_main=n_main,
                             pos0=float(PAST_LEN))
    whole = lambda a: pl.BlockSpec(a.shape, lambda i: (0,) * a.ndim)
    out_shape = (jax.ShapeDtypeStruct((n, n_main), F32),
                 jax.ShapeDtypeStruct((n, n_in - n_main), F32),
                 jax.ShapeDtypeStruct((w_qkt.shape[0], n), F32))
    return pl.pallas_call(
        kern,
        grid=(1,),
        in_specs=[whole(x), whole(g), pl.BlockSpec((None, D, n_in), lambda i: (layer, 0, 0)),
                  whole(w_qkt), whole(inv_t)],
        out_specs=[whole(s) for s in out_shape],
        out_shape=out_shape,
        compiler_params=_cparams(("arbitrary",)),
        name="dec_inproj",
    )(x, g, w_in_all, w_qkt, inv_t)


def _dec_state_kernel(zm_ref, qkt_ref, conv_ref, cw_ref, s_ref, *rest, BB, H, DK, DV, DC,
                      log_gamma):
    ain_ref, rin_ref, nconv_ref, ns_ref, o_scr = rest[-5:]
    DR = H * DK
    zm = zm_ref[...]
    cb, u = zm[:, 0:DC], zm[:, DC:2 * DC] * zm[:, 2 * DC:3 * DC]
    buf = conv_ref[...]
    b0, b1 = buf[:, 0:DC], buf[:, DC:2 * DC]
    cw = cw_ref[...]
    conv = cw[0:1, :] * b0 + cw[1:2, :] * b1 + cw[2:3, :] * u
    nconv_ref[:, 0:DC] = b1
    nconv_ref[:, DC:2 * DC] = u
    ain_ref[...] = cb * conv

    qkt = qkt_ref[...]
    v = zm[:, 3 * DC + 2 * DR:3 * DC + 3 * DR]
    g = zm[:, 3 * DC + 3 * DR:3 * DC + 4 * DR]
    for j in range(BB):
        for h in range(H):
            gamma = math.exp(log_gamma[h])
            qc = qkt[h * DK:(h + 1) * DK, j:j + 1]
            kc = qkt[DR + h * DK:DR + (h + 1) * DK, j:j + 1]
            s_prev = s_ref[j, h]
            vrow = v[j:j + 1, h * DV:(h + 1) * DV]
            qk = jnp.sum(qc * kc, axis=0, keepdims=True)
            inter = jnp.sum(qc * s_prev, axis=0, keepdims=True) * gamma
            o_scr[j:j + 1, h * DV:(h + 1) * DV] = qk * vrow + inter
            ns_ref[j, h] = gamma * s_prev + kc * vrow
    o = o_scr[...]
    for h in range(H):
        oh = o[:, h * DV:(h + 1) * DV]
        mu = jnp.mean(oh, axis=-1, keepdims=True)
        d = oh - mu
        var = jnp.mean(d * d, axis=-1, keepdims=True)
        rin_ref[:, h * DV:(h + 1) * DV] = _silu(g[:, h * DV:(h + 1) * DV]) * (d * lax.rsqrt(var + EPS))


def _dec_state(zm, qkt3, conv2d, cw, state_all, new_states, layer, *, H, DK, DV):
    n = zm.shape[0]
    depth = state_all.shape[0]
    DC = cw.shape[1]
    BB = DEC_BB
    kern = functools.partial(_dec_state_kernel, BB=BB, H=H, DK=DK, DV=DV, DC=DC,
                             log_gamma=_log_gammas(H))
    args = [zm, qkt3, conv2d, cw, state_all, new_states]
    in_specs = [
        pl.BlockSpec((BB, zm.shape[1]), lambda b: (b, 0)),
        pl.BlockSpec((None, qkt3.shape[1], BB), lambda b: (b, 0, 0)),
        pl.BlockSpec((BB, 2 * DC), lambda b: (b, 0)),
        pl.BlockSpec(cw.shape, lambda b: (0, 0)),
        pl.BlockSpec((None, BB, H, DK, DV), lambda b: (layer, b, 0, 0, 0)),
        pl.BlockSpec(memory_space=pl.ANY),
    ]
    aliases = {5: 3}
    return pl.pallas_call(
        kern,
        grid=(n // BB,),
        in_specs=in_specs,
        out_specs=[
            pl.BlockSpec((BB, DC), lambda b: (b, 0)),
            pl.BlockSpec((BB, H * DV), lambda b: (b, 0)),
            pl.BlockSpec((BB, 2 * DC), lambda b: (b, 0)),
            pl.BlockSpec((None, BB, H, DK, DV), lambda b: (layer, b, 0, 0, 0)),
        ],
        out_shape=(jax.ShapeDtypeStruct((n, DC), F32),
                   jax.ShapeDtypeStruct((n, H * DV), F32),
                   jax.ShapeDtypeStruct((n, 2 * DC), F32),
                   jax.ShapeDtypeStruct((depth, n, H, DK, DV), F32)),
        scratch_shapes=[pltpu.VMEM((BB, H * DV), F32)],
        input_output_aliases=aliases,
        compiler_params=_cparams(("arbitrary",)),
        name="dec_state",
    )(*args)


def _dec_outproj_kernel(x_ref, zg_ref, ain_ref, rin_ref, wco_ref, wro_ref, wo_ref, out_ref, *, D):
    a = _dot(ain_ref[...].astype(BF16), wco_ref[...])
    r = _dot(rin_ref[...].astype(BF16), wro_ref[...])
    zg = zg_ref[...]
    m = _sigmoid(zg[:, 0:D]) * a + _sigmoid(zg[:, D:2 * D]) * r
    out_ref[...] = x_ref[...] + _dot(m.astype(BF16), wo_ref[...])


def _dec_outproj(x, zg, ain, rin, w_co, w_ro, w_o):
    n, D = x.shape
    return pl.pallas_call(
        functools.partial(_dec_outproj_kernel, D=D),
        out_shape=jax.ShapeDtypeStruct((n, D), F32),
        compiler_params=pltpu.CompilerParams(vmem_limit_bytes=VMEM_LIMIT),
        name="dec_outproj",
    )(x, zg, ain, rin, w_co, w_ro, w_o)


def _ffn_dense_kernel(x_ref, g_ref, wg_ref, wu_ref, wd_ref, out_ref, hn_ref):
    @pl.when(pl.program_id(1) == 0)
    def _init():
        x = x_ref[...]
        hn_ref[...] = _rms(x, g_ref[...]).astype(BF16)
        out_ref[...] = x

    hn = hn_ref[...]
    act = (_silu(_dot(hn, wg_ref[...].astype(BF16)))
           * _dot(hn, wu_ref[...].astype(BF16))).astype(BF16)
    out_ref[...] += _dot(act, wd_ref[...].astype(BF16))


def _ffn_dense(x, g, w_gu_all, w_down_all, layer):
    n, D = x.shape
    FF = w_down_all.shape[1]
    tm = min(FFN_TM, n)
    fc = FFN_FC
    nf = FF // fc
    return pl.pallas_call(
        _ffn_dense_kernel,
        grid=(n // tm, nf),
        in_specs=[
            pl.BlockSpec((tm, D), lambda i, j: (i, 0)),
            pl.BlockSpec((1, D), lambda i, j: (0, 0)),
            pl.BlockSpec((None, D, fc), lambda i, j: (layer, 0, j)),
            pl.BlockSpec((None, D, fc), lambda i, j: (layer, 0, nf + j)),
            pl.BlockSpec((None, fc, D), lambda i, j: (layer, j, 0)),
        ],
        out_specs=pl.BlockSpec((tm, D), lambda i, j: (i, 0)),
        out_shape=jax.ShapeDtypeStruct((n, D), F32),
        scratch_shapes=[pltpu.VMEM((tm, D), BF16)],
        compiler_params=_cparams(("arbitrary", "arbitrary")),
        name="ffn_dense",
    )(x, g, w_gu_all, w_gu_all, w_down_all)


def _router_kernel(x_ref, g_ref, wrt_ref, hn_ref, route_ref, cnt_ref, *, T, E, CH):
    hn = _rms(x_ref[...], g_ref[...])
    hn_ref[...] = hn
    logits = _dot_nt(wrt_ref[...], hn, precision=lax.Precision.HIGHEST)
    eidx = lax.broadcasted_iota(jnp.int32, (E, T), 0)
    m1 = jnp.max(logits, axis=0, keepdims=True)
    i1 = jnp.min(jnp.where(logits == m1, eidx, E), axis=0, keepdims=True)
    sel1 = eidx == i1
    rest = jnp.where(sel1, -jnp.inf, logits)
    m2 = jnp.max(rest, axis=0, keepdims=True)
    i2 = jnp.min(jnp.where(rest == m2, eidx, E), axis=0, keepdims=True)
    sel2 = eidx == i2
    e2 = jnp.exp(m2 - m1)
    w1 = 1.0 / (1.0 + e2)
    f1 = jnp.where(sel1, 1.0, 0.0)
    f2 = jnp.where(sel2, 1.0, 0.0)
    sel = f1 + f2
    selb = sel.astype(BF16)
    route_ref[0:1, :] = i1.astype(F32)
    route_ref[1:2, :] = i2.astype(F32)
    route_ref[4:5, :] = w1
    route_ref[5:6, :] = e2 * w1
    route_ref[6:8, :] = jnp.zeros((2, T), F32)
    src = lax.broadcasted_iota(jnp.int32, (CH, CH), 0)
    dst = lax.broadcasted_iota(jnp.int32, (CH, CH), 1)
    tri = jnp.where(src < dst, 1.0, 0.0).astype(BF16)
    cnt = jnp.zeros((E, 1), F32)
    for cidx in range(T // CH):
        cols = slice(cidx * CH, (cidx + 1) * CH)
        pos = _dot(selb[:, cols], tri) + cnt
        route_ref[2:3, cols] = jnp.sum(pos * f1[:, cols], axis=0, keepdims=True)
        route_ref[3:4, cols] = jnp.sum(pos * f2[:, cols], axis=0, keepdims=True)
        cnt = cnt + jnp.sum(sel[:, cols], axis=1, keepdims=True)
    cnt_ref[...] = jnp.broadcast_to(cnt, cnt_ref.shape).astype(jnp.int32)


def _router(x, g, w_router_t, T):
    n, D = x.shape
    E = w_router_t.shape[0]
    nt = n // T
    kern = functools.partial(_router_kernel, T=T, E=E, CH=min(256, T))
    return pl.pallas_call(
        kern,
        grid=(nt,),
        in_specs=[
            pl.BlockSpec((T, D), lambda i: (i, 0)),
            pl.BlockSpec((1, D), lambda i: (0, 0)),
            pl.BlockSpec((E, D), lambda i: (0, 0)),
        ],
        out_specs=[
            pl.BlockSpec((T, D), lambda i: (i, 0)),
            pl.BlockSpec((8, T), lambda i: (0, i)),
            pl.BlockSpec((None, E, 128), lambda i: (i, 0, 0)),
        ],
        out_shape=(jax.ShapeDtypeStruct((n, D), F32),
                   jax.ShapeDtypeStruct((8, n), F32),
                   jax.ShapeDtypeStruct((nt, E, 128), jnp.int32)),
        compiler_params=_cparams(("arbitrary",)),
        name="moe_router",
    )(x, g, w_router_t)


def _route_plan(counts, tm, n_tiles_max):
    E = counts.shape[1]
    tot = jnp.sum(counts, axis=0)
    tiles_e = (tot + (tm - 1)) // tm
    tile_end = jnp.cumsum(tiles_e)
    tile_start = tile_end - tiles_e
    base = (tile_start * tm)[None, :] + jnp.cumsum(counts, axis=0) - counts
    r = jnp.arange(n_tiles_max, dtype=jnp.int32)
    tile_e = jnp.minimum(jnp.sum(r[:, None] >= tile_end[None, :], axis=1), E - 1).astype(jnp.int32)
    n_used = tile_end[E - 1:E].astype(jnp.int32)
    valid = jnp.clip(tot[tile_e] - (r - tile_start[tile_e]) * tm, 0, tm)
    valid = jnp.where(r < n_used[0], valid, 0).astype(jnp.int32)
    return base.astype(jnp.int32), tile_e, valid, n_used


def _global_slots(route, base, T):
    nt, E = base.shape
    idx = route[0:TOP_K].astype(jnp.int32).reshape(TOP_K, nt, T)
    slot = route[TOP_K:2 * TOP_K].astype(jnp.int32).reshape(TOP_K, nt, T)
    onehot = idx[..., None] == jnp.arange(E, dtype=jnp.int32)
    start = jnp.sum(jnp.where(onehot, base[None, :, None, :], 0), axis=-1)
    return (start + slot).reshape(TOP_K, nt * T).T.reshape(-1)


SUBLANES = 8


def _for_each_row(n_rows, slots_ref, fn):
    def body(q, carry):
        first = q * (SUBLANES * TOP_K)
        for u in range(SUBLANES):
            fn(q, u, [slots_ref[first + (u * TOP_K + k)] for k in range(TOP_K)])
        return carry
    lax.fori_loop(0, n_rows // SUBLANES, body, 0)


def _dispatch_kernel(g_ref, hn_ref, xs_in_ref, xs_ref, sem, *, TD):
    del xs_in_ref

    def send(q, u, slots):
        for k, dst in enumerate(slots):
            pltpu.make_async_copy(hn_ref.at[q, pl.ds(u, 1)], xs_ref.at[pl.ds(dst, 1)],
                                  sem).start(priority=k % 2)
    _for_each_row(TD, g_ref, send)
    for k in range(TOP_K):
        pltpu.make_async_copy(hn_ref, hn_ref, sem).wait()


def _dispatch(g, hn, xs):
    n, D = hn.shape
    TD = min(MOE_TD, n)
    return pl.pallas_call(
        functools.partial(_dispatch_kernel, TD=TD),
        grid=(n // TD,),
        in_specs=[
            pl.BlockSpec((TOP_K * TD,), lambda i: (i,), memory_space=pltpu.SMEM),
            pl.BlockSpec((TD // SUBLANES, SUBLANES, D), lambda i: (i, 0, 0)),
            pl.BlockSpec(memory_space=pl.ANY),
        ],
        out_specs=pl.BlockSpec(memory_space=pl.ANY),
        out_shape=jax.ShapeDtypeStruct(xs.shape, xs.dtype),
        scratch_shapes=[pltpu.SemaphoreType.DMA(())],
        input_output_aliases={2: 0},
        compiler_params=_cparams(("arbitrary",)),
        name="moe_dispatch",
    )(g, hn.reshape(n // SUBLANES, SUBLANES, D), xs)


def _experts_kernel(te_ref, valid_ref, nused_ref, x_ref, wg_ref, wu_ref, wd_ref, y_ref, xb_ref,
                    *, TM, R, SUB):
    r, j = pl.program_id(0), pl.program_id(1)
    valid = jnp.where(r < nused_ref[0], valid_ref[r], 0)
    nch = lax.shift_right_logical(valid + (R - 1), R.bit_length() - 1)

    @pl.when((nch == 0) & (j == 0))
    def _idle():
        y_ref[...] = jnp.zeros_like(y_ref)

    def variant(rows):
        def run(first):
            wg, wu, wd = (w[...].astype(BF16) for w in (wg_ref, wu_ref, wd_ref))
            for lo in range(0, rows, SUB):
                hi = min(lo + SUB, rows)
                if first:
                    xb_ref[lo:hi, :] = x_ref[lo:hi, :].astype(BF16)
                xs = xb_ref[lo:hi, :]
                y = _dot((_silu(_dot(xs, wg)) * _dot(xs, wu)).astype(BF16), wd)
                if first:
                    y_ref[lo:hi, :] = y
                else:
                    y_ref[lo:hi, :] += y
            if first and rows < TM:
                y_ref[rows:TM, :] = jnp.zeros((TM - rows, y_ref.shape[1]), F32)

        @pl.when((nch == rows // R) & (j == 0))
        def _first():
            run(True)

        @pl.when((nch == rows // R) & (j > 0))
        def _rest():
            run(False)

    for n in range(1, TM // R + 1):
        variant(n * R)


def _experts(tile_e, valid, n_used, xs, w_gu_all, w_down_all, layer):
    rows, D = xs.shape
    FF = w_down_all.shape[2]
    TM, R, fc = MOE_TM, MOE_ROWS, MOE_FC
    nf = FF // fc
    nt = rows // TM

    def tile_of(r, nu):
        return jnp.maximum(jnp.minimum(r, nu[0] - 1), 0)

    def chunk_of(r, j, nu):
        return jnp.where(r < nu[0], j, nf - 1)

    grid_spec = pltpu.PrefetchScalarGridSpec(
        num_scalar_prefetch=3,
        grid=(nt, nf),
        in_specs=[
            pl.BlockSpec((TM, D), lambda r, j, te, vl, nu: (tile_of(r, nu), 0)),
            pl.BlockSpec((None, None, D, fc),
                         lambda r, j, te, vl, nu: (layer, te[tile_of(r, nu)], 0, chunk_of(r, j, nu))),
            pl.BlockSpec((None, None, D, fc),
                         lambda r, j, te, vl, nu: (layer, te[tile_of(r, nu)], 0,
                                                   nf + chunk_of(r, j, nu))),
            pl.BlockSpec((None, None, fc, D),
                         lambda r, j, te, vl, nu: (layer, te[tile_of(r, nu)], chunk_of(r, j, nu), 0)),
        ],
        out_specs=pl.BlockSpec((TM, D), lambda r, j, te, vl, nu: (r, 0)),
        scratch_shapes=[pltpu.VMEM((TM, D), BF16)],
    )
    return pl.pallas_call(
        functools.partial(_experts_kernel, TM=TM, R=R, SUB=MOE_SUB),
        grid_spec=grid_spec,
        out_shape=jax.ShapeDtypeStruct((rows, D), F32),
        compiler_params=_cparams(("arbitrary", "arbitrary")),
        name="moe_experts",
    )(tile_e, valid, n_used, xs, w_gu_all, w_gu_all, w_down_all)


def _ple_kernel(*refs, moe, final, tm):
    refs = list(refs)
    if moe:
        g_ref, gnext_ref, x_ref, w_ref, ys_ref = refs[0:5]
        refs = refs[5:]
        buf_ref, sem = refs[-2:]
        refs = refs[:-2]
    else:
        x_ref = refs.pop(0)
    p_ref, gn_ref, wgate_ref, wple_ref = refs[0:4]
    gfin_ref = refs[4] if final else None
    out_ref = refs[-1]

    if moe:
        i = pl.program_id(0)
        slot = lax.rem(i, 2)

        def fetch(slots_ref, s):
            def get(q, u, slots):
                for k, src in enumerate(slots):
                    pltpu.make_async_copy(ys_ref.at[pl.ds(src, 1)],
                                          buf_ref.at[s, k, q, pl.ds(u, 1)],
                                          sem.at[s]).start(priority=k % 2)
            _for_each_row(tm, slots_ref, get)

        @pl.when(i == 0)
        def _first():
            fetch(g_ref, 0)

        @pl.when(i + 1 < pl.num_programs(0))
        def _ahead():
            fetch(gnext_ref, 1 - slot)

        for k in range(TOP_K):
            pltpu.make_async_copy(buf_ref.at[slot, k], buf_ref.at[slot, k], sem.at[slot]).wait()

    n_parts = 2 if tm % (2 * SUBLANES) == 0 and tm >= 256 else 1
    rows_per = tm // n_parts
    for part in range(n_parts):
        rows = slice(part * rows_per, (part + 1) * rows_per)
        x = x_ref[rows, :]
        if moe:
            groups = slice(part * rows_per // SUBLANES, (part + 1) * rows_per // SUBLANES)
            w = w_ref[rows, :]
            for k in range(TOP_K):
                x = x + w[:, k:k + 1] * buf_ref[slot, k, groups].reshape(x.shape)
        gate = _sigmoid(_dot(_rms(x, gn_ref[...]).astype(BF16), wgate_ref[...]))
        y = x + gate * _dot(p_ref[rows, :].astype(BF16), wple_ref[...])
        if final:
            y = _rms(y, gfin_ref[...])
        out_ref[rows, :] = y


def _ple(x, moe, p_all, layer, g, w_gate, w_ple, g_final):
    n, D = x.shape
    DP = p_all.shape[-1]
    tm = min(PLE_TM, n)
    tok = pl.BlockSpec((tm, D), lambda i: (i, 0))
    const = lambda *shape: pl.BlockSpec(shape, lambda i: (0,) * len(shape))
    args, specs, scratch = [], [], []
    if moe is not None:
        slots, weights, ys = moe
        last = n // tm - 1
        args += [slots, slots, x, weights, ys]
        specs += [pl.BlockSpec((TOP_K * tm,), lambda i: (i,), memory_space=pltpu.SMEM),
                  pl.BlockSpec((TOP_K * tm,), lambda i: (jnp.minimum(i + 1, last),),
                               memory_space=pltpu.SMEM),
                  tok, pl.BlockSpec((tm, TOP_K), lambda i: (i, 0)),
                  pl.BlockSpec(memory_space=pl.ANY)]
        scratch = [pltpu.VMEM((2, TOP_K, tm // SUBLANES, SUBLANES, D), F32),
                   pltpu.SemaphoreType.DMA((2,))]
    else:
        args.append(x)
        specs.append(tok)
    args += [p_all, g, w_gate, w_ple]
    specs += [pl.BlockSpec((None, tm, DP), lambda i: (layer, i, 0)), const(1, D),
              const(D, D), const(DP, D)]
    if g_final is not None:
        args.append(g_final)
        specs.append(const(1, D))
    kern = functools.partial(_ple_kernel, moe=moe is not None, final=g_final is not None, tm=tm)
    return pl.pallas_call(
        kern,
        grid=(n // tm,),
        in_specs=specs,
        out_specs=tok,
        out_shape=jax.ShapeDtypeStruct((n, D), F32),
        scratch_shapes=scratch,
        compiler_params=_cparams(("arbitrary",)),
        name="ple_moe" if moe is not None else "ple",
    )(*args)


def kernel(x_prompt, x_sample, state_conv, state_ret, p_prompt, p_sample, norm_mix_g, w_in,
           conv_w, w_conv_out, w_ret_out, w_o, norm_ffn_g, w_dense_gu, w_dense_down, w_router,
           w_exp_gu, w_exp_down, norm_ple_g, w_ple, w_ple_gate, norm_final_g):
    B, T, D = x_prompt.shape
    NP = B * T
    NS = x_sample.shape[0]
    depth = w_in.shape[0]
    _, _, H, DK, DV = state_ret.shape
    DC = conv_w.shape[-1]
    DR = H * DK
    E = w_router.shape[-1]
    assert x_sample.shape[1] == 1 and CONV_W - 1 == state_conv.shape[2]
    assert T % MIX_CHUNK == 0 and NS % DEC_BB == 0 and NP % ROUTE_TILE == 0

    half = DK // 2
    inv = ROPE_BASE ** (-jnp.arange(half, dtype=F32) / half)
    inv_dup = jnp.concatenate([inv, inv]).reshape(1, DK)
    sgn = jnp.concatenate([-jnp.ones((half,), F32), jnp.ones((half,), F32)]).reshape(1, DK)
    pos = jnp.arange(T, dtype=F32).reshape(T, 1)
    cos_p, sin_p = _rope_tables(pos, inv_dup, sgn)
    inv_t = jnp.broadcast_to(inv_dup.reshape(DK, 1), (DK, NS))

    n_tiles_max = (TOP_K * (NP + NS)) // MOE_TM + E

    row = lambda a: a.reshape(1, -1)
    xp = x_prompt
    xs = x_sample.reshape(NS, D)
    pp = p_prompt.reshape(depth, NP, -1)
    ps = p_sample.reshape(depth, NS, -1)
    conv_p, ret_p, conv_s = [], [], []
    sorted_x = None
    ret_s = jnp.zeros(state_ret.shape, F32)
    w_in_b = _to_bf16(w_in, 512)
    for i in range(depth):
        w_co_b = w_conv_out[i].astype(BF16)
        w_ro_b = w_ret_out[i].astype(BF16)
        w_o_b = w_o[i].astype(BF16)
        g_mix = row(norm_mix_g[i])

        xp, cst, rst = _mixer_prompt(xp, cos_p, sin_p, g_mix, w_in_b, i, conv_w[i], w_co_b,
                                     w_ro_b, w_o_b, H=H, DK=DK, DV=DV)
        conv_p.append(cst[:, 8 - (CONV_W - 1):, :])
        ret_p.append(rst)

        n_main = 3 * DC + 4 * DR
        w_qkt = w_in_b[i, :, 3 * DC:3 * DC + 2 * DR].T
        zm, zg, qkt = _dec_inproj(xs, g_mix, w_in_b, i, w_qkt, inv_t, n_main, H=H, DK=DK)
        qkt3 = qkt.reshape(2 * DR, NS // DEC_BB, DEC_BB).transpose(1, 0, 2)
        ain, rin, ncv, ret_s = _dec_state(zm, qkt3, state_conv[i].reshape(NS, -1), conv_w[i],
                                          state_ret, ret_s, i, H=H, DK=DK, DV=DV)
        xs = _dec_outproj(xs, zg, ain, rin, w_co_b, w_ro_b, w_o_b)
        conv_s.append(ncv.reshape(NS, CONV_W - 1, DC))

        g_ffn = row(norm_ffn_g[i])
        xp2 = xp.reshape(NP, D)
        if i % 2 == 0:
            xp2 = _ffn_dense(xp2, g_ffn, w_dense_gu, w_dense_down, i // 2)
            xs = _ffn_dense(xs, g_ffn, w_dense_gu, w_dense_down, i // 2)
            moe_p = moe_s = None
        else:
            w_rt = w_router[i // 2].T
            hn_p, route_p, cnt_p = _router(xp2, g_ffn, w_rt, ROUTE_TILE)
            hn_s, route_s, cnt_s = _router(xs, g_ffn, w_rt, NS)
            counts = jnp.concatenate([cnt_p[:, :, 0], cnt_s[:, :, 0]], axis=0)
            base, tile_e, valid, n_used = _route_plan(counts, MOE_TM, n_tiles_max)
            ntp = NP // ROUTE_TILE
            slots_p = _global_slots(route_p, base[:ntp], ROUTE_TILE)
            slots_s = _global_slots(route_s, base[ntp:], NS)
            if sorted_x is None:
                sorted_x = jnp.zeros((n_tiles_max * MOE_TM, D), F32)
            sorted_x = _dispatch(slots_p, hn_p, sorted_x)
            sorted_x = _dispatch(slots_s, hn_s, sorted_x)
            sorted_y = _experts(tile_e, valid, n_used, sorted_x, w_exp_gu, w_exp_down, i // 2)
            moe_p = (slots_p, route_p[4:6].T, sorted_y)
            moe_s = (slots_s, route_s[4:6].T, sorted_y)

        g_fin = row(norm_final_g) if i == depth - 1 else None
        w_pg_b = w_ple_gate[i].astype(BF16)
        w_pl_b = w_ple[i].astype(BF16)
        xp = _ple(xp2, moe_p, pp, i, row(norm_ple_g[i]), w_pg_b, w_pl_b, g_fin).reshape(B, T, D)
        xs = _ple(xs, moe_s, ps, i, row(norm_ple_g[i]), w_pg_b, w_pl_b, g_fin)

    return (xp, xs.reshape(NS, 1, D), jnp.stack(conv_p), jnp.stack(ret_p),
            jnp.stack(conv_s), ret_s)
```

```python
import functools
import math

import jax
import jax.numpy as jnp
from jax import lax
from jax.experimental import pallas as pl
from jax.experimental.pallas import tpu as pltpu

F32 = jnp.float32
BF16 = jnp.bfloat16

CONV_W = 3
TOP_K = 2
ROPE_BASE = 10000.0
EPS = 1e-6
PAST_LEN = 16384

V7X_VMEM_BYTES = 64 * 1024 * 1024
VMEM_LIMIT = V7X_VMEM_BYTES - 8 * 1024 * 1024

MIX_CHUNK = 1024
RET_SUB = 256
FFN_TM = 1024
FFN_FC = 512
ROUTE_TILE = 2048
MOE_TM = 1024
MOE_ROWS = 256
MOE_SUB = 1024
MOE_FC = 512
MOE_TD = 2048
PLE_TM = 1024
PLE_PART = 256
DEC_BB = 16


def _cparams(sem):
    return pltpu.CompilerParams(dimension_semantics=sem, vmem_limit_bytes=VMEM_LIMIT)


def _dot(a, b):
    return jnp.dot(a, b, preferred_element_type=F32)


def _dot_nt(a, b, precision=None):
    return lax.dot_general(a, b, (((1,), (1,)), ((), ())), precision=precision,
                           preferred_element_type=F32)


def _dot_tn(a, b):
    return lax.dot_general(a, b, (((0,), (0,)), ((), ())), preferred_element_type=F32)


def _rms(x, g):
    return x * lax.rsqrt(jnp.mean(x * x, axis=-1, keepdims=True) + EPS) * g


def _sigmoid(x):
    return 1.0 / (1.0 + jnp.exp(-x))


def _silu(x):
    return x * _sigmoid(x)


def _log_gammas(n_heads):
    return tuple(math.log(1.0 - 2.0 ** (-5.0 - h)) for h in range(n_heads))


def _cast_kernel(w_ref, o_ref):
    o_ref[...] = w_ref[...].astype(o_ref.dtype)


def _to_bf16(w, bn):
    depth, K, N = w.shape
    spec = pl.BlockSpec((None, K, bn), lambda l, j: (l, 0, j))
    return pl.pallas_call(
        _cast_kernel,
        grid=(depth, N // bn),
        in_specs=[spec],
        out_specs=spec,
        out_shape=jax.ShapeDtypeStruct(w.shape, BF16),
        compiler_params=_cparams(("arbitrary", "arbitrary")),
        name="cast_bf16",
    )(w)


def _rope_table_kernel(pos_ref, inv_ref, sgn_ref, cos_ref, sin_ref):
    ang = pos_ref[...] * inv_ref[...]
    cos_ref[...] = jnp.cos(ang)
    sin_ref[...] = jnp.sin(ang) * sgn_ref[...]


def _rope_tables(pos, inv_dup, sgn):
    n = pos.shape[0]
    dk = inv_dup.shape[1]
    return pl.pallas_call(
        _rope_table_kernel,
        out_shape=(jax.ShapeDtypeStruct((n, dk), F32), jax.ShapeDtypeStruct((n, dk), F32)),
        name="rope_tables",
    )(pos, inv_dup, sgn)


def _mixer_prompt_kernel(x_ref, cos_ref, sin_ref, g_ref, win_ref, cw_ref, wco_ref, wro_ref,
                         wo_ref, out_ref, cst_ref, rst_ref, tail_ref, state_ref, decay_ref,
                         *, LT, L, H, DK, DV, DC, D, log_gamma):
    c = pl.program_id(1)
    DR = H * DK

    @pl.when(c == 0)
    def _init():
        tail_ref[...] = jnp.zeros_like(tail_ref)
        state_ref[...] = jnp.zeros_like(state_ref)
        diff = (lax.broadcasted_iota(jnp.int32, (L, L), 0)
                - lax.broadcasted_iota(jnp.int32, (L, L), 1)).astype(F32)
        for h in range(H):
            decay_ref[h] = jnp.where(diff >= 0.0,
                                     jnp.exp(jnp.maximum(diff, 0.0) * log_gamma[h]), 0.0)

    x = x_ref[...]
    xn = _rms(x, g_ref[...]).astype(BF16)

    zc = _dot(xn, win_ref[:, 0:3 * DC])
    cb, u = zc[:, 0:DC], zc[:, DC:2 * DC] * zc[:, 2 * DC:3 * DC]
    tail = tail_ref[...]
    p1, p2 = tail[7:8, :], tail[6:7, :]
    row = lax.broadcasted_iota(jnp.int32, (LT, DC), 0)
    u1 = jnp.where(row == 0, p1, pltpu.roll(u, 1, axis=0))
    u2 = jnp.where(row == 0, p2, jnp.where(row == 1, p1, pltpu.roll(u, 2, axis=0)))
    cw = cw_ref[...]
    conv = cw[0:1, :] * u2 + cw[1:2, :] * u1 + cw[2:3, :] * u
    tail_ref[...] = u[LT - 8:LT, :]
    a = _dot((cb * conv).astype(BF16), wco_ref[...])

    zr = _dot(xn, win_ref[:, 3 * DC:3 * DC + 4 * DR])
    ridx_k = lax.broadcasted_iota(jnp.int32, (L, DK), 0).astype(F32)
    ridx_v = lax.broadcasted_iota(jnp.int32, (L, DV), 0).astype(F32)
    subs = []
    for s in range(LT // L):
        rows = slice(s * L, (s + 1) * L)
        cos, sin = cos_ref[rows, :], sin_ref[rows, :]
        heads = []
        for h in range(H):
            lg = log_gamma[h]
            qh = zr[rows, h * DK:(h + 1) * DK]
            kh = zr[rows, DR + h * DK:DR + (h + 1) * DK]
            vh = zr[rows, 2 * DR + h * DV:2 * DR + (h + 1) * DV]
            gh = zr[rows, 3 * DR + h * DV:3 * DR + (h + 1) * DV]
            qh = qh * cos + pltpu.roll(qh, DK // 2, axis=1) * sin
            kh = (kh * cos + pltpu.roll(kh, DK // 2, axis=1) * sin) * (DK ** -0.5)
            qb, vb = qh.astype(BF16), vh.astype(BF16)
            s_prev = state_ref[h]
            scores = _dot_nt(qb, kh.astype(BF16)) * decay_ref[h]
            o = _dot(scores.astype(BF16), vb)
            o = o + _dot(qb, s_prev.astype(BF16)) * jnp.exp((ridx_v + 1.0) * lg)
            k_dec = jnp.exp((L - 1.0 - ridx_k) * lg)
            state_ref[h] = math.exp(L * lg) * s_prev + _dot_tn((kh * k_dec).astype(BF16), vb)
            mu = jnp.mean(o, axis=-1, keepdims=True)
            d = o - mu
            var = jnp.mean(d * d, axis=-1, keepdims=True)
            heads.append(_silu(gh) * (d * lax.rsqrt(var + EPS)))
        subs.append(jnp.concatenate(heads, axis=-1).astype(BF16))
    r = _dot(jnp.concatenate(subs, axis=0), wro_ref[...])

    zg = _dot(xn, win_ref[:, 3 * DC + 4 * DR:3 * DC + 4 * DR + 2 * D])
    m = _sigmoid(zg[:, 0:D]) * a + _sigmoid(zg[:, D:2 * D]) * r
    out_ref[...] = x + _dot(m.astype(BF16), wo_ref[...])

    @pl.when(c == pl.num_programs(1) - 1)
    def _final():
        cst_ref[...] = u[LT - 8:LT, :]
        rst_ref[...] = state_ref[...]


def _mixer_prompt(x, cos, sin, g, w_in_all, layer, cw, w_co, w_ro, w_o, *, H, DK, DV):
    B, T, D = x.shape
    DC = cw.shape[1]
    LT, L = MIX_CHUNK, RET_SUB
    const = lambda *shape: pl.BlockSpec(shape, lambda b, c: (0,) * len(shape))
    kern = functools.partial(_mixer_prompt_kernel, LT=LT, L=L, H=H, DK=DK, DV=DV, DC=DC, D=D,
                             log_gamma=_log_gammas(H))
    return pl.pallas_call(
        kern,
        grid=(B, T // LT),
        in_specs=[
            pl.BlockSpec((None, LT, D), lambda b, c: (b, c, 0)),
            pl.BlockSpec((LT, DK), lambda b, c: (c, 0)),
            pl.BlockSpec((LT, DK), lambda b, c: (c, 0)),
            const(1, D),
            pl.BlockSpec((None,) + w_in_all.shape[1:], lambda b, c: (layer, 0, 0)),
            const(*cw.shape), const(*w_co.shape),
            const(*w_ro.shape), const(*w_o.shape),
        ],
        out_specs=[
            pl.BlockSpec((None, LT, D), lambda b, c: (b, c, 0)),
            pl.BlockSpec((None, 8, DC), lambda b, c: (b, 0, 0)),
            pl.BlockSpec((None, H, DK, DV), lambda b, c: (b, 0, 0, 0)),
        ],
        out_shape=(jax.ShapeDtypeStruct((B, T, D), F32),
                   jax.ShapeDtypeStruct((B, 8, DC), F32),
                   jax.ShapeDtypeStruct((B, H, DK, DV), F32)),
        scratch_shapes=[pltpu.VMEM((8, DC), F32), pltpu.VMEM((H, DK, DV), F32),
                        pltpu.VMEM((H, L, L), F32)],
        compiler_params=_cparams(("arbitrary", "arbitrary")),
        name="mixer_prompt",
    )(x, cos, sin, g, w_in_all, cw, w_co, w_ro, w_o)


def _dec_inproj_kernel(x_ref, g_ref, win_ref, wqkt_ref, invt_ref, zm_ref, zg_ref, qkt_ref,
                       *, H, DK, n_main, pos0):
    xn = _rms(x_ref[...], g_ref[...]).astype(BF16)
    zm_ref[...] = _dot(xn, win_ref[:, 0:n_main])
    zg_ref[...] = _dot(xn, win_ref[:, n_main:win_ref.shape[1]])
    qkt = _dot_nt(wqkt_ref[...], xn)
    ang = pos0 * invt_ref[...]
    cos, sin = jnp.cos(ang), jnp.sin(ang)
    half = DK // 2
    for hh in range(2 * H):
        blk = qkt[hh * DK:(hh + 1) * DK, :]
        x1, x2 = blk[0:half, :], blk[half:DK, :]
        scale = 1.0 if hh < H else DK ** -0.5
        qkt_ref[hh * DK:hh * DK + half, :] = (x1 * cos[0:half] - x2 * sin[0:half]) * scale
        qkt_ref[hh * DK + half:(hh + 1) * DK, :] = (x2 * cos[half:DK] + x1 * sin[half:DK]) * scale


def _dec_inproj(x, g, w_in_all, layer, w_qkt, inv_t, n_main, *, H, DK):
    n, D = x.shape
    n_in = w_in_all.shape[2]
    kern = functools.partial(_dec_inproj_kernel, H=H, DK=DK, n_main=n_main,
                             pos0=float(PAST_LEN))
    whole = lambda a: pl.BlockSpec(a.shape, lambda i: (0,) * a.ndim)
    out_shape = (jax.ShapeDtypeStruct((n, n_main), F32),
                 jax.ShapeDtypeStruct((n, n_in - n_main), F32),
                 jax.ShapeDtypeStruct((w_qkt.shape[0], n), F32))
    return pl.pallas_call(
        kern,
        grid=(1,),
        in_specs=[whole(x), whole(g), pl.BlockSpec((None, D, n_in), lambda i: (layer, 0, 0)),
                  whole(w_qkt), whole(inv_t)],
        out_specs=[whole(s) for s in out_shape],
        out_shape=out_shape,
        compiler_params=_cparams(("arbitrary",)),
        name="dec_inproj",
    )(x, g, w_in_all, w_qkt, inv_t)


def _dec_state_kernel(zm_ref, qkt_ref, conv_ref, cw_ref, s_ref, *rest, BB, H, DK, DV, DC,
                      log_gamma):
    ain_ref, rin_ref, nconv_ref, ns_ref, o_scr = rest[-5:]
    DR = H * DK
    zm = zm_ref[...]
    cb, u = zm[:, 0:DC], zm[:, DC:2 * DC] * zm[:, 2 * DC:3 * DC]
    buf = conv_ref[...]
    b0, b1 = buf[:, 0:DC], buf[:, DC:2 * DC]
    cw = cw_ref[...]
    conv = cw[0:1, :] * b0 + cw[1:2, :] * b1 + cw[2:3, :] * u
    nconv_ref[:, 0:DC] = b1
    nconv_ref[:, DC:2 * DC] = u
    ain_ref[...] = cb * conv

    qkt = qkt_ref[...]
    v = zm[:, 3 * DC + 2 * DR:3 * DC + 3 * DR]
    g = zm[:, 3 * DC + 3 * DR:3 * DC + 4 * DR]
    for j in range(BB):
        for h in range(H):
            gamma = math.exp(log_gamma[h])
            qc = qkt[h * DK:(h + 1) * DK, j:j + 1]
            kc = qkt[DR + h * DK:DR + (h + 1) * DK, j:j + 1]
            s_prev = s_ref[j, h]
            vrow = v[j:j + 1, h * DV:(h + 1) * DV]
            qk = jnp.sum(qc * kc, axis=0, keepdims=True)
            inter = jnp.sum(qc * s_prev, axis=0, keepdims=True) * gamma
            o_scr[j:j + 1, h * DV:(h + 1) * DV] = qk * vrow + inter
            ns_ref[j, h] = gamma * s_prev + kc * vrow
    o = o_scr[...]
    for h in range(H):
        oh = o[:, h * DV:(h + 1) * DV]
        mu = jnp.mean(oh, axis=-1, keepdims=True)
        d = oh - mu
        var = jnp.mean(d * d, axis=-1, keepdims=True)
        rin_ref[:, h * DV:(h + 1) * DV] = _silu(g[:, h * DV:(h + 1) * DV]) * (d * lax.rsqrt(var + EPS))


def _dec_state(zm, qkt3, conv2d, cw, state_all, new_states, layer, *, H, DK, DV):
    n = zm.shape[0]
    depth = state_all.shape[0]
    DC = cw.shape[1]
    BB = DEC_BB
    kern = functools.partial(_dec_state_kernel, BB=BB, H=H, DK=DK, DV=DV, DC=DC,
                             log_gamma=_log_gammas(H))
    args = [zm, qkt3, conv2d, cw, state_all, new_states]
    in_specs = [
        pl.BlockSpec((BB, zm.shape[1]), lambda b: (b, 0)),
        pl.BlockSpec((None, qkt3.shape[1], BB), lambda b: (b, 0, 0)),
        pl.BlockSpec((BB, 2 * DC), lambda b: (b, 0)),
        pl.BlockSpec(cw.shape, lambda b: (0, 0)),
        pl.BlockSpec((None, BB, H, DK, DV), lambda b: (layer, b, 0, 0, 0)),
        pl.BlockSpec(memory_space=pl.ANY),
    ]
    aliases = {5: 3}
    return pl.pallas_call(
        kern,
        grid=(n // BB,),
        in_specs=in_specs,
        out_specs=[
            pl.BlockSpec((BB, DC), lambda b: (b, 0)),
            pl.BlockSpec((BB, H * DV), lambda b: (b, 0)),
            pl.BlockSpec((BB, 2 * DC), lambda b: (b, 0)),
            pl.BlockSpec((None, BB, H, DK, DV), lambda b: (layer, b, 0, 0, 0)),
        ],
        out_shape=(jax.ShapeDtypeStruct((n, DC), F32),
                   jax.ShapeDtypeStruct((n, H * DV), F32),
                   jax.ShapeDtypeStruct((n, 2 * DC), F32),
                   jax.ShapeDtypeStruct((depth, n, H, DK, DV), F32)),
        scratch_shapes=[pltpu.VMEM((BB, H * DV), F32)],
        input_output_aliases=aliases,
        compiler_params=_cparams(("arbitrary",)),
        name="dec_state",
    )(*args)


def _dec_outproj_kernel(x_ref, zg_ref, ain_ref, rin_ref, wco_ref, wro_ref, wo_ref, out_ref, *, D):
    a = _dot(ain_ref[...].astype(BF16), wco_ref[...])
    r = _dot(rin_ref[...].astype(BF16), wro_ref[...])
    zg = zg_ref[...]
    m = _sigmoid(zg[:, 0:D]) * a + _sigmoid(zg[:, D:2 * D]) * r
    out_ref[...] = x_ref[...] + _dot(m.astype(BF16), wo_ref[...])


def _dec_outproj(x, zg, ain, rin, w_co, w_ro, w_o):
    n, D = x.shape
    return pl.pallas_call(
        functools.partial(_dec_outproj_kernel, D=D),
        out_shape=jax.ShapeDtypeStruct((n, D), F32),
        compiler_params=pltpu.CompilerParams(vmem_limit_bytes=VMEM_LIMIT),
        name="dec_outproj",
    )(x, zg, ain, rin, w_co, w_ro, w_o)


def _ffn_dense_kernel(x_ref, g_ref, wg_ref, wu_ref, wd_ref, out_ref, hn_ref):
    @pl.when(pl.program_id(1) == 0)
    def _init():
        x = x_ref[...]
        hn_ref[...] = _rms(x, g_ref[...]).astype(BF16)
        out_ref[...] = x

    hn = hn_ref[...]
    act = (_silu(_dot(hn, wg_ref[...].astype(BF16)))
           * _dot(hn, wu_ref[...].astype(BF16))).astype(BF16)
    out_ref[...] += _dot(act, wd_ref[...].astype(BF16))


def _ffn_dense(x, g, w_gu_all, w_down_all, layer):
    n, D = x.shape
    FF = w_down_all.shape[1]
    tm = min(FFN_TM, n)
    fc = FFN_FC
    nf = FF // fc
    return pl.pallas_call(
        _ffn_dense_kernel,
        grid=(n // tm, nf),
        in_specs=[
            pl.BlockSpec((tm, D), lambda i, j: (i, 0)),
            pl.BlockSpec((1, D), lambda i, j: (0, 0)),
            pl.BlockSpec((None, D, fc), lambda i, j: (layer, 0, j)),
            pl.BlockSpec((None, D, fc), lambda i, j: (layer, 0, nf + j)),
            pl.BlockSpec((None, fc, D), lambda i, j: (layer, j, 0)),
        ],
        out_specs=pl.BlockSpec((tm, D), lambda i, j: (i, 0)),
        out_shape=jax.ShapeDtypeStruct((n, D), F32),
        scratch_shapes=[pltpu.VMEM((tm, D), BF16)],
        compiler_params=_cparams(("arbitrary", "arbitrary")),
        name="ffn_dense",
    )(x, g, w_gu_all, w_gu_all, w_down_all)


def _router_kernel(x_ref, g_ref, wrt_ref, hn_ref, route_ref, cnt_ref, *, T, E, CH):
    hn = _rms(x_ref[...], g_ref[...])
    hn_ref[...] = hn
    logits = _dot_nt(wrt_ref[...], hn, precision=lax.Precision.HIGHEST)
    eidx = lax.broadcasted_iota(jnp.int32, (E, T), 0)
    m1 = jnp.max(logits, axis=0, keepdims=True)
    i1 = jnp.min(jnp.where(logits == m1, eidx, E), axis=0, keepdims=True)
    sel1 = eidx == i1
    rest = jnp.where(sel1, -jnp.inf, logits)
    m2 = jnp.max(rest, axis=0, keepdims=True)
    i2 = jnp.min(jnp.where(rest == m2, eidx, E), axis=0, keepdims=True)
    sel2 = eidx == i2
    e2 = jnp.exp(m2 - m1)
    w1 = 1.0 / (1.0 + e2)
    f1 = jnp.where(sel1, 1.0, 0.0)
    f2 = jnp.where(sel2, 1.0, 0.0)
    sel = f1 + f2
    selb = sel.astype(BF16)
    route_ref[0:1, :] = i1.astype(F32)
    route_ref[1:2, :] = i2.astype(F32)
    route_ref[4:5, :] = w1
    route_ref[5:6, :] = e2 * w1
    route_ref[6:8, :] = jnp.zeros((2, T), F32)
    src = lax.broadcasted_iota(jnp.int32, (CH, CH), 0)
    dst = lax.broadcasted_iota(jnp.int32, (CH, CH), 1)
    tri = jnp.where(src < dst, 1.0, 0.0).astype(BF16)
    cnt = jnp.zeros((E, 1), F32)
    for cidx in range(T // CH):
        cols = slice(cidx * CH, (cidx + 1) * CH)
        pos = _dot(selb[:, cols], tri) + cnt
        route_ref[2:3, cols] = jnp.sum(pos * f1[:, cols], axis=0, keepdims=True)
        route_ref[3:4, cols] = jnp.sum(pos * f2[:, cols], axis=0, keepdims=True)
        cnt = cnt + jnp.sum(sel[:, cols], axis=1, keepdims=True)
    cnt_ref[...] = jnp.broadcast_to(cnt, cnt_ref.shape).astype(jnp.int32)


def _router(x, g, w_router_t, T):
    n, D = x.shape
    E = w_router_t.shape[0]
    nt = n // T
    kern = functools.partial(_router_kernel, T=T, E=E, CH=min(256, T))
    return pl.pallas_call(
        kern,
        grid=(nt,),
        in_specs=[
            pl.BlockSpec((T, D), lambda i: (i, 0)),
            pl.BlockSpec((1, D), lambda i: (0, 0)),
            pl.BlockSpec((E, D), lambda i: (0, 0)),
        ],
        out_specs=[
            pl.BlockSpec((T, D), lambda i: (i, 0)),
            pl.BlockSpec((8, T), lambda i: (0, i)),
            pl.BlockSpec((None, E, 128), lambda i: (i, 0, 0)),
        ],
        out_shape=(jax.ShapeDtypeStruct((n, D), F32),
                   jax.ShapeDtypeStruct((8, n), F32),
                   jax.ShapeDtypeStruct((nt, E, 128), jnp.int32)),
        compiler_params=_cparams(("arbitrary",)),
        name="moe_router",
    )(x, g, w_router_t)


def _route_plan(counts, tm, n_tiles_max):
    E = counts.shape[1]
    tot = jnp.sum(counts, axis=0)
    tiles_e = (tot + (tm - 1)) // tm
    tile_end = jnp.cumsum(tiles_e)
    tile_start = tile_end - tiles_e
    base = (tile_start * tm)[None, :] + jnp.cumsum(counts, axis=0) - counts
    r = jnp.arange(n_tiles_max, dtype=jnp.int32)
    tile_e = jnp.minimum(jnp.sum(r[:, None] >= tile_end[None, :], axis=1), E - 1).astype(jnp.int32)
    n_used = tile_end[E - 1:E].astype(jnp.int32)
    valid = jnp.clip(tot[tile_e] - (r - tile_start[tile_e]) * tm, 0, tm)
    valid = jnp.where(r < n_used[0], valid, 0).astype(jnp.int32)
    return base.astype(jnp.int32), tile_e, valid, n_used


def _global_slots(route, base, T):
    nt, E = base.shape
    idx = route[0:TOP_K].astype(jnp.int32).reshape(TOP_K, nt, T)
    slot = route[TOP_K:2 * TOP_K].astype(jnp.int32).reshape(TOP_K, nt, T)
    onehot = idx[..., None] == jnp.arange(E, dtype=jnp.int32)
    start = jnp.sum(jnp.where(onehot, base[None, :, None, :], 0), axis=-1)
    return (start + slot).reshape(TOP_K, nt * T).T.reshape(-1)


SUBLANES = 8


def _for_each_row(n_rows, slots_ref, fn):
    def body(q, carry):
        first = q * (SUBLANES * TOP_K)
        for u in range(SUBLANES):
            fn(q, u, [slots_ref[first + (u * TOP_K + k)] for k in range(TOP_K)])
        return carry
    lax.fori_loop(0, n_rows // SUBLANES, body, 0)


def _dispatch_kernel(g_ref, hn_ref, xs_in_ref, xs_ref, sem, *, TD):
    del xs_in_ref

    def send(q, u, slots):
        for k, dst in enumerate(slots):
            pltpu.make_async_copy(hn_ref.at[q, pl.ds(u, 1)], xs_ref.at[pl.ds(dst, 1)],
                                  sem).start(priority=k % 2)
    _for_each_row(TD, g_ref, send)
    for k in range(TOP_K):
        pltpu.make_async_copy(hn_ref, hn_ref, sem).wait()


def _dispatch(g, hn, xs):
    n, D = hn.shape
    TD = min(MOE_TD, n)
    return pl.pallas_call(
        functools.partial(_dispatch_kernel, TD=TD),
        grid=(n // TD,),
        in_specs=[
            pl.BlockSpec((TOP_K * TD,), lambda i: (i,), memory_space=pltpu.SMEM),
            pl.BlockSpec((TD // SUBLANES, SUBLANES, D), lambda i: (i, 0, 0)),
            pl.BlockSpec(memory_space=pl.ANY),
        ],
        out_specs=pl.BlockSpec(memory_space=pl.ANY),
        out_shape=jax.ShapeDtypeStruct(xs.shape, xs.dtype),
        scratch_shapes=[pltpu.SemaphoreType.DMA(())],
        input_output_aliases={2: 0},
        compiler_params=_cparams(("arbitrary",)),
        name="moe_dispatch",
    )(g, hn.reshape(n // SUBLANES, SUBLANES, D), xs)


def _experts_kernel(te_ref, valid_ref, nused_ref, x_ref, wg_ref, wu_ref, wd_ref, y_ref,
                    *, TM, R, SUB):
    r, j = pl.program_id(0), pl.program_id(1)
    valid = jnp.where(r < nused_ref[0], valid_ref[r], 0)
    nch = lax.shift_right_logical(valid + (R - 1), R.bit_length() - 1)

    @pl.when((nch == 0) & (j == 0))
    def _idle():
        y_ref[...] = jnp.zeros_like(y_ref)

    def variant(rows):
        def run(first):
            wg, wu, wd = (w[...].astype(BF16) for w in (wg_ref, wu_ref, wd_ref))
            for lo in range(0, rows, SUB):
                hi = min(lo + SUB, rows)
                xs = x_ref[lo:hi, :].astype(BF16)
                y = _dot((_silu(_dot(xs, wg)) * _dot(xs, wu)).astype(BF16), wd)
                if first:
                    y_ref[lo:hi, :] = y
                else:
                    y_ref[lo:hi, :] += y
            if first and rows < TM:
                y_ref[rows:TM, :] = jnp.zeros((TM - rows, y_ref.shape[1]), F32)

        @pl.when((nch == rows // R) & (j == 0))
        def _first():
            run(True)

        @pl.when((nch == rows // R) & (j > 0))
        def _rest():
            run(False)

    for n in range(1, TM // R + 1):
        variant(n * R)


def _experts(tile_e, valid, n_used, xs, w_gu_all, w_down_all, layer):
    rows, D = xs.shape
    FF = w_down_all.shape[2]
    TM, R, fc = MOE_TM, MOE_ROWS, MOE_FC
    nf = FF // fc
    nt = rows // TM

    def tile_of(r, nu):
        return jnp.maximum(jnp.minimum(r, nu[0] - 1), 0)

    def chunk_of(r, j, nu):
        return jnp.where(r < nu[0], j, nf - 1)

    grid_spec = pltpu.PrefetchScalarGridSpec(
        num_scalar_prefetch=3,
        grid=(nt, nf),
        in_specs=[
            pl.BlockSpec((TM, D), lambda r, j, te, vl, nu: (tile_of(r, nu), 0)),
            pl.BlockSpec((None, None, D, fc),
                         lambda r, j, te, vl, nu: (layer, te[tile_of(r, nu)], 0, chunk_of(r, j, nu))),
            pl.BlockSpec((None, None, D, fc),
                         lambda r, j, te, vl, nu: (layer, te[tile_of(r, nu)], 0,
                                                   nf + chunk_of(r, j, nu))),
            pl.BlockSpec((None, None, fc, D),
                         lambda r, j, te, vl, nu: (layer, te[tile_of(r, nu)], chunk_of(r, j, nu), 0)),
        ],
        out_specs=pl.BlockSpec((TM, D), lambda r, j, te, vl, nu: (r, 0)),
    )
    return pl.pallas_call(
        functools.partial(_experts_kernel, TM=TM, R=R, SUB=MOE_SUB),
        grid_spec=grid_spec,
        out_shape=jax.ShapeDtypeStruct((rows, D), F32),
        compiler_params=_cparams(("arbitrary", "arbitrary")),
        name="moe_experts",
    )(tile_e, valid, n_used, xs, w_gu_all, w_gu_all, w_down_all)


def _ple_kernel(*refs, moe, final, tm):
    refs = list(refs)
    if moe:
        g_ref, gnext_ref, x_ref, w_ref, ys_ref = refs[0:5]
        refs = refs[5:]
        buf_ref, sem = refs[-2:]
        refs = refs[:-2]
    else:
        x_ref = refs.pop(0)
    p_ref, gn_ref, wgate_ref, wple_ref = refs[0:4]
    gfin_ref = refs[4] if final else None
    out_ref = refs[-1]

    if moe:
        i = pl.program_id(0)
        slot = lax.rem(i, 2)

        def fetch(slots_ref, s):
            def get(q, u, slots):
                for k, src in enumerate(slots):
                    pltpu.make_async_copy(ys_ref.at[pl.ds(src, 1)],
                                          buf_ref.at[s, k, q, pl.ds(u, 1)],
                                          sem.at[s]).start(priority=k % 2)
            _for_each_row(tm, slots_ref, get)

        @pl.when(i == 0)
        def _first():
            fetch(g_ref, 0)

        @pl.when(i + 1 < pl.num_programs(0))
        def _ahead():
            fetch(gnext_ref, 1 - slot)

        for k in range(TOP_K):
            pltpu.make_async_copy(buf_ref.at[slot, k], buf_ref.at[slot, k], sem.at[slot]).wait()

    n_parts = max(1, tm // PLE_PART)
    rows_per = tm // n_parts
    for part in range(n_parts):
        rows = slice(part * rows_per, (part + 1) * rows_per)
        x = x_ref[rows, :]
        if moe:
            groups = slice(part * rows_per // SUBLANES, (part + 1) * rows_per // SUBLANES)
            w = w_ref[rows, :]
            for k in range(TOP_K):
                x = x + w[:, k:k + 1] * buf_ref[slot, k, groups].reshape(x.shape)
        gate = _sigmoid(_dot(_rms(x, gn_ref[...]).astype(BF16), wgate_ref[...]))
        y = x + gate * _dot(p_ref[rows, :].astype(BF16), wple_ref[...])
        if final:
            y = _rms(y, gfin_ref[...])
        out_ref[rows, :] = y


def _ple(x, moe, p_all, layer, g, w_gate, w_ple, g_final):
    n, D = x.shape
    DP = p_all.shape[-1]
    tm = min(PLE_TM, n)
    tok = pl.BlockSpec((tm, D), lambda i: (i, 0))
    const = lambda *shape: pl.BlockSpec(shape, lambda i: (0,) * len(shape))
    args, specs, scratch = [], [], []
    if moe is not None:
        slots, weights, ys = moe
        last = n // tm - 1
        args += [slots, slots, x, weights, ys]
        specs += [pl.BlockSpec((TOP_K * tm,), lambda i: (i,), memory_space=pltpu.SMEM),
                  pl.BlockSpec((TOP_K * tm,), lambda i: (jnp.minimum(i + 1, last),),
                               memory_space=pltpu.SMEM),
                  tok, pl.BlockSpec((tm, TOP_K), lambda i: (i, 0)),
                  pl.BlockSpec(memory_space=pl.ANY)]
        scratch = [pltpu.VMEM((2, TOP_K, tm // SUBLANES, SUBLANES, D), F32),
                   pltpu.SemaphoreType.DMA((2,))]
    else:
        args.append(x)
        specs.append(tok)
    args += [p_all, g, w_gate, w_ple]
    specs += [pl.BlockSpec((None, tm, DP), lambda i: (layer, i, 0)), const(1, D),
              const(D, D), const(DP, D)]
    if g_final is not None:
        args.append(g_final)
        specs.append(const(1, D))
    kern = functools.partial(_ple_kernel, moe=moe is not None, final=g_final is not None, tm=tm)
    return pl.pallas_call(
        kern,
        grid=(n // tm,),
        in_specs=specs,
        out_specs=tok,
        out_shape=jax.ShapeDtypeStruct((n, D), F32),
        scratch_shapes=scratch,
        compiler_params=_cparams(("arbitrary",)),
        name="ple_moe" if moe is not None else "ple",
    )(*args)


def kernel(x_prompt, x_sample, state_conv, state_ret, p_prompt, p_sample, norm_mix_g, w_in,
           conv_w, w_conv_out, w_ret_out, w_o, norm_ffn_g, w_dense_gu, w_dense_down, w_router,
           w_exp_gu, w_exp_down, norm_ple_g, w_ple, w_ple_gate, norm_final_g):
    B, T, D = x_prompt.shape
    NP = B * T
    NS = x_sample.shape[0]
    depth = w_in.shape[0]
    _, _, H, DK, DV = state_ret.shape
    DC = conv_w.shape[-1]
    DR = H * DK
    E = w_router.shape[-1]
    assert x_sample.shape[1] == 1 and CONV_W - 1 == state_conv.shape[2]
    assert T % MIX_CHUNK == 0 and NS % DEC_BB == 0 and NP % ROUTE_TILE == 0

    half = DK // 2
    inv = ROPE_BASE ** (-jnp.arange(half, dtype=F32) / half)
    inv_dup = jnp.concatenate([inv, inv]).reshape(1, DK)
    sgn = jnp.concatenate([-jnp.ones((half,), F32), jnp.ones((half,), F32)]).reshape(1, DK)
    pos = jnp.arange(T, dtype=F32).reshape(T, 1)
    cos_p, sin_p = _rope_tables(pos, inv_dup, sgn)
    inv_t = jnp.broadcast_to(inv_dup.reshape(DK, 1), (DK, NS))

    n_tiles_max = (TOP_K * (NP + NS)) // MOE_TM + E

    row = lambda a: a.reshape(1, -1)
    xp = x_prompt
    xs = x_sample.reshape(NS, D)
    pp = p_prompt.reshape(depth, NP, -1)
    ps = p_sample.reshape(depth, NS, -1)
    conv_p, ret_p, conv_s = [], [], []
    sorted_x = None
    ret_s = jnp.zeros(state_ret.shape, F32)
    w_in_b = _to_bf16(w_in, 512)
    for i in range(depth):
        w_co_b = w_conv_out[i].astype(BF16)
        w_ro_b = w_ret_out[i].astype(BF16)
        w_o_b = w_o[i].astype(BF16)
        g_mix = row(norm_mix_g[i])

        xp, cst, rst = _mixer_prompt(xp, cos_p, sin_p, g_mix, w_in_b, i, conv_w[i], w_co_b,
                                     w_ro_b, w_o_b, H=H, DK=DK, DV=DV)
        conv_p.append(cst[:, 8 - (CONV_W - 1):, :])
        ret_p.append(rst)

        n_main = 3 * DC + 4 * DR
        w_qkt = w_in_b[i, :, 3 * DC:3 * DC + 2 * DR].T
        zm, zg, qkt = _dec_inproj(xs, g_mix, w_in_b, i, w_qkt, inv_t, n_main, H=H, DK=DK)
        qkt3 = qkt.reshape(2 * DR, NS // DEC_BB, DEC_BB).transpose(1, 0, 2)
        ain, rin, ncv, ret_s = _dec_state(zm, qkt3, state_conv[i].reshape(NS, -1), conv_w[i],
                                          state_ret, ret_s, i, H=H, DK=DK, DV=DV)
        xs = _dec_outproj(xs, zg, ain, rin, w_co_b, w_ro_b, w_o_b)
        conv_s.append(ncv.reshape(NS, CONV_W - 1, DC))

        g_ffn = row(norm_ffn_g[i])
        xp2 = xp.reshape(NP, D)
        if i % 2 == 0:
            xp2 = _ffn_dense(xp2, g_ffn, w_dense_gu, w_dense_down, i // 2)
            xs = _ffn_dense(xs, g_ffn, w_dense_gu, w_dense_down, i // 2)
            moe_p = moe_s = None
        else:
            w_rt = w_router[i // 2].T
            hn_p, route_p, cnt_p = _router(xp2, g_ffn, w_rt, ROUTE_TILE)
            hn_s, route_s, cnt_s = _router(xs, g_ffn, w_rt, NS)
            counts = jnp.concatenate([cnt_p[:, :, 0], cnt_s[:, :, 0]], axis=0)
            base, tile_e, valid, n_used = _route_plan(counts, MOE_TM, n_tiles_max)
            ntp = NP // ROUTE_TILE
            slots_p = _global_slots(route_p, base[:ntp], ROUTE_TILE)
            slots_s = _global_slots(route_s, base[ntp:], NS)
            if sorted_x is None:
                sorted_x = jnp.zeros((n_tiles_max * MOE_TM, D), F32)
            sorted_x = _dispatch(slots_p, hn_p, sorted_x)
            sorted_x = _dispatch(slots_s, hn_s, sorted_x)
            sorted_y = _experts(tile_e, valid, n_used, sorted_x, w_exp_gu, w_exp_down, i // 2)
            moe_p = (slots_p, route_p[4:6].T, sorted_y)
            moe_s = (slots_s, route_s[4:6].T, sorted_y)

        g_fin = row(norm_final_g) if i == depth - 1 else None
        w_pg_b = w_ple_gate[i].astype(BF16)
        w_pl_b = w_ple[i].astype(BF16)
        xp = _ple(xp2, moe_p, pp, i, row(norm_ple_g[i]), w_pg_b, w_pl_b, g_fin).reshape(B, T, D)
        xs = _ple(xs, moe_s, ps, i, row(norm_ple_g[i]), w_pg_b, w_pl_b, g_fin)

    return (xp, xs.reshape(NS, 1, D), jnp.stack(conv_p), jnp.stack(ret_p),
            jnp.stack(conv_s), ret_s)
```

```python
import functools
import math

import jax
import jax.numpy as jnp
from jax import lax
from jax.experimental import pallas as pl
from jax.experimental.pallas import tpu as pltpu

F32 = jnp.float32
BF16 = jnp.bfloat16

CONV_W = 3
TOP_K = 2
ROPE_BASE = 10000.0
EPS = 1e-6
PAST_LEN = 16384

V7X_VMEM_BYTES = 64 * 1024 * 1024
VMEM_LIMIT = V7X_VMEM_BYTES - 8 * 1024 * 1024

MIX_CHUNK = 1024
RET_SUB = 256
FFN_TM = 1024
FFN_FC = 512
ROUTE_TILE = 2048
MOE_TM = 1024
MOE_ROWS = 256
MOE_SUB = 1024
MOE_FC = 512
MOE_TD = 2048
PLE_TM = 1024
PLE_MOE_TM = 512
PLE_PART = 256
DEC_BB = 16


def _cparams(sem):
    return pltpu.CompilerParams(dimension_semantics=sem, vmem_limit_bytes=VMEM_LIMIT)


def _dot(a, b):
    return jnp.dot(a, b, preferred_element_type=F32)


def _dot_nt(a, b, precision=None):
    return lax.dot_general(a, b, (((1,), (1,)), ((), ())), precision=precision,
                           preferred_element_type=F32)


def _dot_tn(a, b):
    return lax.dot_general(a, b, (((0,), (0,)), ((), ())), preferred_element_type=F32)


def _rms(x, g):
    return x * lax.rsqrt(jnp.mean(x * x, axis=-1, keepdims=True) + EPS) * g


def _sigmoid(x):
    return 1.0 / (1.0 + jnp.exp(-x))


def _silu(x):
    return x * _sigmoid(x)


def _log_gammas(n_heads):
    return tuple(math.log(1.0 - 2.0 ** (-5.0 - h)) for h in range(n_heads))


def _cast_kernel(w_ref, o_ref):
    o_ref[...] = w_ref[...].astype(o_ref.dtype)


def _to_bf16(w, bn):
    depth, K, N = w.shape
    spec = pl.BlockSpec((None, K, bn), lambda l, j: (l, 0, j))
    return pl.pallas_call(
        _cast_kernel,
        grid=(depth, N // bn),
        in_specs=[spec],
        out_specs=spec,
        out_shape=jax.ShapeDtypeStruct(w.shape, BF16),
        compiler_params=_cparams(("arbitrary", "arbitrary")),
        name="cast_bf16",
    )(w)


def _rope_table_kernel(pos_ref, inv_ref, sgn_ref, cos_ref, sin_ref):
    ang = pos_ref[...] * inv_ref[...]
    cos_ref[...] = jnp.cos(ang)
    sin_ref[...] = jnp.sin(ang) * sgn_ref[...]


def _rope_tables(pos, inv_dup, sgn):
    n = pos.shape[0]
    dk = inv_dup.shape[1]
    return pl.pallas_call(
        _rope_table_kernel,
        out_shape=(jax.ShapeDtypeStruct((n, dk), F32), jax.ShapeDtypeStruct((n, dk), F32)),
        name="rope_tables",
    )(pos, inv_dup, sgn)


def _mixer_prompt_kernel(x_ref, cos_ref, sin_ref, g_ref, win_ref, cw_ref, wco_ref, wro_ref,
                         wo_ref, out_ref, cst_ref, rst_ref, tail_ref, state_ref, decay_ref,
                         *, LT, L, H, DK, DV, DC, D, log_gamma):
    c = pl.program_id(1)
    DR = H * DK

    @pl.when(c == 0)
    def _init():
        tail_ref[...] = jnp.zeros_like(tail_ref)
        state_ref[...] = jnp.zeros_like(state_ref)
        diff = (lax.broadcasted_iota(jnp.int32, (L, L), 0)
                - lax.broadcasted_iota(jnp.int32, (L, L), 1)).astype(F32)
        for h in range(H):
            decay_ref[h] = jnp.where(diff >= 0.0,
                                     jnp.exp(jnp.maximum(diff, 0.0) * log_gamma[h]), 0.0)

    x = x_ref[...]
    xn = _rms(x, g_ref[...]).astype(BF16)

    zc = _dot(xn, win_ref[:, 0:3 * DC])
    cb, u = zc[:, 0:DC], zc[:, DC:2 * DC] * zc[:, 2 * DC:3 * DC]
    tail = tail_ref[...]
    p1, p2 = tail[7:8, :], tail[6:7, :]
    row = lax.broadcasted_iota(jnp.int32, (LT, DC), 0)
    u1 = jnp.where(row == 0, p1, pltpu.roll(u, 1, axis=0))
    u2 = jnp.where(row == 0, p2, jnp.where(row == 1, p1, pltpu.roll(u, 2, axis=0)))
    cw = cw_ref[...]
    conv = cw[0:1, :] * u2 + cw[1:2, :] * u1 + cw[2:3, :] * u
    tail_ref[...] = u[LT - 8:LT, :]
    a = _dot((cb * conv).astype(BF16), wco_ref[...])

    zr = _dot(xn, win_ref[:, 3 * DC:3 * DC + 4 * DR])
    ridx_k = lax.broadcasted_iota(jnp.int32, (L, DK), 0).astype(F32)
    ridx_v = lax.broadcasted_iota(jnp.int32, (L, DV), 0).astype(F32)
    subs = []
    for s in range(LT // L):
        rows = slice(s * L, (s + 1) * L)
        cos, sin = cos_ref[rows, :], sin_ref[rows, :]
        heads = []
        for h in range(H):
            lg = log_gamma[h]
            qh = zr[rows, h * DK:(h + 1) * DK]
            kh = zr[rows, DR + h * DK:DR + (h + 1) * DK]
            vh = zr[rows, 2 * DR + h * DV:2 * DR + (h + 1) * DV]
            gh = zr[rows, 3 * DR + h * DV:3 * DR + (h + 1) * DV]
            qh = qh * cos + pltpu.roll(qh, DK // 2, axis=1) * sin
            kh = (kh * cos + pltpu.roll(kh, DK // 2, axis=1) * sin) * (DK ** -0.5)
            qb, vb = qh.astype(BF16), vh.astype(BF16)
            s_prev = state_ref[h]
            scores = _dot_nt(qb, kh.astype(BF16)) * decay_ref[h]
            o = _dot(scores.astype(BF16), vb)
            o = o + _dot(qb, s_prev.astype(BF16)) * jnp.exp((ridx_v + 1.0) * lg)
            k_dec = jnp.exp((L - 1.0 - ridx_k) * lg)
            state_ref[h] = math.exp(L * lg) * s_prev + _dot_tn((kh * k_dec).astype(BF16), vb)
            mu = jnp.mean(o, axis=-1, keepdims=True)
            d = o - mu
            var = jnp.mean(d * d, axis=-1, keepdims=True)
            heads.append(_silu(gh) * (d * lax.rsqrt(var + EPS)))
        subs.append(jnp.concatenate(heads, axis=-1).astype(BF16))
    r = _dot(jnp.concatenate(subs, axis=0), wro_ref[...])

    zg = _dot(xn, win_ref[:, 3 * DC + 4 * DR:3 * DC + 4 * DR + 2 * D])
    m = _sigmoid(zg[:, 0:D]) * a + _sigmoid(zg[:, D:2 * D]) * r
    out_ref[...] = x + _dot(m.astype(BF16), wo_ref[...])

    @pl.when(c == pl.num_programs(1) - 1)
    def _final():
        cst_ref[...] = u[LT - 8:LT, :]
        rst_ref[...] = state_ref[...]


def _mixer_prompt(x, cos, sin, g, w_in_all, layer, cw, w_co, w_ro, w_o, *, H, DK, DV):
    B, T, D = x.shape
    DC = cw.shape[1]
    LT, L = MIX_CHUNK, RET_SUB
    const = lambda *shape: pl.BlockSpec(shape, lambda b, c: (0,) * len(shape))
    kern = functools.partial(_mixer_prompt_kernel, LT=LT, L=L, H=H, DK=DK, DV=DV, DC=DC, D=D,
                             log_gamma=_log_gammas(H))
    return pl.pallas_call(
        kern,
        grid=(B, T // LT),
        in_specs=[
            pl.BlockSpec((None, LT, D), lambda b, c: (b, c, 0)),
            pl.BlockSpec((LT, DK), lambda b, c: (c, 0)),
            pl.BlockSpec((LT, DK), lambda b, c: (c, 0)),
            const(1, D),
            pl.BlockSpec((None,) + w_in_all.shape[1:], lambda b, c: (layer, 0, 0)),
            const(*cw.shape), const(*w_co.shape),
            const(*w_ro.shape), const(*w_o.shape),
        ],
        out_specs=[
            pl.BlockSpec((None, LT, D), lambda b, c: (b, c, 0)),
            pl.BlockSpec((None, 8, DC), lambda b, c: (b, 0, 0)),
            pl.BlockSpec((None, H, DK, DV), lambda b, c: (b, 0, 0, 0)),
        ],
        out_shape=(jax.ShapeDtypeStruct((B, T, D), F32),
                   jax.ShapeDtypeStruct((B, 8, DC), F32),
                   jax.ShapeDtypeStruct((B, H, DK, DV), F32)),
        scratch_shapes=[pltpu.VMEM((8, DC), F32), pltpu.VMEM((H, DK, DV), F32),
                        pltpu.VMEM((H, L, L), F32)],
        compiler_params=_cparams(("arbitrary", "arbitrary")),
        name="mixer_prompt",
    )(x, cos, sin, g, w_in_all, cw, w_co, w_ro, w_o)


def _dec_inproj_kernel(x_ref, g_ref, win_ref, wqkt_ref, invt_ref, zm_ref, zg_ref, qkt_ref,
                       *, H, DK, n_main, pos0):
    xn = _rms(x_ref[...], g_ref[...]).astype(BF16)
    zm_ref[...] = _dot(xn, win_ref[:, 0:n_main])
    zg_ref[...] = _dot(xn, win_ref[:, n_main:win_ref.shape[1]])
    qkt = _dot_nt(wqkt_ref[...], xn)
    ang = pos0 * invt_ref[...]
    cos, sin = jnp.cos(ang), jnp.sin(ang)
    half = DK // 2
    for hh in range(2 * H):
        blk = qkt[hh * DK:(hh + 1) * DK, :]
        x1, x2 = blk[0:half, :], blk[half:DK, :]
        scale = 1.0 if hh < H else DK ** -0.5
        qkt_ref[hh * DK:hh * DK + half, :] = (x1 * cos[0:half] - x2 * sin[0:half]) * scale
        qkt_ref[hh * DK + half:(hh + 1) * DK, :] = (x2 * cos[half:DK] + x1 * sin[half:DK]) * scale


def _dec_inproj(x, g, w_in_all, layer, w_qkt, inv_t, n_main, *, H, DK):
    n, D = x.shape
    n_in = w_in_all.shape[2]
    kern = functools.partial(_dec_inproj_kernel, H=H, DK=DK, n_main=n_main,
                             pos0=float(PAST_LEN))
    whole = lambda a: pl.BlockSpec(a.shape, lambda i: (0,) * a.ndim)
    out_shape = (jax.ShapeDtypeStruct((n, n_main), F32),
                 jax.ShapeDtypeStruct((n, n_in - n_main), F32),
                 jax.ShapeDtypeStruct((w_qkt.shape[0], n), F32))
    return pl.pallas_call(
        kern,
        grid=(1,),
        in_specs=[whole(x), whole(g), pl.BlockSpec((None, D, n_in), lambda i: (layer, 0, 0)),
                  whole(w_qkt), whole(inv_t)],
        out_specs=[whole(s) for s in out_shape],
        out_shape=out_shape,
        compiler_params=_cparams(("arbitrary",)),
        name="dec_inproj",
    )(x, g, w_in_all, w_qkt, inv_t)


def _dec_state_kernel(zm_ref, qkt_ref, conv_ref, cw_ref, s_ref, *rest, BB, H, DK, DV, DC,
                      log_gamma):
    ain_ref, rin_ref, nconv_ref, ns_ref, o_scr = rest[-5:]
    DR = H * DK
    zm = zm_ref[...]
    cb, u = zm[:, 0:DC], zm[:, DC:2 * DC] * zm[:, 2 * DC:3 * DC]
    buf = conv_ref[...]
    b0, b1 = buf[:, 0:DC], buf[:, DC:2 * DC]
    cw = cw_ref[...]
    conv = cw[0:1, :] * b0 + cw[1:2, :] * b1 + cw[2:3, :] * u
    nconv_ref[:, 0:DC] = b1
    nconv_ref[:, DC:2 * DC] = u
    ain_ref[...] = cb * conv

    qkt = qkt_ref[...]
    v = zm[:, 3 * DC + 2 * DR:3 * DC + 3 * DR]
    g = zm[:, 3 * DC + 3 * DR:3 * DC + 4 * DR]
    for j in range(BB):
        for h in range(H):
            gamma = math.exp(log_gamma[h])
            qc = qkt[h * DK:(h + 1) * DK, j:j + 1]
            kc = qkt[DR + h * DK:DR + (h + 1) * DK, j:j + 1]
            s_prev = s_ref[j, h]
            vrow = v[j:j + 1, h * DV:(h + 1) * DV]
            qk = jnp.sum(qc * kc, axis=0, keepdims=True)
            inter = jnp.sum(qc * s_prev, axis=0, keepdims=True) * gamma
            o_scr[j:j + 1, h * DV:(h + 1) * DV] = qk * vrow + inter
            ns_ref[j, h] = gamma * s_prev + kc * vrow
    o = o_scr[...]
    for h in range(H):
        oh = o[:, h * DV:(h + 1) * DV]
        mu = jnp.mean(oh, axis=-1, keepdims=True)
        d = oh - mu
        var = jnp.mean(d * d, axis=-1, keepdims=True)
        rin_ref[:, h * DV:(h + 1) * DV] = _silu(g[:, h * DV:(h + 1) * DV]) * (d * lax.rsqrt(var + EPS))


def _dec_state(zm, qkt3, conv2d, cw, state_all, new_states, layer, *, H, DK, DV):
    n = zm.shape[0]
    depth = state_all.shape[0]
    DC = cw.shape[1]
    BB = DEC_BB
    kern = functools.partial(_dec_state_kernel, BB=BB, H=H, DK=DK, DV=DV, DC=DC,
                             log_gamma=_log_gammas(H))
    args = [zm, qkt3, conv2d, cw, state_all, new_states]
    in_specs = [
        pl.BlockSpec((BB, zm.shape[1]), lambda b: (b, 0)),
        pl.BlockSpec((None, qkt3.shape[1], BB), lambda b: (b, 0, 0)),
        pl.BlockSpec((BB, 2 * DC), lambda b: (b, 0)),
        pl.BlockSpec(cw.shape, lambda b: (0, 0)),
        pl.BlockSpec((None, BB, H, DK, DV), lambda b: (layer, b, 0, 0, 0)),
        pl.BlockSpec(memory_space=pl.ANY),
    ]
    aliases = {5: 3}
    return pl.pallas_call(
        kern,
        grid=(n // BB,),
        in_specs=in_specs,
        out_specs=[
            pl.BlockSpec((BB, DC), lambda b: (b, 0)),
            pl.BlockSpec((BB, H * DV), lambda b: (b, 0)),
            pl.BlockSpec((BB, 2 * DC), lambda b: (b, 0)),
            pl.BlockSpec((None, BB, H, DK, DV), lambda b: (layer, b, 0, 0, 0)),
        ],
        out_shape=(jax.ShapeDtypeStruct((n, DC), F32),
                   jax.ShapeDtypeStruct((n, H * DV), F32),
                   jax.ShapeDtypeStruct((n, 2 * DC), F32),
                   jax.ShapeDtypeStruct((depth, n, H, DK, DV), F32)),
        scratch_shapes=[pltpu.VMEM((BB, H * DV), F32)],
        input_output_aliases=aliases,
        compiler_params=_cparams(("arbitrary",)),
        name="dec_state",
    )(*args)


def _dec_outproj_kernel(x_ref, zg_ref, ain_ref, rin_ref, wco_ref, wro_ref, wo_ref, out_ref, *, D):
    a = _dot(ain_ref[...].astype(BF16), wco_ref[...])
    r = _dot(rin_ref[...].astype(BF16), wro_ref[...])
    zg = zg_ref[...]
    m = _sigmoid(zg[:, 0:D]) * a + _sigmoid(zg[:, D:2 * D]) * r
    out_ref[...] = x_ref[...] + _dot(m.astype(BF16), wo_ref[...])


def _dec_outproj(x, zg, ain, rin, w_co, w_ro, w_o):
    n, D = x.shape
    return pl.pallas_call(
        functools.partial(_dec_outproj_kernel, D=D),
        out_shape=jax.ShapeDtypeStruct((n, D), F32),
        compiler_params=pltpu.CompilerParams(vmem_limit_bytes=VMEM_LIMIT),
        name="dec_outproj",
    )(x, zg, ain, rin, w_co, w_ro, w_o)


def _ffn_dense_kernel(x_ref, g_ref, wg_ref, wu_ref, wd_ref, out_ref, hn_ref):
    @pl.when(pl.program_id(1) == 0)
    def _init():
        x = x_ref[...]
        hn_ref[...] = _rms(x, g_ref[...]).astype(BF16)
        out_ref[...] = x

    hn = hn_ref[...]
    act = (_silu(_dot(hn, wg_ref[...].astype(BF16)))
           * _dot(hn, wu_ref[...].astype(BF16))).astype(BF16)
    out_ref[...] += _dot(act, wd_ref[...].astype(BF16))


def _ffn_dense(x, g, w_gu_all, w_down_all, layer):
    n, D = x.shape
    FF = w_down_all.shape[1]
    tm = min(FFN_TM, n)
    fc = FFN_FC
    nf = FF // fc
    return pl.pallas_call(
        _ffn_dense_kernel,
        grid=(n // tm, nf),
        in_specs=[
            pl.BlockSpec((tm, D), lambda i, j: (i, 0)),
            pl.BlockSpec((1, D), lambda i, j: (0, 0)),
            pl.BlockSpec((None, D, fc), lambda i, j: (layer, 0, j)),
            pl.BlockSpec((None, D, fc), lambda i, j: (layer, 0, nf + j)),
            pl.BlockSpec((None, fc, D), lambda i, j: (layer, j, 0)),
        ],
        out_specs=pl.BlockSpec((tm, D), lambda i, j: (i, 0)),
        out_shape=jax.ShapeDtypeStruct((n, D), F32),
        scratch_shapes=[pltpu.VMEM((tm, D), BF16)],
        compiler_params=_cparams(("arbitrary", "arbitrary")),
        name="ffn_dense",
    )(x, g, w_gu_all, w_gu_all, w_down_all)


def _router_kernel(x_ref, g_ref, wrt_ref, hn_ref, route_ref, cnt_ref, *, T, E, CH):
    hn = _rms(x_ref[...], g_ref[...])
    hn_ref[...] = hn
    logits = _dot_nt(wrt_ref[...], hn, precision=lax.Precision.HIGHEST)
    eidx = lax.broadcasted_iota(jnp.int32, (E, T), 0)
    m1 = jnp.max(logits, axis=0, keepdims=True)
    i1 = jnp.min(jnp.where(logits == m1, eidx, E), axis=0, keepdims=True)
    sel1 = eidx == i1
    rest = jnp.where(sel1, -jnp.inf, logits)
    m2 = jnp.max(rest, axis=0, keepdims=True)
    i2 = jnp.min(jnp.where(rest == m2, eidx, E), axis=0, keepdims=True)
    sel2 = eidx == i2
    e2 = jnp.exp(m2 - m1)
    w1 = 1.0 / (1.0 + e2)
    f1 = jnp.where(sel1, 1.0, 0.0)
    f2 = jnp.where(sel2, 1.0, 0.0)
    sel = f1 + f2
    selb = sel.astype(BF16)
    route_ref[0:1, :] = i1.astype(F32)
    route_ref[1:2, :] = i2.astype(F32)
    route_ref[4:5, :] = w1
    route_ref[5:6, :] = e2 * w1
    route_ref[6:8, :] = jnp.zeros((2, T), F32)
    src = lax.broadcasted_iota(jnp.int32, (CH, CH), 0)
    dst = lax.broadcasted_iota(jnp.int32, (CH, CH), 1)
    tri = jnp.where(src < dst, 1.0, 0.0).astype(BF16)
    cnt = jnp.zeros((E, 1), F32)
    for cidx in range(T // CH):
        cols = slice(cidx * CH, (cidx + 1) * CH)
        pos = _dot(selb[:, cols], tri) + cnt
        route_ref[2:3, cols] = jnp.sum(pos * f1[:, cols], axis=0, keepdims=True)
        route_ref[3:4, cols] = jnp.sum(pos * f2[:, cols], axis=0, keepdims=True)
        cnt = cnt + jnp.sum(sel[:, cols], axis=1, keepdims=True)
    cnt_ref[...] = jnp.broadcast_to(cnt, cnt_ref.shape).astype(jnp.int32)


def _router(x, g, w_router_t, T):
    n, D = x.shape
    E = w_router_t.shape[0]
    nt = n // T
    kern = functools.partial(_router_kernel, T=T, E=E, CH=min(256, T))
    return pl.pallas_call(
        kern,
        grid=(nt,),
        in_specs=[
            pl.BlockSpec((T, D), lambda i: (i, 0)),
            pl.BlockSpec((1, D), lambda i: (0, 0)),
            pl.BlockSpec((E, D), lambda i: (0, 0)),
        ],
        out_specs=[
            pl.BlockSpec((T, D), lambda i: (i, 0)),
            pl.BlockSpec((8, T), lambda i: (0, i)),
            pl.BlockSpec((None, E, 128), lambda i: (i, 0, 0)),
        ],
        out_shape=(jax.ShapeDtypeStruct((n, D), F32),
                   jax.ShapeDtypeStruct((8, n), F32),
                   jax.ShapeDtypeStruct((nt, E, 128), jnp.int32)),
        compiler_params=_cparams(("arbitrary",)),
        name="moe_router",
    )(x, g, w_router_t)


def _route_plan(counts, tm, n_tiles_max):
    E = counts.shape[1]
    tot = jnp.sum(counts, axis=0)
    tiles_e = (tot + (tm - 1)) // tm
    tile_end = jnp.cumsum(tiles_e)
    tile_start = tile_end - tiles_e
    base = (tile_start * tm)[None, :] + jnp.cumsum(counts, axis=0) - counts
    r = jnp.arange(n_tiles_max, dtype=jnp.int32)
    tile_e = jnp.minimum(jnp.sum(r[:, None] >= tile_end[None, :], axis=1), E - 1).astype(jnp.int32)
    n_used = tile_end[E - 1:E].astype(jnp.int32)
    valid = jnp.clip(tot[tile_e] - (r - tile_start[tile_e]) * tm, 0, tm)
    valid = jnp.where(r < n_used[0], valid, 0).astype(jnp.int32)
    return base.astype(jnp.int32), tile_e, valid, n_used


def _global_slots(route, base, T):
    nt, E = base.shape
    idx = route[0:TOP_K].astype(jnp.int32).reshape(TOP_K, nt, T)
    slot = route[TOP_K:2 * TOP_K].astype(jnp.int32).reshape(TOP_K, nt, T)
    onehot = idx[..., None] == jnp.arange(E, dtype=jnp.int32)
    start = jnp.sum(jnp.where(onehot, base[None, :, None, :], 0), axis=-1)
    return (start + slot).reshape(TOP_K, nt * T).T.reshape(-1)


SUBLANES = 8


def _for_each_row(n_rows, slots_ref, fn):
    def body(q, carry):
        first = q * (SUBLANES * TOP_K)
        for u in range(SUBLANES):
            fn(q, u, [slots_ref[first + (u * TOP_K + k)] for k in range(TOP_K)])
        return carry
    lax.fori_loop(0, n_rows // SUBLANES, body, 0)


def _dispatch_kernel(g_ref, hn_ref, xs_in_ref, xs_ref, sem, *, TD):
    del xs_in_ref

    def send(q, u, slots):
        for k, dst in enumerate(slots):
            pltpu.make_async_copy(hn_ref.at[q, pl.ds(u, 1)], xs_ref.at[pl.ds(dst, 1)],
                                  sem).start(priority=k % 2)
    _for_each_row(TD, g_ref, send)
    for k in range(TOP_K):
        pltpu.make_async_copy(hn_ref, hn_ref, sem).wait()


def _dispatch(g, hn, xs):
    n, D = hn.shape
    TD = min(MOE_TD, n)
    return pl.pallas_call(
        functools.partial(_dispatch_kernel, TD=TD),
        grid=(n // TD,),
        in_specs=[
            pl.BlockSpec((TOP_K * TD,), lambda i: (i,), memory_space=pltpu.SMEM),
            pl.BlockSpec((TD // SUBLANES, SUBLANES, D), lambda i: (i, 0, 0)),
            pl.BlockSpec(memory_space=pl.ANY),
        ],
        out_specs=pl.BlockSpec(memory_space=pl.ANY),
        out_shape=jax.ShapeDtypeStruct(xs.shape, xs.dtype),
        scratch_shapes=[pltpu.SemaphoreType.DMA(())],
        input_output_aliases={2: 0},
        compiler_params=_cparams(("arbitrary",)),
        name="moe_dispatch",
    )(g, hn.reshape(n // SUBLANES, SUBLANES, D), xs)


def _experts_kernel(te_ref, valid_ref, nused_ref, x_ref, wg_ref, wu_ref, wd_ref, y_ref,
                    *, TM, R, SUB):
    r, j = pl.program_id(0), pl.program_id(1)
    valid = jnp.where(r < nused_ref[0], valid_ref[r], 0)
    nch = lax.shift_right_logical(valid + (R - 1), R.bit_length() - 1)

    @pl.when((nch == 0) & (j == 0))
    def _idle():
        y_ref[...] = jnp.zeros_like(y_ref)

    def variant(rows):
        def run(first):
            wg, wu, wd = (w[...].astype(BF16) for w in (wg_ref, wu_ref, wd_ref))
            for lo in range(0, rows, SUB):
                hi = min(lo + SUB, rows)
                xs = x_ref[lo:hi, :].astype(BF16)
                y = _dot((_silu(_dot(xs, wg)) * _dot(xs, wu)).astype(BF16), wd)
                if first:
                    y_ref[lo:hi, :] = y
                else:
                    y_ref[lo:hi, :] += y
            if first and rows < TM:
                y_ref[rows:TM, :] = jnp.zeros((TM - rows, y_ref.shape[1]), F32)

        @pl.when((nch == rows // R) & (j == 0))
        def _first():
            run(True)

        @pl.when((nch == rows // R) & (j > 0))
        def _rest():
            run(False)

    for n in range(1, TM // R + 1):
        variant(n * R)


def _experts(tile_e, valid, n_used, xs, w_gu_all, w_down_all, layer):
    rows, D = xs.shape
    FF = w_down_all.shape[2]
    TM, R, fc = MOE_TM, MOE_ROWS, MOE_FC
    nf = FF // fc
    nt = rows // TM

    def tile_of(r, nu):
        return jnp.maximum(jnp.minimum(r, nu[0] - 1), 0)

    def chunk_of(r, j, nu):
        return jnp.where(r < nu[0], j, nf - 1)

    grid_spec = pltpu.PrefetchScalarGridSpec(
        num_scalar_prefetch=3,
        grid=(nt, nf),
        in_specs=[
            pl.BlockSpec((TM, D), lambda r, j, te, vl, nu: (tile_of(r, nu), 0)),
            pl.BlockSpec((None, None, D, fc),
                         lambda r, j, te, vl, nu: (layer, te[tile_of(r, nu)], 0, chunk_of(r, j, nu))),
            pl.BlockSpec((None, None, D, fc),
                         lambda r, j, te, vl, nu: (layer, te[tile_of(r, nu)], 0,
                                                   nf + chunk_of(r, j, nu))),
            pl.BlockSpec((None, None, fc, D),
                         lambda r, j, te, vl, nu: (layer, te[tile_of(r, nu)], chunk_of(r, j, nu), 0)),
        ],
        out_specs=pl.BlockSpec((TM, D), lambda r, j, te, vl, nu: (r, 0)),
    )
    return pl.pallas_call(
        functools.partial(_experts_kernel, TM=TM, R=R, SUB=MOE_SUB),
        grid_spec=grid_spec,
        out_shape=jax.ShapeDtypeStruct((rows, D), F32),
        compiler_params=_cparams(("arbitrary", "arbitrary")),
        name="moe_experts",
    )(tile_e, valid, n_used, xs, w_gu_all, w_gu_all, w_down_all)


def _ple_kernel(*refs, moe, final, tm):
    refs = list(refs)
    if moe:
        g_ref, gnext_ref, x_ref, w_ref, ys_ref = refs[0:5]
        refs = refs[5:]
        buf_ref, sem = refs[-2:]
        refs = refs[:-2]
    else:
        x_ref = refs.pop(0)
    p_ref, gn_ref, wgate_ref, wple_ref = refs[0:4]
    gfin_ref = refs[4] if final else None
    out_ref = refs[-1]

    if moe:
        i = pl.program_id(0)
        slot = lax.rem(i, 2)

        def fetch(slots_ref, s):
            def get(q, u, slots):
                for k, src in enumerate(slots):
                    pltpu.make_async_copy(ys_ref.at[pl.ds(src, 1)],
                                          buf_ref.at[s, k, q, pl.ds(u, 1)],
                                          sem.at[s]).start(priority=k % 2)
            _for_each_row(tm, slots_ref, get)

        @pl.when(i == 0)
        def _first():
            fetch(g_ref, 0)

        @pl.when(i + 1 < pl.num_programs(0))
        def _ahead():
            fetch(gnext_ref, 1 - slot)

    n_parts = max(1, tm // PLE_PART)
    rows_per = tm // n_parts
    pes = [_dot(p_ref[part * rows_per:(part + 1) * rows_per, :].astype(BF16), wple_ref[...])
           for part in range(n_parts)]
    if moe:
        for k in range(TOP_K):
            pltpu.make_async_copy(buf_ref.at[slot, k], buf_ref.at[slot, k], sem.at[slot]).wait()

    for part in range(n_parts):
        rows = slice(part * rows_per, (part + 1) * rows_per)
        x = x_ref[rows, :]
        if moe:
            groups = slice(part * rows_per // SUBLANES, (part + 1) * rows_per // SUBLANES)
            w = w_ref[rows, :]
            for k in range(TOP_K):
                x = x + w[:, k:k + 1] * buf_ref[slot, k, groups].reshape(x.shape)
        gate = _sigmoid(_dot(_rms(x, gn_ref[...]).astype(BF16), wgate_ref[...]))
        y = x + gate * pes[part]
        if final:
            y = _rms(y, gfin_ref[...])
        out_ref[rows, :] = y


def _ple(x, moe, p_all, layer, g, w_gate, w_ple, g_final):
    n, D = x.shape
    DP = p_all.shape[-1]
    tm = min(PLE_TM if moe is None else PLE_MOE_TM, n)
    tok = pl.BlockSpec((tm, D), lambda i: (i, 0))
    const = lambda *shape: pl.BlockSpec(shape, lambda i: (0,) * len(shape))
    args, specs, scratch = [], [], []
    if moe is not None:
        slots, weights, ys = moe
        last = n // tm - 1
        args += [slots, slots, x, weights, ys]
        specs += [pl.BlockSpec((TOP_K * tm,), lambda i: (i,), memory_space=pltpu.SMEM),
                  pl.BlockSpec((TOP_K * tm,), lambda i: (jnp.minimum(i + 1, last),),
                               memory_space=pltpu.SMEM),
                  tok, pl.BlockSpec((tm, TOP_K), lambda i: (i, 0)),
                  pl.BlockSpec(memory_space=pl.ANY)]
        scratch = [pltpu.VMEM((2, TOP_K, tm // SUBLANES, SUBLANES, D), F32),
                   pltpu.SemaphoreType.DMA((2,))]
    else:
        args.append(x)
        specs.append(tok)
    args += [p_all, g, w_gate, w_ple]
    specs += [pl.BlockSpec((None, tm, DP), lambda i: (layer, i, 0)), const(1, D),
              const(D, D), const(DP, D)]
    if g_final is not None:
        args.append(g_final)
        specs.append(const(1, D))
    kern = functools.partial(_ple_kernel, moe=moe is not None, final=g_final is not None, tm=tm)
    return pl.pallas_call(
        kern,
        grid=(n // tm,),
        in_specs=specs,
        out_specs=tok,
        out_shape=jax.ShapeDtypeStruct((n, D), F32),
        scratch_shapes=scratch,
        compiler_params=_cparams(("arbitrary",)),
        name="ple_moe" if moe is not None else "ple",
    )(*args)


def kernel(x_prompt, x_sample, state_conv, state_ret, p_prompt, p_sample, norm_mix_g, w_in,
           conv_w, w_conv_out, w_ret_out, w_o, norm_ffn_g, w_dense_gu, w_dense_down, w_router,
           w_exp_gu, w_exp_down, norm_ple_g, w_ple, w_ple_gate, norm_final_g):
    B, T, D = x_prompt.shape
    NP = B * T
    NS = x_sample.shape[0]
    depth = w_in.shape[0]
    _, _, H, DK, DV = state_ret.shape
    DC = conv_w.shape[-1]
    DR = H * DK
    E = w_router.shape[-1]
    assert x_sample.shape[1] == 1 and CONV_W - 1 == state_conv.shape[2]
    assert T % MIX_CHUNK == 0 and NS % DEC_BB == 0 and NP % ROUTE_TILE == 0

    half = DK // 2
    inv = ROPE_BASE ** (-jnp.arange(half, dtype=F32) / half)
    inv_dup = jnp.concatenate([inv, inv]).reshape(1, DK)
    sgn = jnp.concatenate([-jnp.ones((half,), F32), jnp.ones((half,), F32)]).reshape(1, DK)
    pos = jnp.arange(T, dtype=F32).reshape(T, 1)
    cos_p, sin_p = _rope_tables(pos, inv_dup, sgn)
    inv_t = jnp.broadcast_to(inv_dup.reshape(DK, 1), (DK, NS))

    n_tiles_max = (TOP_K * (NP + NS)) // MOE_TM + E

    row = lambda a: a.reshape(1, -1)
    xp = x_prompt
    xs = x_sample.reshape(NS, D)
    pp = p_prompt.reshape(depth, NP, -1)
    ps = p_sample.reshape(depth, NS, -1)
    conv_p, ret_p, conv_s = [], [], []
    sorted_x = None
    ret_s = jnp.zeros(state_ret.shape, F32)
    w_in_b = _to_bf16(w_in, 512)
    for i in range(depth):
        w_co_b = w_conv_out[i].astype(BF16)
        w_ro_b = w_ret_out[i].astype(BF16)
        w_o_b = w_o[i].astype(BF16)
        g_mix = row(norm_mix_g[i])

        xp, cst, rst = _mixer_prompt(xp, cos_p, sin_p, g_mix, w_in_b, i, conv_w[i], w_co_b,
                                     w_ro_b, w_o_b, H=H, DK=DK, DV=DV)
        conv_p.append(cst[:, 8 - (CONV_W - 1):, :])
        ret_p.append(rst)

        n_main = 3 * DC + 4 * DR
        w_qkt = w_in_b[i, :, 3 * DC:3 * DC + 2 * DR].T
        zm, zg, qkt = _dec_inproj(xs, g_mix, w_in_b, i, w_qkt, inv_t, n_main, H=H, DK=DK)
        qkt3 = qkt.reshape(2 * DR, NS // DEC_BB, DEC_BB).transpose(1, 0, 2)
        ain, rin, ncv, ret_s = _dec_state(zm, qkt3, state_conv[i].reshape(NS, -1), conv_w[i],
                                          state_ret, ret_s, i, H=H, DK=DK, DV=DV)
        xs = _dec_outproj(xs, zg, ain, rin, w_co_b, w_ro_b, w_o_b)
        conv_s.append(ncv.reshape(NS, CONV_W - 1, DC))

        g_ffn = row(norm_ffn_g[i])
        xp2 = xp.reshape(NP, D)
        if i % 2 == 0:
            xp2 = _ffn_dense(xp2, g_ffn, w_dense_gu, w_dense_down, i // 2)
            xs = _ffn_dense(xs, g_ffn, w_dense_gu, w_dense_down, i // 2)
            moe_p = moe_s = None
        else:
            w_rt = w_router[i // 2].T
            hn_p, route_p, cnt_p = _router(xp2, g_ffn, w_rt, ROUTE_TILE)
            hn_s, route_s, cnt_s = _router(xs, g_ffn, w_rt, NS)
            counts = jnp.concatenate([cnt_p[:, :, 0], cnt_s[:, :, 0]], axis=0)
            base, tile_e, valid, n_used = _route_plan(counts, MOE_TM, n_tiles_max)
            ntp = NP // ROUTE_TILE
            slots_p = _global_slots(route_p, base[:ntp], ROUTE_TILE)
            slots_s = _global_slots(route_s, base[ntp:], NS)
            if sorted_x is None:
                sorted_x = jnp.zeros((n_tiles_max * MOE_TM, D), F32)
            sorted_x = _dispatch(slots_p, hn_p, sorted_x)
            sorted_x = _dispatch(slots_s, hn_s, sorted_x)
            sorted_y = _experts(tile_e, valid, n_used, sorted_x, w_exp_gu, w_exp_down, i // 2)
            moe_p = (slots_p, route_p[4:6].T, sorted_y)
            moe_s = (slots_s, route_s[4:6].T, sorted_y)

        g_fin = row(norm_final_g) if i == depth - 1 else None
        w_pg_b = w_ple_gate[i].astype(BF16)
        w_pl_b = w_ple[i].astype(BF16)
        xp = _ple(xp2, moe_p, pp, i, row(norm_ple_g[i]), w_pg_b, w_pl_b, g_fin).reshape(B, T, D)
        xs = _ple(xs, moe_s, ps, i, row(norm_ple_g[i]), w_pg_b, w_pl_b, g_fin)

    return (xp, xs.reshape(NS, 1, D), jnp.stack(conv_p), jnp.stack(ret_p),
            jnp.stack(conv_s), ret_s)
```
